```python
import math
import jax, jax.numpy as jnp
from jax import lax
import numpy as np

D_MODEL = 1024
BATCH = 8
SEQ = 2048
DEPTH = 2
DEC_BATCH = 32
DEC_SEQ = 1
PAST_LEN = 8192
PAGE_SIZE = 128

N_MEM = 256
EPS = 1e-6
NEG_INF = -1e30

R_HEADS = 4
R_DK = 64
R_DV = 128
R_CHUNK = 128
G_GROUPS = 4
G_DIM = 128
G_CHUNK = 128
C_HEADS = 4
C_DQK = 64
C_DV = 128
C_QBLOCK = 128
M_HEADS = 4
M_DH = 128

N_BRANCH = 4
BRANCH_W = 512
D_FF = -(-(8 * D_MODEL) // (3 * 256)) * 256

IN_SPLITS = (R_HEADS * R_DK, R_HEADS * R_DK, R_HEADS * R_DV, R_HEADS * R_DV,
             G_GROUPS * G_DIM, G_GROUPS * G_DIM,
             C_HEADS * 2 * C_DQK, C_HEADS * 2 * C_DQK, C_HEADS * C_DV,
             M_HEADS * M_DH,
             N_BRANCH * D_MODEL)
D_IN = sum(IN_SPLITS)
IN_OFFSETS = tuple(int(o) for o in np.cumsum(IN_SPLITS)[:-1])

kernel_name = 'hybrid_retention_chunkmlp_diffattn_step'


def rms_norm(x, g):
    xf = x.astype(jnp.float32)
    y = xf * lax.rsqrt(jnp.mean(jnp.square(xf), axis=-1, keepdims=True) + EPS)
    return (y * g.astype(jnp.float32)).astype(x.dtype)


def layer_norm(x, g):
    xf = x.astype(jnp.float32)
    xc = xf - jnp.mean(xf, axis=-1, keepdims=True)
    y = xc * lax.rsqrt(jnp.mean(jnp.square(xc), axis=-1, keepdims=True) + EPS)
    return (y * g.astype(jnp.float32)).astype(x.dtype)


def alibi_slopes(n_heads):
    return jnp.asarray([2.0 ** (-8.0 * (h + 1) / n_heads) for h in range(n_heads)], jnp.float32)


def retention_log_decay():
    return jnp.log1p(-jnp.asarray([2.0 ** (-5 - h) for h in range(R_HEADS)], jnp.float32))


def retention_chunk(state, q, k, v, log_g):
    L = q.shape[1]
    qf, kf, vf = q.astype(jnp.float32), k.astype(jnp.float32), v.astype(jnp.float32)
    idx = jnp.arange(L, dtype=jnp.float32)
    diff = idx[:, None] - idx[None, :]
    decay = jnp.where(diff >= 0, jnp.exp(log_g[:, None, None] * jnp.maximum(diff, 0.0)), 0.0)
    scores = jnp.einsum('bihd,bjhd->bhij', qf, kf) * decay
    intra = jnp.einsum('bhij,bjhe->bihe', scores, vf)
    q_decay = jnp.exp(log_g[None, :] * (idx[:, None] + 1.0))
    cross = jnp.einsum('bihd,bhde->bihe', qf * q_decay[None, :, :, None], state)
    k_decay = jnp.exp(log_g[None, :] * (L - 1.0 - idx[:, None]))
    new_state = (state * jnp.exp(log_g * L)[None, :, None, None]
                 + jnp.einsum('bjhd,bjhe->bhde', kf * k_decay[None, :, :, None], vf))
    return new_state, intra + cross


def retention_out(o, gate, g_gn):
    B, L = o.shape[:2]
    o = layer_norm(o, g_gn).reshape(B, L, -1)
    return jax.nn.silu(gate) * o.astype(gate.dtype)


def chunk_mlp_prompt(u, v, w_s, b_s):
    B, S, _ = u.shape
    vc = v.reshape(B, S // G_CHUNK, G_CHUNK, G_GROUPS, G_DIM)
    mix = jnp.einsum('gts,bnsgd->bntgd', jnp.tril(w_s), vc) + b_s.T[:, :, None]
    return u * mix.reshape(B, S, -1)


def chunk_mlp_sample(u, v, w_s, b_s, pos):
    B, L, _ = u.shape
    off = pos % G_CHUNK
    blk = pos // G_CHUNK
    mask = (blk[:, None] == blk[None, :]) & (pos[:, None] >= pos[None, :])
    w = jnp.where(mask[None], w_s[:, off[:, None], off[None, :]], 0.0)
    mix = jnp.einsum('gts,bsgd->btgd', w, v.reshape(B, L, G_GROUPS, G_DIM)) + b_s[:, off].T[:, :, None]
    return u * mix.reshape(B, L, -1)


def diff_lambda(w_lam, lam_init):
    wl = w_lam.astype(jnp.float32)
    return jnp.exp(jnp.sum(wl[0] * wl[1])) - jnp.exp(jnp.sum(wl[2] * wl[3])) + lam_init


def diff_qk(q, k, g_qn, g_kn):
    B, L = q.shape[:2]
    q = rms_norm(q.reshape(B, L, C_HEADS, 2, C_DQK), g_qn) * (C_DQK ** -0.5)
    k = rms_norm(k.reshape(B, L, C_HEADS, 2, C_DQK), g_kn)
    return q, k


def diff_attend(q, k, v, q_pos, k_pos, lam):
    s = jnp.einsum('bqhcd,bkhcd->bchqk', q, k, preferred_element_type=jnp.float32)
    dist = (q_pos[:, None] - k_pos[None, :]).astype(jnp.float32)
    bias = -alibi_slopes(C_HEADS)[:, None, None] * jnp.abs(dist)
    s = jnp.where(dist >= 0, s + bias, NEG_INF)
    p = jax.nn.softmax(s, axis=-1)
    a = p[:, 0] - lam * p[:, 1]
    return jnp.einsum('bhqk,bkhe->bqhe', a, v.astype(jnp.float32))


def diff_out(o, g_hn, lam_init):
    B, L = o.shape[:2]
    return (rms_norm(o, g_hn) * (1.0 - lam_init)).reshape(B, L, -1)


def mem_kv(mem, g_mem, w_mem_kv, g_mk):
    B, N, _ = mem.shape
    k, v = jnp.split(rms_norm(mem, g_mem) @ w_mem_kv, 2, axis=-1)
    k = rms_norm(k.reshape(B, N, M_HEADS, M_DH), g_mk)
    return k, v.reshape(B, N, M_HEADS, M_DH)


def mem_attend(q, k, v, g_mq):
    B, L = q.shape[:2]
    q = rms_norm(q.reshape(B, L, M_HEADS, M_DH), g_mq) * (M_DH ** -0.5)
    s = jnp.einsum('bqhd,bnhd->bhqn', q, k, preferred_element_type=jnp.float32)
    p = jax.nn.softmax(s, axis=-1)
    return jnp.einsum('bhqn,bnhd->bqhd', p, v.astype(jnp.float32)).reshape(B, L, -1)


def merge_branches(x, branches, gate_logits, w_branch, w_out):
    B, L, _ = x.shape
    br = jnp.stack([b.astype(x.dtype) for b in branches], axis=2)
    proj = jnp.einsum('blnc,ncd->blnd', br, w_branch)
    gates = jax.nn.sigmoid(gate_logits.astype(jnp.float32)).reshape(B, L, N_BRANCH, D_MODEL)
    merged = jnp.sum(gates * proj, axis=2).astype(x.dtype)
    return merged @ w_out


def swiglu_ffn(x, g, w_in, w_out):
    gate, up = jnp.split(rms_norm(x, g) @ w_in, 2, axis=-1)
    return (jax.nn.silu(gate) * up) @ w_out


def setup_inputs(seed: int = 0) -> dict:
    key = jax.random.key(seed)
    keys = iter(jax.random.split(key, 32))

    def nrm(shape, scale=1.0):
        return jax.random.normal(next(keys), shape, jnp.float32) * scale

    def gain(shape):
        return 1.0 + nrm(shape, 0.02)

    n_pages = PAST_LEN // PAGE_SIZE
    n_phys = (DEC_BATCH * n_pages * 5) // 4
    page_table = jax.random.permutation(next(keys), n_phys)[: DEC_BATCH * n_pages]
    page_table = page_table.reshape(DEC_BATCH, n_pages).astype(jnp.int32)
    return {
        'x_prompt': nrm((BATCH, SEQ, D_MODEL)),
        'x_sample': nrm((DEC_BATCH, DEC_SEQ, D_MODEL)),
        'mem_prompt': nrm((BATCH, N_MEM, D_MODEL)),
        'cache_diff_k': nrm((DEPTH, n_phys, PAGE_SIZE, C_HEADS, 2, C_DQK)),
        'cache_diff_v': nrm((DEPTH, n_phys, PAGE_SIZE, C_HEADS, C_DV)),
        'page_table': page_table,
        'cache_mem_k': nrm((DEPTH, DEC_BATCH, N_MEM, M_HEADS, M_DH)),
        'cache_mem_v': nrm((DEPTH, DEC_BATCH, N_MEM, M_HEADS, M_DH)),
        'state_ret': nrm((DEPTH, DEC_BATCH, R_HEADS, R_DK, R_DV)),
        'norm_mix': gain((DEPTH, D_MODEL)),
        'w_in': nrm((DEPTH, D_MODEL, D_IN), D_MODEL ** -0.5),
        'ret_norm': gain((DEPTH, R_HEADS, R_DV)),
        'cmlp_norm': gain((DEPTH, G_GROUPS * G_DIM)),
        'cmlp_ws': nrm((DEPTH, G_GROUPS, G_CHUNK, G_CHUNK), G_CHUNK ** -0.5),
        'cmlp_bs': 1.0 + nrm((DEPTH, G_GROUPS, G_CHUNK), 0.1),
        'diff_qn': gain((DEPTH, C_DQK)),
        'diff_kn': gain((DEPTH, C_DQK)),
        'diff_lambda_w': nrm((DEPTH, 4, C_DQK), 0.1),
        'diff_hn': gain((DEPTH, C_HEADS, C_DV)),
        'mem_norm': gain((DEPTH, D_MODEL)),
        'w_mem_kv': nrm((DEPTH, D_MODEL, 2 * M_HEADS * M_DH), D_MODEL ** -0.5),
        'mem_qn': gain((DEPTH, M_DH)),
        'mem_kn': gain((DEPTH, M_DH)),
        'w_branch': nrm((DEPTH, N_BRANCH, BRANCH_W, D_MODEL), BRANCH_W ** -0.5),
        'w_out': nrm((DEPTH, D_MODEL, D_MODEL), D_MODEL ** -0.5),
        'norm_ffn': gain((DEPTH, D_MODEL)),
        'w_ffn_in': nrm((DEPTH, D_MODEL, 2 * D_FF), D_MODEL ** -0.5),
        'w_ffn_out': nrm((DEPTH, D_FF, D_MODEL), D_FF ** -0.5),
    }


def reference(x_prompt, x_sample, mem_prompt, cache_diff_k, cache_diff_v, page_table,
              cache_mem_k, cache_mem_v, state_ret,
              norm_mix, w_in, ret_norm, cmlp_norm, cmlp_ws, cmlp_bs,
              diff_qn, diff_kn, diff_lambda_w, diff_hn,
              mem_norm, w_mem_kv, mem_qn, mem_kn,
              w_branch, w_out, norm_ffn, w_ffn_in, w_ffn_out):
    f32 = jnp.float32
    log_g = retention_log_decay()
    B, S, _ = x_prompt.shape
    DB, L, _ = x_sample.shape
    past_len = page_table.shape[1] * cache_diff_k.shape[2]
    pos_p = jnp.arange(S, dtype=jnp.int32)
    pos_s = past_len + jnp.arange(L, dtype=jnp.int32)
    kpos_s = jnp.arange(past_len + L, dtype=jnp.int32)

    def to_chunks(t):
        return t.reshape(B, S // R_CHUNK, R_CHUNK, *t.shape[2:]).swapaxes(0, 1)

    def scan_step(st, qkv):
        return retention_chunk(st, qkv[0], qkv[1], qkv[2], log_g)

    xp, xs = x_prompt, x_sample
    pk, pv, pmk, pmv, pst = [], [], [], [], []
    sk, sv, sst, scv = [], [], [], []
    for l in range(DEPTH):
        lam_init = 0.8 - 0.6 * math.exp(-0.3 * l)
        lam = diff_lambda(diff_lambda_w[l], lam_init)

        hp = rms_norm(xp, norm_mix[l])
        rq, rk, rv, rg, gu, gv, cq, ck, cv, mq, gl = jnp.split(hp @ w_in[l], IN_OFFSETS, axis=-1)
        rq = rq.reshape(B, S, R_HEADS, R_DK)
        rk = rk.reshape(B, S, R_HEADS, R_DK) * (R_DK ** -0.5)
        rv = rv.reshape(B, S, R_HEADS, R_DV)
        st0 = jnp.zeros((B, R_HEADS, R_DK, R_DV), f32)
        st_p, o = lax.scan(scan_step, st0, (to_chunks(rq), to_chunks(rk), to_chunks(rv)))
        o_ret = retention_out(o.swapaxes(0, 1).reshape(B, S, R_HEADS, R_DV), rg, ret_norm[l])
        vn = layer_norm(gv, cmlp_norm[l])
        o_cm = chunk_mlp_prompt(gu, vn, cmlp_ws[l], cmlp_bs[l])
        q, k = diff_qk(cq, ck, diff_qn[l], diff_kn[l])
        v = cv.reshape(B, S, C_HEADS, C_DV)
        qb = q.reshape(B, S // C_QBLOCK, C_QBLOCK, C_HEADS, 2, C_DQK).swapaxes(0, 1)
        pb = pos_p.reshape(S // C_QBLOCK, C_QBLOCK)
        ob = lax.map(lambda a: diff_attend(a[0], k, v, a[1], pos_p, lam), (qb, pb))
        o_df = diff_out(ob.swapaxes(0, 1).reshape(B, S, C_HEADS, C_DV), diff_hn[l], lam_init)
        mk, mv = mem_kv(mem_prompt, mem_norm[l], w_mem_kv[l], mem_kn[l])
        o_mm = mem_attend(mq, mk, mv, mem_qn[l])
        xp = xp + merge_branches(xp, (o_ret, o_cm, o_df, o_mm), gl, w_branch[l], w_out[l])
        xp = xp + swiglu_ffn(xp, norm_ffn[l], w_ffn_in[l], w_ffn_out[l])
        pk.append(k)
        pv.append(v)
        pmk.append(mk)
        pmv.append(mv)
        pst.append(st_p.astype(x_prompt.dtype))

        hs = rms_norm(xs, norm_mix[l])
        rq, rk, rv, rg, gu, gv, cq, ck, cv, mq, gl = jnp.split(hs @ w_in[l], IN_OFFSETS, axis=-1)
        rq = rq.reshape(DB, L, R_HEADS, R_DK)
        rk = rk.reshape(DB, L, R_HEADS, R_DK) * (R_DK ** -0.5)
        rv = rv.reshape(DB, L, R_HEADS, R_DV)
        st_s, o = retention_chunk(state_ret[l].astype(f32), rq, rk, rv, log_g)
        o_ret = retention_out(o, rg, ret_norm[l])
        vn_s = layer_norm(gv, cmlp_norm[l])
        o_cm = chunk_mlp_sample(gu, vn_s, cmlp_ws[l], cmlp_bs[l], pos_s)
        q, k = diff_qk(cq, ck, diff_qn[l], diff_kn[l])
        v = cv.reshape(DB, L, C_HEADS, C_DV)
        k_past = cache_diff_k[l, page_table].reshape(DB, past_len, C_HEADS, 2, C_DQK)
        v_past = cache_diff_v[l, page_table].reshape(DB, past_len, C_HEADS, C_DV)
        k_all = jnp.concatenate([k_past, k.astype(k_past.dtype)], axis=1)
        v_all = jnp.concatenate([v_past, v.astype(v_past.dtype)], axis=1)
        o_df = diff_out(diff_attend(q, k_all, v_all, pos_s, kpos_s, lam), diff_hn[l], lam_init)
        o_mm = mem_attend(mq, cache_mem_k[l], cache_mem_v[l], mem_qn[l])
        xs = xs + merge_branches(xs, (o_ret, o_cm, o_df, o_mm), gl, w_branch[l], w_out[l])
        xs = xs + swiglu_ffn(xs, norm_ffn[l], w_ffn_in[l], w_ffn_out[l])
        sk.append(k)
        sv.append(v)
        sst.append(st_s.astype(state_ret.dtype))
        scv.append(vn_s)

    y_prompt = xp
    y_sample = xs
    new_diff_k_prompt = jnp.stack(pk)
    new_diff_v_prompt = jnp.stack(pv)
    new_mem_k_prompt = jnp.stack(pmk)
    new_mem_v_prompt = jnp.stack(pmv)
    new_state_ret_prompt = jnp.stack(pst)
    new_diff_k_sample = jnp.stack(sk)
    new_diff_v_sample = jnp.stack(sv)
    new_state_ret_sample = jnp.stack(sst)
    new_cmlp_v_sample = jnp.stack(scv)
    return (y_prompt, y_sample, new_diff_k_prompt, new_diff_v_prompt, new_mem_k_prompt, new_mem_v_prompt,
            new_state_ret_prompt, new_diff_k_sample, new_diff_v_sample, new_state_ret_sample, new_cmlp_v_sample)
```

```python
import functools
import math

import jax
import jax.numpy as jnp
from jax import lax
from jax.experimental import pallas as pl
from jax.experimental.pallas import tpu as pltpu

F32 = jnp.float32
BF16 = jnp.bfloat16

EPS = 1e-6
NEG_INF = -1e30

D_MODEL = 1024
SEG = 512
N_SEG = 9
HEADS = 4
R_DK = 64
HEAD_W = 128
CHUNK = 128
C_DQK = 64
M_DH = 128
N_BRANCH = 4
D_FF = 2816
FFN_CHUNKS = ((0, 1024), (1024, 2048), (2048, 2816))

VMEM_LIMIT_BYTES = 56 * 1024 * 1024


def _cparams(*sem):
    return pltpu.CompilerParams(dimension_semantics=sem, vmem_limit_bytes=VMEM_LIMIT_BYTES)


def _dot(a, b):
    return jnp.dot(a, b, preferred_element_type=F32)


def _dot_nt(a, b):
    return lax.dot_general(a, b, (((1,), (1,)), ((), ())), preferred_element_type=F32)


def _dot_tn(a, b):
    return lax.dot_general(a, b, (((0,), (0,)), ((), ())), preferred_element_type=F32)


def _sigmoid(x):
    return 1.0 / (1.0 + jnp.exp(-x))


def _rms(x, g):
    return x * lax.rsqrt(jnp.mean(x * x, axis=-1, keepdims=True) + EPS) * g


def _layer_norm(x, g):
    xc = x - jnp.mean(x, axis=-1, keepdims=True)
    return xc * lax.rsqrt(jnp.mean(xc * xc, axis=-1, keepdims=True) + EPS) * g


def _head_rms(y, g, group):
    lane = lax.broadcasted_iota(jnp.int32, (1, HEAD_W), 1)
    outs = []
    for hb in range(SEG // HEAD_W):
        blk = y[:, hb * HEAD_W:(hb + 1) * HEAD_W]
        sq = blk * blk
        if group == HEAD_W:
            ms = jnp.mean(sq, axis=-1, keepdims=True)
        else:
            lo = jnp.sum(jnp.where(lane < group, sq, 0.0), axis=-1, keepdims=True)
            hi = jnp.sum(jnp.where(lane < group, 0.0, sq), axis=-1, keepdims=True)
            ms = jnp.where(lane < group, lo, hi) * (1.0 / group)
        outs.append(blk * lax.rsqrt(ms + EPS))
    return jnp.concatenate(outs, axis=-1) * g


def _diff_lambda(wl, lam_init):
    a = jnp.sum(wl[0:1] * wl[1:2], axis=-1, keepdims=True)
    b = jnp.sum(wl[2:3] * wl[3:4], axis=-1, keepdims=True)
    return jnp.exp(a) - jnp.exp(b) + lam_init


def _const_spec(shape):
    nd = len(shape)
    return pl.BlockSpec(shape, lambda *_: (0,) * nd, pipeline_mode=pl.Buffered(1))


def _proj_kernel(x_ref, g_ref, w_ref, cn_ref, ws_ref, bst_ref, qn_ref, kn_ref, mqn_ref, *outs,
                 prompt, tm, off):
    if prompt:
        rqk_o, rv_o, rg_o, ocm_o, cq_o, ck_o, ckb_o, cv_o, cvb_o, mq_o = outs
    else:
        rqk_o, rv_o, rg_o, ocm_o, vn_o, cq_o, ck_o, cv_o, mq_o = outs
    hb = _rms(x_ref[...], g_ref[...]).astype(BF16)

    def seg(s):
        return _dot(hb, w_ref[s])

    lane = lax.broadcasted_iota(jnp.int32, (1, SEG), 1)
    y = seg(0) * jnp.where(lane < HEADS * R_DK, 1.0, R_DK ** -0.5)
    rqk_o[...] = y.astype(rqk_o.dtype)
    rv_o[...] = seg(1)
    rg_o[...] = seg(2)

    gu = seg(3)
    vn = _layer_norm(seg(4), cn_ref[...])
    if prompt:
        vnb = vn.astype(BF16)
        row = lax.broadcasted_iota(jnp.int32, (CHUNK, CHUNK), 0)
        col = lax.broadcasted_iota(jnp.int32, (CHUNK, CHUNK), 1)
        for g in range(HEADS):
            gs = slice(g * HEAD_W, (g + 1) * HEAD_W)
            wt = jnp.where(row >= col, ws_ref[g], 0.0).astype(BF16)
            bcol = bst_ref[:, g:g + 1]
            for c in range(tm // CHUNK):
                cs = slice(c * CHUNK, (c + 1) * CHUNK)
                mix = _dot(wt, vnb[cs, gs]) + bcol
                ocm_o[cs, gs] = (gu[cs, gs] * mix).astype(BF16)
    else:
        vn_o[...] = vn
        for g in range(HEADS):
            gs = slice(g * HEAD_W, (g + 1) * HEAD_W)
            w00 = ws_ref[g][off:off + 1, off:off + 1]
            b0 = bst_ref[off:off + 1, g:g + 1]
            ocm_o[:, gs] = (gu[:, gs] * (w00 * vn[:, gs] + b0)).astype(BF16)

    cq_o[...] = (_head_rms(seg(5), qn_ref[...], C_DQK) * (C_DQK ** -0.5)).astype(BF16)
    ck = _head_rms(seg(6), kn_ref[...], C_DQK)
    ck_o[...] = ck
    cv = seg(7)
    cv_o[...] = cv
    if prompt:
        ckb_o[...] = ck.astype(BF16)
        cvb_o[...] = cv.astype(BF16)
    mq_o[...] = (_head_rms(seg(8), mqn_ref[...], M_DH) * (M_DH ** -0.5)).astype(BF16)


def _proj_in(x, g, w9, cn, ws, bst, qn, kn, mqn, *, prompt, off=0):
    m = x.shape[0]
    tm = min(512, m)
    row = lambda i: (i, 0)
    blk = pl.BlockSpec((tm, SEG), row)
    if prompt:
        dts = (BF16, F32, F32, BF16, BF16, F32, BF16, F32, BF16, BF16)
    else:
        dts = (F32, F32, F32, BF16, F32, BF16, F32, F32, BF16)
    return pl.pallas_call(
        functools.partial(_proj_kernel, prompt=prompt, tm=tm, off=off),
        grid=(m // tm,),
        in_specs=[pl.BlockSpec((tm, D_MODEL), row), _const_spec((1, D_MODEL)),
                  _const_spec((N_SEG, D_MODEL, SEG)), _const_spec((1, SEG)),
                  _const_spec((HEADS, CHUNK, CHUNK)), _const_spec((CHUNK, HEADS)),
                  _const_spec((1, SEG)), _const_spec((1, SEG)), _const_spec((1, SEG))],
        out_specs=[blk] * len(dts),
        out_shape=[jax.ShapeDtypeStruct((m, SEG), dt) for dt in dts],
        compiler_params=_cparams("parallel"),
        name="proj_in_prompt" if prompt else "proj_in_sample",
    )(x, g, w9, cn, ws, bst, qn, kn, mqn)


def _ret_log_decay(h):
    return math.log1p(-(2.0 ** (-5 - h)))


def _ret_finish(o, gate, gn):
    return gate * _sigmoid(gate) * _layer_norm(o, gn)


def _retention_kernel(rqk_ref, rv_ref, rg_ref, gn_ref, o_ref, st_o, st_ref, dec_ref, qd_ref, kd_ref):
    c = pl.program_id(1)

    @pl.when(c == 0)
    def _():
        st_ref[...] = jnp.zeros_like(st_ref)
        i = lax.broadcasted_iota(jnp.int32, (CHUNK, CHUNK), 0).astype(F32)
        j = lax.broadcasted_iota(jnp.int32, (CHUNK, CHUNK), 1).astype(F32)
        for h in range(HEADS):
            lg = _ret_log_decay(h)
            dec_ref[h] = jnp.where(i >= j, jnp.exp(lg * jnp.maximum(i - j, 0.0)), 0.0)
            qd_ref[h] = jnp.exp(lg * (i + 1.0))
            kd_ref[h] = jnp.exp(lg * (CHUNK - 1.0 - i))

    qk = rqk_ref[...]
    q_all = qk[:, :HEADS * R_DK]
    k_all = qk[:, HEADS * R_DK:]
    v_all = rv_ref[...]
    rg = rg_ref[...]
    gn = gn_ref[...]
    st = st_ref[...]
    stb = st.astype(BF16)
    lane = lax.broadcasted_iota(jnp.int32, (1, HEADS * R_DK), 1)
    zero = jnp.zeros_like(q_all)

    vk_parts = []
    for h in range(HEADS):
        hs = slice(h * HEAD_W, (h + 1) * HEAD_W)
        v = v_all[:, hs]
        qm = jnp.where((lane >= h * R_DK) & (lane < (h + 1) * R_DK), q_all, zero)
        s = _dot_nt(qm, k_all) * dec_ref[h]
        intra = _dot(s.astype(BF16), v.astype(BF16))
        cross = _dot(qm, stb) * qd_ref[h]
        o_ref[:, hs] = _ret_finish(intra + cross, rg[:, hs], gn[:, hs]).astype(BF16)
        vk_parts.append((v * kd_ref[h]).astype(BF16))
    kv = _dot_tn(k_all, jnp.concatenate(vk_parts, axis=-1))
    for h in range(HEADS):
        rs = slice(h * R_DK, (h + 1) * R_DK)
        st_ref[rs, :] = st[rs, :] * math.exp(_ret_log_decay(h) * CHUNK) + kv[rs, h * HEAD_W:(h + 1) * HEAD_W]

    @pl.when(c == pl.num_programs(1) - 1)
    def _():
        st_o[...] = st_ref[...]


def _retention_prompt(rqk, rv, rg, gn, batch, seq):
    nc = seq // CHUNK
    blk = pl.BlockSpec((CHUNK, SEG), lambda b, c: (b * nc + c, 0))
    tbl = pltpu.VMEM((HEADS, CHUNK, CHUNK), F32)
    return pl.pallas_call(
        _retention_kernel,
        grid=(batch, nc),
        in_specs=[blk, blk, blk, _const_spec((1, SEG))],
        out_specs=[blk, pl.BlockSpec((None, HEADS * R_DK, HEAD_W), lambda b, c: (b, 0, 0))],
        out_shape=[jax.ShapeDtypeStruct((batch * seq, SEG), BF16),
                   jax.ShapeDtypeStruct((batch, HEADS * R_DK, HEAD_W), F32)],
        scratch_shapes=[pltpu.VMEM((HEADS * R_DK, HEAD_W), F32), tbl, tbl, tbl],
        compiler_params=_cparams("parallel", "arbitrary"),
        name="retention_prompt",
    )(rqk, rv, rg, gn)


def _diff_finish(o0, o1, lam, hn, lam_init):
    o = o0 - lam * o1
    return _rms(o, hn) * (1.0 - lam_init)


def _diff_attn_kernel(lw_ref, q_ref, k_ref, v_ref, hn_ref, o_ref, m_ref, l_ref, acc_ref, *, tq, lam_init):
    h = pl.program_id(1)
    i = pl.program_id(2)
    q = q_ref[...]
    lane = lax.broadcasted_iota(jnp.int32, (1, HEAD_W), 1)
    zero = jnp.zeros_like(q)
    qs = (jnp.where(lane < C_DQK, q, zero), jnp.where(lane < C_DQK, zero, q))
    col = lax.broadcasted_iota(jnp.int32, (1, tq), 1).astype(F32)
    slope = jnp.exp2(jnp.full((1, tq), -8.0 / HEADS, F32) * (h + 1).astype(F32))
    rowi = lax.broadcasted_iota(jnp.int32, (tq, tq), 0)
    coli = lax.broadcasted_iota(jnp.int32, (tq, tq), 1)

    m_ref[...] = jnp.full_like(m_ref, NEG_INF)
    l_ref[...] = jnp.zeros_like(l_ref)
    acc_ref[...] = jnp.zeros_like(acc_ref)

    def block(j, masked):
        start = pl.multiple_of(j * tq, tq)
        kb = k_ref[pl.ds(start, tq), :]
        vb = v_ref[pl.ds(start, tq), :]
        bias = slope * (col + ((j - i) * tq).astype(F32))
        for c in range(2):
            s = _dot_nt(qs[c], kb) + bias
            if masked:
                s = jnp.where(rowi >= coli, s, NEG_INF)
            m_prev = m_ref[c]
            m_new = jnp.maximum(m_prev, jnp.max(s, axis=-1, keepdims=True))
            alpha = jnp.exp(m_prev - m_new)
            p = jnp.exp(s - m_new)
            l_ref[c] = alpha * l_ref[c] + jnp.sum(p, axis=-1, keepdims=True)
            acc_ref[c] = alpha * acc_ref[c] + _dot(p.astype(BF16), vb)
            m_ref[c] = m_new

    def body(j, carry):
        block(j, False)
        return carry

    lax.fori_loop(0, i, body, 0)
    block(i, True)

    lam = _diff_lambda(lw_ref[...], lam_init)
    o = _diff_finish(acc_ref[0] / l_ref[0], acc_ref[1] / l_ref[1], lam, hn_ref[...], lam_init)
    o_ref[...] = o.astype(BF16)


def _diff_attn_prompt(lw, cq, ckb, cvb, hn, batch, seq, lam_init):
    tq = 512
    nq = seq // tq
    qblk = pl.BlockSpec((tq, HEAD_W), lambda b, h, i: (b * nq + i, h))
    kvblk = pl.BlockSpec((seq, HEAD_W), lambda b, h, i: (b, h))
    return pl.pallas_call(
        functools.partial(_diff_attn_kernel, tq=tq, lam_init=lam_init),
        grid=(batch, HEADS, nq),
        in_specs=[_const_spec((4, C_DQK)), qblk, kvblk, kvblk,
                  pl.BlockSpec((1, HEAD_W), lambda b, h, i: (0, h))],
        out_specs=qblk,
        out_shape=jax.ShapeDtypeStruct((batch * seq, SEG), BF16),
        scratch_shapes=[pltpu.VMEM((2, tq, 1), F32), pltpu.VMEM((2, tq, 1), F32),
                        pltpu.VMEM((2, tq, HEAD_W), F32)],
        compiler_params=_cparams("parallel", "parallel", "arbitrary"),
        name="diff_attn_prompt",
    )(lw, cq, ckb, cvb, hn)


def _mem_kv_kernel(x_ref, g_ref, w_ref, kn_ref, k_o, v_o):
    hb = _rms(x_ref[...], g_ref[...]).astype(BF16)
    k_o[...] = _head_rms(_dot(hb, w_ref[0]), kn_ref[...], M_DH)
    v_o[...] = _dot(hb, w_ref[1])


def _mem_kv(mem, g, w2, kn):
    m = mem.shape[0]
    tm = 512
    row = lambda i: (i, 0)
    blk = pl.BlockSpec((tm, SEG), row)
    return pl.pallas_call(
        _mem_kv_kernel,
        grid=(m // tm,),
        in_specs=[pl.BlockSpec((tm, D_MODEL), row), _const_spec((1, D_MODEL)),
                  _const_spec((2, D_MODEL, SEG)), _const_spec((1, SEG))],
        out_specs=[blk, blk],
        out_shape=[jax.ShapeDtypeStruct((m, SEG), F32)] * 2,
        compiler_params=_cparams("parallel"),
        name="mem_kv",
    )(mem, g, w2, kn)


def _mem_attend_head(qh, kh, vh):
    s = _dot_nt(qh, kh.astype(BF16))
    p = jnp.exp(s - jnp.max(s, axis=-1, keepdims=True))
    o = _dot(p.astype(BF16), vh.astype(BF16))
    return o / jnp.sum(p, axis=-1, keepdims=True)


def _mem_attn_kernel(q_ref, k_ref, v_ref, o_ref):
    for h in range(HEADS):
        hs = slice(h * HEAD_W, (h + 1) * HEAD_W)
        o_ref[:, hs] = _mem_attend_head(q_ref[:, hs], k_ref[:, hs], v_ref[:, hs]).astype(BF16)


def _mem_attn_prompt(mq, mk, mv, batch, seq, n_mem):
    tq = 512
    nq = seq // tq
    qblk = pl.BlockSpec((tq, SEG), lambda i: (i, 0))
    kvblk = pl.BlockSpec((n_mem, SEG), lambda i: (i // nq, 0))
    return pl.pallas_call(
        _mem_attn_kernel,
        grid=(batch * nq,),
        in_specs=[qblk, kvblk, kvblk],
        out_specs=qblk,
        out_shape=jax.ShapeDtypeStruct((batch * seq, SEG), BF16),
        compiler_params=_cparams("parallel"),
        name="mem_attn_prompt",
    )(mq, mk, mv)


def _merge_kernel(x_ref, g_ref, b0_ref, b1_ref, b2_ref, b3_ref, wgl_ref, wb_ref, wo_ref, o_ref):
    x = x_ref[...]
    hb = _rms(x, g_ref[...]).astype(BF16)
    merged = None
    for n, b_ref in enumerate((b0_ref, b1_ref, b2_ref, b3_ref)):
        term = _sigmoid(_dot(hb, wgl_ref[n])) * _dot(b_ref[...], wb_ref[n])
        merged = term if merged is None else merged + term
    o_ref[...] = x + _dot(merged.astype(BF16), wo_ref[...])


def _merge(x, g, branches, wgl, wb, wo):
    m = x.shape[0]
    tm = min(512, m)
    row = lambda i: (i, 0)
    xblk = pl.BlockSpec((tm, D_MODEL), row)
    bblk = pl.BlockSpec((tm, SEG), row)
    return pl.pallas_call(
        _merge_kernel,
        grid=(m // tm,),
        in_specs=[xblk, _const_spec((1, D_MODEL)), bblk, bblk, bblk, bblk,
                  _const_spec((N_BRANCH, D_MODEL, D_MODEL)), _const_spec((N_BRANCH, SEG, D_MODEL)),
                  _const_spec((D_MODEL, D_MODEL))],
        out_specs=xblk,
        out_shape=jax.ShapeDtypeStruct((m, D_MODEL), F32),
        compiler_params=_cparams("parallel"),
        name="merge",
    )(x, g, *branches, wgl, wb, wo)


def _ffn_kernel(x_ref, g_ref, wi_ref, wo_ref, o_ref):
    x = x_ref[...]
    hb = _rms(x, g_ref[...]).astype(BF16)
    acc = x
    for a, b in FFN_CHUNKS:
        gate = _dot(hb, wi_ref[:, a:b])
        up = _dot(hb, wi_ref[:, D_FF + a:D_FF + b])
        act = (gate * _sigmoid(gate) * up).astype(BF16)
        acc = acc + _dot(act, wo_ref[a:b, :])
    o_ref[...] = acc


def _ffn(x, g, wi, wo):
    m = x.shape[0]
    tm = min(512, m)
    row = lambda i: (i, 0)
    xblk = pl.BlockSpec((tm, D_MODEL), row)
    return pl.pallas_call(
        _ffn_kernel,
        grid=(m // tm,),
        in_specs=[xblk, _const_spec((1, D_MODEL)), _const_spec((D_MODEL, 2 * D_FF)),
                  _const_spec((D_FF, D_MODEL))],
        out_specs=xblk,
        out_shape=jax.ShapeDtypeStruct((m, D_MODEL), F32),
        compiler_params=_cparams("parallel"),
        name="ffn",
    )(x, g, wi, wo)


PAD_ROWS = 16


def _sample_mix_kernel(rqk_ref, rv_ref, rg_ref, mq_ref, st_ref, mk_ref, mv_ref, gn_ref,
                       oret_o, omm_o, st_o):
    qk = rqk_ref[...]
    q = qk[:, :HEADS * R_DK]
    k = qk[:, HEADS * R_DK:]
    v = rv_ref[...]
    rg = rg_ref[...]
    gn = gn_ref[...]
    st = st_ref[...]
    stb = st.astype(BF16)
    lane = lax.broadcasted_iota(jnp.int32, (1, HEADS * R_DK), 1)

    def first_row(a):
        r = lax.broadcasted_iota(jnp.int32, (PAD_ROWS, a.shape[1]), 0)
        return jnp.where(r == 0, jnp.broadcast_to(a, (PAD_ROWS, a.shape[1])), 0.0).astype(BF16)

    kv = _dot_tn(first_row(k), first_row(v))
    for h in range(HEADS):
        hs = slice(h * HEAD_W, (h + 1) * HEAD_W)
        rs = slice(h * R_DK, (h + 1) * R_DK)
        gamma = math.exp(_ret_log_decay(h))
        qm = jnp.where((lane >= h * R_DK) & (lane < (h + 1) * R_DK), q, 0.0)
        score = jnp.sum(qm * k, axis=-1, keepdims=True)
        cross = _dot(jnp.broadcast_to(qm, (PAD_ROWS, HEADS * R_DK)).astype(BF16), stb)[0:1] * gamma
        o = score * v[:, hs] + cross
        oret_o[:, hs] = _ret_finish(o, rg[:, hs], gn[:, hs]).astype(BF16)
        st_o[rs, :] = st[rs, :] * gamma + kv[rs, hs]

    mq = mq_ref[...]
    for h in range(HEADS):
        hs = slice(h * HEAD_W, (h + 1) * HEAD_W)
        qh = jnp.broadcast_to(mq[:, hs], (PAD_ROWS, HEAD_W))
        omm_o[:, hs] = _mem_attend_head(qh, mk_ref[:, hs], mv_ref[:, hs])[0:1].astype(BF16)


def _sample_mix(layer, rqk, rv, rg, mq, state, mem_k, mem_v, gn):
    db = rqk.shape[0]
    n_mem = mem_k.shape[2]
    row = pl.BlockSpec((None, 1, SEG), lambda b: (b, 0, 0))
    stblk = pl.BlockSpec((None, None, HEADS * R_DK, HEAD_W), lambda b: (layer, b, 0, 0))
    memblk = pl.BlockSpec((None, None, n_mem, SEG), lambda b: (layer, b, 0, 0))
    r3 = lambda a: a.reshape(db, 1, SEG)
    return pl.pallas_call(
        _sample_mix_kernel,
        grid=(db,),
        in_specs=[row, row, row, row, stblk, memblk, memblk, _const_spec((1, SEG))],
        out_specs=[row, row, pl.BlockSpec((None, HEADS * R_DK, HEAD_W), lambda b: (b, 0, 0))],
        out_shape=[jax.ShapeDtypeStruct((db, 1, SEG), BF16), jax.ShapeDtypeStruct((db, 1, SEG), BF16),
                   jax.ShapeDtypeStruct((db, HEADS * R_DK, HEAD_W), F32)],
        compiler_params=_cparams("parallel"),
        name="sample_mix",
    )(r3(rqk), r3(rv), r3(rg), r3(mq), state, mem_k, mem_v, gn)


N_MAPS = 2 * HEADS
PAGES_PER_STEP = 8


def _paged_attn_kernel(pt_ref, lw_ref, q_ref, kn_ref, vn_ref, hn_ref, *rest, page, past_len, lam_init):
    del pt_ref
    npg = PAGES_PER_STEP
    k_refs = rest[:npg]
    v_refs = rest[npg:2 * npg]
    o_ref, m_ref, l_ref, acc_ref = rest[2 * npg:]
    s_idx = pl.program_id(1)

    @pl.when(s_idx == 0)
    def _():
        m_ref[...] = jnp.full_like(m_ref, NEG_INF)
        l_ref[...] = jnp.zeros_like(l_ref)
        acc_ref[...] = jnp.zeros_like(acc_ref)

    r8 = lax.broadcasted_iota(jnp.int32, (N_MAPS, SEG), 0)
    l8 = lax.broadcasted_iota(jnp.int32, (N_MAPS, SEG), 1)
    sel = (l8 >= r8 * C_DQK) & (l8 < (r8 + 1) * C_DQK)
    q8f = jnp.where(sel, jnp.broadcast_to(q_ref[...].astype(F32), (N_MAPS, SEG)), 0.0)
    q8 = q8f.astype(BF16)
    hrow = jnp.right_shift(lax.broadcasted_iota(jnp.int32, (N_MAPS, 1), 0), 1)
    slope = jnp.exp2((-8.0 / HEADS) * (hrow + 1).astype(F32))
    tok = lax.broadcasted_iota(jnp.int32, (1, page), 1)

    scores = []
    for p in range(npg):
        kpos = (s_idx * npg + p) * page + tok
        bias = slope * (kpos - past_len).astype(F32)
        scores.append(_dot_nt(q8, k_refs[p][...].astype(BF16)) + bias)
    s = jnp.concatenate(scores, axis=-1)
    m_prev = m_ref[...]
    m_new = jnp.maximum(m_prev, jnp.max(s, axis=-1, keepdims=True))
    alpha = jnp.exp(m_prev - m_new)
    pr = jnp.exp(s - m_new)
    l_ref[...] = alpha * l_ref[...] + jnp.sum(pr, axis=-1, keepdims=True)
    acc = alpha * acc_ref[...]
    prb = pr.astype(BF16)
    for p in range(npg):
        acc = acc + _dot(prb[:, p * page:(p + 1) * page], v_refs[p][...].astype(BF16))
    acc_ref[...] = acc
    m_ref[...] = m_new

    @pl.when(s_idx == pl.num_programs(1) - 1)
    def _():
        s_new = jnp.sum(q8f * kn_ref[...], axis=-1, keepdims=True)
        m_fin = jnp.maximum(m_ref[...], s_new)
        a = jnp.exp(m_ref[...] - m_fin)
        p_new = jnp.exp(s_new - m_fin)
        l_fin = a * l_ref[...] + p_new
        o_all = (a * acc_ref[...] + p_new * vn_ref[...]) / l_fin
        lam = _diff_lambda(lw_ref[...], lam_init)
        hn = hn_ref[...]
        for h in range(HEADS):
            hs = slice(h * HEAD_W, (h + 1) * HEAD_W)
            o = _diff_finish(o_all[2 * h:2 * h + 1, hs], o_all[2 * h + 1:2 * h + 2, hs], lam, hn[:, hs], lam_init)
            o_ref[:, hs] = o.astype(BF16)


def _paged_attn(layer, page_table, lw, cq, ck, cv, hn, cache_k, cache_v, lam_init):
    db, n_pages = page_table.shape
    page = cache_k.shape[2]
    npg = PAGES_PER_STEP
    row = pl.BlockSpec((None, 1, SEG), lambda b, s, pt: (b, 0, 0))

    def page_spec(p):
        return pl.BlockSpec((None, None, page, SEG),
                            lambda b, s, pt: (layer, pt[b * n_pages + s * npg + p], 0, 0))

    r3 = lambda a: a.reshape(db, 1, SEG)
    grid_spec = pltpu.PrefetchScalarGridSpec(
        num_scalar_prefetch=1,
        grid=(db, n_pages // npg),
        in_specs=[pl.BlockSpec((4, C_DQK), lambda b, s, pt: (0, 0)), row, row, row,
                  pl.BlockSpec((1, SEG), lambda b, s, pt: (0, 0))]
                 + [page_spec(p) for p in range(npg)] * 2,
        out_specs=row,
        scratch_shapes=[pltpu.VMEM((N_MAPS, 1), F32), pltpu.VMEM((N_MAPS, 1), F32),
                        pltpu.VMEM((N_MAPS, SEG), F32)],
    )
    return pl.pallas_call(
        functools.partial(_paged_attn_kernel, page=page, past_len=n_pages * page, lam_init=lam_init),
        grid_spec=grid_spec,
        out_shape=jax.ShapeDtypeStruct((db, 1, SEG), BF16),
        compiler_params=_cparams("parallel", "arbitrary"),
        name="paged_diff_attn",
    )(page_table.reshape(-1), lw, r3(cq), r3(ck), r3(cv), hn,
      *([cache_k] * npg), *([cache_v] * npg))


def kernel(x_prompt, x_sample, mem_prompt, cache_diff_k, cache_diff_v, page_table, cache_mem_k, cache_mem_v, state_ret, norm_mix, w_in, ret_norm, cmlp_norm, cmlp_ws, cmlp_bs, diff_qn, diff_kn, diff_lambda_w, diff_hn, mem_norm, w_mem_kv, mem_qn, mem_kn, w_branch, w_out, norm_ffn, w_ffn_in, w_ffn_out):
    batch, seq, _ = x_prompt.shape
    db, dec_seq, _ = x_sample.shape
    assert dec_seq == 1, "the sample group decodes one token per sequence"
    depth = w_in.shape[0]
    n_mem = mem_prompt.shape[1]
    n_phys, page = cache_diff_k.shape[1:3]
    past_len = page_table.shape[1] * page
    n_main = N_SEG * SEG

    w9 = w_in[:, :, :n_main].reshape(depth, D_MODEL, N_SEG, SEG).transpose(0, 2, 1, 3).astype(BF16)
    wgl = w_in[:, :, n_main:].reshape(depth, D_MODEL, N_BRANCH, D_MODEL).transpose(0, 2, 1, 3).astype(BF16)
    wmem = w_mem_kv.reshape(depth, D_MODEL, 2, SEG).transpose(0, 2, 1, 3).astype(BF16)
    wb = w_branch.astype(BF16)
    wo = w_out.astype(BF16)
    wfi = w_ffn_in.astype(BF16)
    wfo = w_ffn_out.astype(BF16)
    bst = cmlp_bs.transpose(0, 2, 1)
    tile = lambda a, n: jnp.tile(a, (1, n)).reshape(depth, 1, SEG)
    qn, kn = tile(diff_qn, SEG // C_DQK), tile(diff_kn, SEG // C_DQK)
    mqn, mkn = tile(mem_qn, HEADS), tile(mem_kn, HEADS)
    row = lambda a: a.reshape(depth, 1, -1)
    g_mix, g_ffn, g_mem = row(norm_mix), row(norm_ffn), row(mem_norm)
    g_ret, g_cm, g_hn = row(ret_norm), row(cmlp_norm), row(diff_hn)

    cache_k = cache_diff_k.reshape(depth, n_phys, page, SEG)
    cache_v = cache_diff_v.reshape(depth, n_phys, page, SEG)
    mem_k_s = cache_mem_k.reshape(depth, db, n_mem, SEG)
    mem_v_s = cache_mem_v.reshape(depth, db, n_mem, SEG)
    state_s = state_ret.reshape(depth, db, HEADS * R_DK, HEAD_W)

    xp = x_prompt.reshape(batch * seq, D_MODEL)
    xs = x_sample.reshape(db, D_MODEL)
    mem = mem_prompt.reshape(batch * n_mem, D_MODEL)

    pk, pv, pmk, pmv, pst = [], [], [], [], []
    sk, sv, sst, scv = [], [], [], []
    for l in range(depth):
        lam_init = 0.8 - 0.6 * math.exp(-0.3 * l)
        lw = diff_lambda_w[l]

        rqk, rv, rg, o_cm, cq, ck, ckb, cv, cvb, mq = _proj_in(
            xp, g_mix[l], w9[l], g_cm[l], cmlp_ws[l], bst[l], qn[l], kn[l], mqn[l], prompt=True)
        o_ret, st_p = _retention_prompt(rqk, rv, rg, g_ret[l], batch, seq)
        o_df = _diff_attn_prompt(lw, cq, ckb, cvb, g_hn[l], batch, seq, lam_init)
        mk, mv = _mem_kv(mem, g_mem[l], wmem[l], mkn[l])
        o_mm = _mem_attn_prompt(mq, mk, mv, batch, seq, n_mem)
        xp = _merge(xp, g_mix[l], (o_ret, o_cm, o_df, o_mm), wgl[l], wb[l], wo[l])
        xp = _ffn(xp, g_ffn[l], wfi[l], wfo[l])
        pk.append(ck.reshape(batch, seq, HEADS, 2, C_DQK))
        pv.append(cv.reshape(batch, seq, HEADS, HEAD_W))
        pmk.append(mk.reshape(batch, n_mem, HEADS, M_DH))
        pmv.append(mv.reshape(batch, n_mem, HEADS, M_DH))
        pst.append(st_p.reshape(batch, HEADS, R_DK, HEAD_W))

        rqk, rv, rg, o_cm, vn_s, cq, ck, cv, mq = _proj_in(
            xs, g_mix[l], w9[l], g_cm[l], cmlp_ws[l], bst[l], qn[l], kn[l], mqn[l],
            prompt=False, off=past_len % CHUNK)
        o_ret, o_mm, st_s = _sample_mix(l, rqk, rv, rg, mq, state_s, mem_k_s, mem_v_s, g_ret[l])
        o_df = _paged_attn(l, page_table, lw, cq, ck, cv, g_hn[l], cache_k, cache_v, lam_init)
        xs = _merge(xs, g_mix[l], (o_ret.reshape(db, SEG), o_cm, o_df.reshape(db, SEG), o_mm.reshape(db, SEG)),
                    wgl[l], wb[l], wo[l])
        xs = _ffn(xs, g_ffn[l], wfi[l], wfo[l])
        sk.append(ck.reshape(db, 1, HEADS, 2, C_DQK))
        sv.append(cv.reshape(db, 1, HEADS, HEAD_W))
        sst.append(st_s.reshape(db, HEADS, R_DK, HEAD_W))
        scv.append(vn_s.reshape(db, 1, SEG))

    return (xp.reshape(batch, seq, D_MODEL), xs.reshape(db, 1, D_MODEL),
            jnp.stack(pk), jnp.stack(pv), jnp.stack(pmk), jnp.stack(pmv), jnp.stack(pst),
            jnp.stack(sk), jnp.stack(sv), jnp.stack(sst), jnp.stack(scv))
```

```python
import functools
import math

import jax
import jax.numpy as jnp
from jax import lax
from jax.experimental import pallas as pl
from jax.experimental.pallas import tpu as pltpu

F32 = jnp.float32
BF16 = jnp.bfloat16

EPS = 1e-6
NEG_INF = -1e30
LOG2E = math.log2(math.e)

D_MODEL = 1024
SEG = 512
N_SEG = 9
HEADS = 4
R_DK = 64
HEAD_W = 128
CHUNK = 128
C_DQK = 64
M_DH = 128
N_BRANCH = 4
D_FF = 2816
FFN_CHUNKS = ((0, 1024), (1024, 2048), (2048, 2816))

VMEM_LIMIT_BYTES = 56 * 1024 * 1024


def _cparams(*sem):
    return pltpu.CompilerParams(dimension_semantics=sem, vmem_limit_bytes=VMEM_LIMIT_BYTES)


def _dot(a, b):
    return jnp.dot(a, b, preferred_element_type=F32)


def _dot_nt(a, b):
    return lax.dot_general(a, b, (((1,), (1,)), ((), ())), preferred_element_type=F32)


def _dot_tn(a, b):
    return lax.dot_general(a, b, (((0,), (0,)), ((), ())), preferred_element_type=F32)


def _sigmoid(x):
    return 1.0 / (1.0 + jnp.exp(-x))


def _rms(x, g):
    return x * lax.rsqrt(jnp.mean(x * x, axis=-1, keepdims=True) + EPS) * g


def _layer_norm(x, g):
    xc = x - jnp.mean(x, axis=-1, keepdims=True)
    return xc * lax.rsqrt(jnp.mean(xc * xc, axis=-1, keepdims=True) + EPS) * g


def _head_rms(y, g, group):
    lane = lax.broadcasted_iota(jnp.int32, (1, HEAD_W), 1)
    outs = []
    for hb in range(SEG // HEAD_W):
        blk = y[:, hb * HEAD_W:(hb + 1) * HEAD_W]
        sq = blk * blk
        if group == HEAD_W:
            ms = jnp.mean(sq, axis=-1, keepdims=True)
        else:
            lo = jnp.sum(jnp.where(lane < group, sq, 0.0), axis=-1, keepdims=True)
            hi = jnp.sum(jnp.where(lane < group, 0.0, sq), axis=-1, keepdims=True)
            ms = jnp.where(lane < group, lo, hi) * (1.0 / group)
        outs.append(blk * lax.rsqrt(ms + EPS))
    return jnp.concatenate(outs, axis=-1) * g


def _diff_lambda(wl, lam_init):
    a = jnp.sum(wl[0:1] * wl[1:2], axis=-1, keepdims=True)
    b = jnp.sum(wl[2:3] * wl[3:4], axis=-1, keepdims=True)
    return jnp.exp(a) - jnp.exp(b) + lam_init


def _const_spec(shape):
    nd = len(shape)
    return pl.BlockSpec(shape, lambda *_: (0,) * nd, pipeline_mode=pl.Buffered(1))


def _proj_kernel(x_ref, g_ref, w_ref, cn_ref, ws_ref, bst_ref, qn_ref, kn_ref, mqn_ref, *outs,
                 prompt, tm, off):
    if prompt:
        rqk_o, rv_o, rg_o, ocm_o, cq_o, ck_o, ckb_o, cv_o, cvb_o, mq_o = outs
    else:
        rqk_o, rv_o, rg_o, ocm_o, vn_o, cq_o, ck_o, cv_o, mq_o = outs
    hb = _rms(x_ref[...], g_ref[...]).astype(BF16)

    def seg(s):
        return _dot(hb, w_ref[s])

    lane = lax.broadcasted_iota(jnp.int32, (1, SEG), 1)
    y = seg(0) * jnp.where(lane < HEADS * R_DK, 1.0, R_DK ** -0.5)
    rqk_o[...] = y.astype(rqk_o.dtype)
    rv_o[...] = seg(1)
    rg_o[...] = seg(2)

    gu = seg(3)
    vn = _layer_norm(seg(4), cn_ref[...])
    if prompt:
        vnb = vn.astype(BF16)
        row = lax.broadcasted_iota(jnp.int32, (CHUNK, CHUNK), 0)
        col = lax.broadcasted_iota(jnp.int32, (CHUNK, CHUNK), 1)
        for g in range(HEADS):
            gs = slice(g * HEAD_W, (g + 1) * HEAD_W)
            wt = jnp.where(row >= col, ws_ref[g], 0.0).astype(BF16)
            bcol = bst_ref[:, g:g + 1]
            for c in range(tm // CHUNK):
                cs = slice(c * CHUNK, (c + 1) * CHUNK)
                mix = _dot(wt, vnb[cs, gs]) + bcol
                ocm_o[cs, gs] = (gu[cs, gs] * mix).astype(BF16)
    else:
        vn_o[...] = vn
        for g in range(HEADS):
            gs = slice(g * HEAD_W, (g + 1) * HEAD_W)
            w00 = ws_ref[g][off:off + 1, off:off + 1]
            b0 = bst_ref[off:off + 1, g:g + 1]
            ocm_o[:, gs] = (gu[:, gs] * (w00 * vn[:, gs] + b0)).astype(BF16)

    cq_o[...] = (_head_rms(seg(5), qn_ref[...], C_DQK) * (C_DQK ** -0.5 * LOG2E)).astype(BF16)
    ck = _head_rms(seg(6), kn_ref[...], C_DQK)
    ck_o[...] = ck
    cv = seg(7)
    cv_o[...] = cv
    if prompt:
        ckb_o[...] = ck.astype(BF16)
        cvb_o[...] = cv.astype(BF16)
    mq_o[...] = (_head_rms(seg(8), mqn_ref[...], M_DH) * (M_DH ** -0.5)).astype(BF16)


def _proj_in(x, g, w9, cn, ws, bst, qn, kn, mqn, *, prompt, off=0):
    m = x.shape[0]
    tm = min(512, m)
    row = lambda i: (i, 0)
    blk = pl.BlockSpec((tm, SEG), row)
    if prompt:
        dts = (BF16, F32, F32, BF16, BF16, F32, BF16, F32, BF16, BF16)
    else:
        dts = (F32, F32, F32, BF16, F32, BF16, F32, F32, BF16)
    return pl.pallas_call(
        functools.partial(_proj_kernel, prompt=prompt, tm=tm, off=off),
        grid=(m // tm,),
        in_specs=[pl.BlockSpec((tm, D_MODEL), row), _const_spec((1, D_MODEL)),
                  _const_spec((N_SEG, D_MODEL, SEG)), _const_spec((1, SEG)),
                  _const_spec((HEADS, CHUNK, CHUNK)), _const_spec((CHUNK, HEADS)),
                  _const_spec((1, SEG)), _const_spec((1, SEG)), _const_spec((1, SEG))],
        out_specs=[blk] * len(dts),
        out_shape=[jax.ShapeDtypeStruct((m, SEG), dt) for dt in dts],
        compiler_params=_cparams("parallel"),
        name="proj_in_prompt" if prompt else "proj_in_sample",
    )(x, g, w9, cn, ws, bst, qn, kn, mqn)


def _ret_log_decay(h):
    return math.log1p(-(2.0 ** (-5 - h)))


def _ret_finish(o, gate, gn):
    return gate * _sigmoid(gate) * _layer_norm(o, gn)


def _retention_kernel(rqk_ref, rv_ref, rg_ref, gn_ref, o_ref, st_o, st_ref, dec_ref, qd_ref, kd_ref):
    c = pl.program_id(1)

    @pl.when(c == 0)
    def _():
        st_ref[...] = jnp.zeros_like(st_ref)
        i = lax.broadcasted_iota(jnp.int32, (CHUNK, CHUNK), 0).astype(F32)
        j = lax.broadcasted_iota(jnp.int32, (CHUNK, CHUNK), 1).astype(F32)
        for h in range(HEADS):
            lg = _ret_log_decay(h)
            dec_ref[h] = jnp.where(i >= j, jnp.exp(lg * jnp.maximum(i - j, 0.0)), 0.0)
            qd_ref[h] = jnp.exp(lg * (i + 1.0))
            kd_ref[h] = jnp.exp(lg * (CHUNK - 1.0 - i))

    qk = rqk_ref[...]
    q_all = qk[:, :HEADS * R_DK]
    k_all = qk[:, HEADS * R_DK:]
    v_all = rv_ref[...]
    rg = rg_ref[...]
    gn = gn_ref[...]
    st = st_ref[...]
    stb = st.astype(BF16)
    lane = lax.broadcasted_iota(jnp.int32, (1, HEADS * R_DK), 1)
    zero = jnp.zeros_like(q_all)

    vk_parts = []
    for h in range(HEADS):
        hs = slice(h * HEAD_W, (h + 1) * HEAD_W)
        v = v_all[:, hs]
        qm = jnp.where((lane >= h * R_DK) & (lane < (h + 1) * R_DK), q_all, zero)
        s = _dot_nt(qm, k_all) * dec_ref[h]
        intra = _dot(s.astype(BF16), v.astype(BF16))
        cross = _dot(qm, stb) * qd_ref[h]
        o_ref[:, hs] = _ret_finish(intra + cross, rg[:, hs], gn[:, hs]).astype(BF16)
        vk_parts.append((v * kd_ref[h]).astype(BF16))
    kv = _dot_tn(k_all, jnp.concatenate(vk_parts, axis=-1))
    for h in range(HEADS):
        rs = slice(h * R_DK, (h + 1) * R_DK)
        st_ref[rs, :] = st[rs, :] * math.exp(_ret_log_decay(h) * CHUNK) + kv[rs, h * HEAD_W:(h + 1) * HEAD_W]

    @pl.when(c == pl.num_programs(1) - 1)
    def _():
        st_o[...] = st_ref[...]


def _retention_prompt(rqk, rv, rg, gn, batch, seq):
    nc = seq // CHUNK
    blk = pl.BlockSpec((CHUNK, SEG), lambda b, c: (b * nc + c, 0))
    tbl = pltpu.VMEM((HEADS, CHUNK, CHUNK), F32)
    return pl.pallas_call(
        _retention_kernel,
        grid=(batch, nc),
        in_specs=[blk, blk, blk, _const_spec((1, SEG))],
        out_specs=[blk, pl.BlockSpec((None, HEADS * R_DK, HEAD_W), lambda b, c: (b, 0, 0))],
        out_shape=[jax.ShapeDtypeStruct((batch * seq, SEG), BF16),
                   jax.ShapeDtypeStruct((batch, HEADS * R_DK, HEAD_W), F32)],
        scratch_shapes=[pltpu.VMEM((HEADS * R_DK, HEAD_W), F32), tbl, tbl, tbl],
        compiler_params=_cparams("parallel", "arbitrary"),
        name="retention_prompt",
    )(rqk, rv, rg, gn)


def _diff_finish(o0, o1, lam, hn, lam_init):
    o = o0 - lam * o1
    return _rms(o, hn) * (1.0 - lam_init)


def _alibi_slope_log2(h, shape):
    return jnp.exp2(jnp.full(shape, -8.0 / HEADS, F32) * (h + 1).astype(F32)) * LOG2E


def _diff_attn_kernel(lw_ref, q_ref, k_ref, v_ref, hn_ref, o_ref, vt_ref, acc_ref, *, tq, nblk, lam_init):
    h = pl.program_id(1)
    i = pl.program_id(2)

    @pl.when(i == 0)
    def _():
        for t in range(nblk):
            vt_ref[t] = v_ref[t * tq:(t + 1) * tq, :].astype(F32).T.astype(BF16)

    q = q_ref[...]
    lane = lax.broadcasted_iota(jnp.int32, (1, HEAD_W), 1)
    zero = jnp.zeros_like(q)
    qs = (jnp.where(lane < C_DQK, q, zero), jnp.where(lane < C_DQK, zero, q))
    krow = lax.broadcasted_iota(jnp.int32, (tq, HEAD_W), 0).astype(F32)
    brep = _alibi_slope_log2(h, (tq, HEAD_W)) * krow
    bias = jnp.concatenate([brep] * (tq // HEAD_W), axis=1)
    slope_row = _alibi_slope_log2(h, (1, tq))
    rowi = lax.broadcasted_iota(jnp.int32, (tq, tq), 0)
    coli = lax.broadcasted_iota(jnp.int32, (tq, tq), 1)
    acc_ref[...] = jnp.zeros_like(acc_ref)

    def block(j, carry, masked):
        start = pl.multiple_of(j * tq, tq)
        kb = k_ref[pl.ds(start, tq), :]
        vtb = vt_ref[j]
        off = slope_row * ((j - i) * tq).astype(F32)
        new = []
        for c in range(2):
            m_prev, l_prev = carry[2 * c], carry[2 * c + 1]
            s = _dot_nt(kb, qs[c]) + bias
            if masked:
                s = jnp.where(coli >= rowi, s, NEG_INF)
            m_new = jnp.maximum(m_prev, jnp.max(s, axis=0, keepdims=True) + off)
            alpha = jnp.exp2(m_prev - m_new)
            p = jnp.exp2(s - (m_new - off))
            l_new = alpha * l_prev + jnp.sum(p, axis=0, keepdims=True)
            acc_ref[c] = alpha * acc_ref[c] + _dot(vtb, p.astype(BF16))
            new += [m_new, l_new]
        return tuple(new)

    neg = jnp.full((1, tq), NEG_INF, F32)
    zer = jnp.zeros((1, tq), F32)
    carry = lax.fori_loop(0, i, lambda j, cr: block(j, cr, False), (neg, zer, neg, zer))
    _, l0, _, l1 = block(i, carry, True)

    lam = _diff_lambda(lw_ref[...], lam_init)
    o = acc_ref[0] / l0 - lam * (acc_ref[1] / l1)
    y = o * lax.rsqrt(jnp.mean(o * o, axis=0, keepdims=True) + EPS)
    o_ref[...] = (y.T * hn_ref[...] * (1.0 - lam_init)).astype(BF16)


def _diff_attn_prompt(lw, cq, ckb, cvb, hn, batch, seq, lam_init):
    tq = 512
    nq = seq // tq
    qblk = pl.BlockSpec((tq, HEAD_W), lambda b, h, i: (b * nq + i, h))
    kvblk = pl.BlockSpec((seq, HEAD_W), lambda b, h, i: (b, h))
    return pl.pallas_call(
        functools.partial(_diff_attn_kernel, tq=tq, nblk=nq, lam_init=lam_init),
        grid=(batch, HEADS, nq),
        in_specs=[_const_spec((4, C_DQK)), qblk, kvblk, kvblk,
                  pl.BlockSpec((1, HEAD_W), lambda b, h, i: (0, h))],
        out_specs=qblk,
        out_shape=jax.ShapeDtypeStruct((batch * seq, SEG), BF16),
        scratch_shapes=[pltpu.VMEM((nq, HEAD_W, tq), BF16), pltpu.VMEM((2, HEAD_W, tq), F32)],
        compiler_params=_cparams("parallel", "parallel", "arbitrary"),
        name="diff_attn_prompt",
    )(lw, cq, ckb, cvb, hn)


def _mem_kv_kernel(x_ref, g_ref, w_ref, kn_ref, k_o, v_o):
    hb = _rms(x_ref[...], g_ref[...]).astype(BF16)
    k_o[...] = _head_rms(_dot(hb, w_ref[0]), kn_ref[...], M_DH)
    v_o[...] = _dot(hb, w_ref[1])


def _mem_kv(mem, g, w2, kn):
    m = mem.shape[0]
    tm = 512
    row = lambda i: (i, 0)
    blk = pl.BlockSpec((tm, SEG), row)
    return pl.pallas_call(
        _mem_kv_kernel,
        grid=(m // tm,),
        in_specs=[pl.BlockSpec((tm, D_MODEL), row), _const_spec((1, D_MODEL)),
                  _const_spec((2, D_MODEL, SEG)), _const_spec((1, SEG))],
        out_specs=[blk, blk],
        out_shape=[jax.ShapeDtypeStruct((m, SEG), F32)] * 2,
        compiler_params=_cparams("parallel"),
        name="mem_kv",
    )(mem, g, w2, kn)


def _mem_attend_head(qh, kh, vh):
    s = _dot_nt(qh, kh.astype(BF16))
    p = jnp.exp(s - jnp.max(s, axis=-1, keepdims=True))
    o = _dot(p.astype(BF16), vh.astype(BF16))
    return o / jnp.sum(p, axis=-1, keepdims=True)


def _mem_attn_kernel(q_ref, k_ref, v_ref, o_ref):
    for h in range(HEADS):
        hs = slice(h * HEAD_W, (h + 1) * HEAD_W)
        o_ref[:, hs] = _mem_attend_head(q_ref[:, hs], k_ref[:, hs], v_ref[:, hs]).astype(BF16)


def _mem_attn_prompt(mq, mk, mv, batch, seq, n_mem):
    tq = 512
    nq = seq // tq
    qblk = pl.BlockSpec((tq, SEG), lambda i: (i, 0))
    kvblk = pl.BlockSpec((n_mem, SEG), lambda i: (i // nq, 0))
    return pl.pallas_call(
        _mem_attn_kernel,
        grid=(batch * nq,),
        in_specs=[qblk, kvblk, kvblk],
        out_specs=qblk,
        out_shape=jax.ShapeDtypeStruct((batch * seq, SEG), BF16),
        compiler_params=_cparams("parallel"),
        name="mem_attn_prompt",
    )(mq, mk, mv)


def _merge_kernel(x_ref, g_ref, b0_ref, b1_ref, b2_ref, b3_ref, wgl_ref, wb_ref, wo_ref, o_ref):
    x = x_ref[...]
    hb = _rms(x, g_ref[...]).astype(BF16)
    merged = None
    for n, b_ref in enumerate((b0_ref, b1_ref, b2_ref, b3_ref)):
        term = _sigmoid(_dot(hb, wgl_ref[n])) * _dot(b_ref[...], wb_ref[n])
        merged = term if merged is None else merged + term
    o_ref[...] = x + _dot(merged.astype(BF16), wo_ref[...])


def _merge(x, g, branches, wgl, wb, wo):
    m = x.shape[0]
    tm = min(512, m)
    row = lambda i: (i, 0)
    xblk = pl.BlockSpec((tm, D_MODEL), row)
    bblk = pl.BlockSpec((tm, SEG), row)
    return pl.pallas_call(
        _merge_kernel,
        grid=(m // tm,),
        in_specs=[xblk, _const_spec((1, D_MODEL)), bblk, bblk, bblk, bblk,
                  _const_spec((N_BRANCH, D_MODEL, D_MODEL)), _const_spec((N_BRANCH, SEG, D_MODEL)),
                  _const_spec((D_MODEL, D_MODEL))],
        out_specs=xblk,
        out_shape=jax.ShapeDtypeStruct((m, D_MODEL), F32),
        compiler_params=_cparams("parallel"),
        name="merge",
    )(x, g, *branches, wgl, wb, wo)


def _ffn_kernel(x_ref, g_ref, wi_ref, wo_ref, o_ref):
    x = x_ref[...]
    hb = _rms(x, g_ref[...]).astype(BF16)
    acc = x
    for a, b in FFN_CHUNKS:
        gate = _dot(hb, wi_ref[:, a:b])
        up = _dot(hb, wi_ref[:, D_FF + a:D_FF + b])
        act = (gate * _sigmoid(gate) * up).astype(BF16)
        acc = acc + _dot(act, wo_ref[a:b, :])
    o_ref[...] = acc


def _ffn(x, g, wi, wo):
    m = x.shape[0]
    tm = min(512, m)
    row = lambda i: (i, 0)
    xblk = pl.BlockSpec((tm, D_MODEL), row)
    return pl.pallas_call(
        _ffn_kernel,
        grid=(m // tm,),
        in_specs=[xblk, _const_spec((1, D_MODEL)), _const_spec((D_MODEL, 2 * D_FF)),
                  _const_spec((D_FF, D_MODEL))],
        out_specs=xblk,
        out_shape=jax.ShapeDtypeStruct((m, D_MODEL), F32),
        compiler_params=_cparams("parallel"),
        name="ffn",
    )(x, g, wi, wo)


PAD_ROWS = 16


def _sample_mix_kernel(rqk_ref, rv_ref, rg_ref, mq_ref, st_ref, mk_ref, mv_ref, gn_ref,
                       oret_o, omm_o, st_o, *, n_mem):
    qk = rqk_ref[...]
    q = qk[:, :HEADS * R_DK]
    k = qk[:, HEADS * R_DK:]
    v = rv_ref[...]
    rg = rg_ref[...]
    gn = gn_ref[...]
    st = st_ref[...]
    stb = st.astype(BF16)
    lane = lax.broadcasted_iota(jnp.int32, (1, HEADS * R_DK), 1)

    def first_row(a):
        r = lax.broadcasted_iota(jnp.int32, (PAD_ROWS, a.shape[1]), 0)
        return jnp.where(r == 0, jnp.broadcast_to(a, (PAD_ROWS, a.shape[1])), 0.0).astype(BF16)

    kv = _dot_tn(first_row(k), first_row(v))
    for h in range(HEADS):
        hs = slice(h * HEAD_W, (h + 1) * HEAD_W)
        rs = slice(h * R_DK, (h + 1) * R_DK)
        gamma = math.exp(_ret_log_decay(h))
        qm = jnp.where((lane >= h * R_DK) & (lane < (h + 1) * R_DK), q, 0.0)
        score = jnp.sum(qm * k, axis=-1, keepdims=True)
        cross = _dot(jnp.broadcast_to(qm, (PAD_ROWS, HEADS * R_DK)).astype(BF16), stb)[0:1] * gamma
        o = score * v[:, hs] + cross
        oret_o[:, hs] = _ret_finish(o, rg[:, hs], gn[:, hs]).astype(BF16)
        st_o[rs, :] = st[rs, :] * gamma + kv[rs, hs]

    mq = mq_ref[...]
    for h in range(HEADS):
        hs = slice(h * HEAD_W, (h + 1) * HEAD_W)
        qh = jnp.broadcast_to(mq[:, hs], (PAD_ROWS, HEAD_W))
        kh = mk_ref[pl.ds(h, n_mem, stride=HEADS), :]
        vh = mv_ref[pl.ds(h, n_mem, stride=HEADS), :]
        omm_o[:, hs] = _mem_attend_head(qh, kh, vh)[0:1].astype(BF16)


def _sample_mix(layer, rqk, rv, rg, mq, state, mem_k, mem_v, gn, n_mem):
    db = rqk.shape[0]
    row = pl.BlockSpec((None, 1, SEG), lambda b: (b, 0, 0))
    stblk = pl.BlockSpec((None, None, HEADS * R_DK, HEAD_W), lambda b: (layer, b, 0, 0))
    memblk = pl.BlockSpec((None, None, n_mem * HEADS, M_DH), lambda b: (layer, b, 0, 0))
    r3 = lambda a: a.reshape(db, 1, SEG)
    return pl.pallas_call(
        functools.partial(_sample_mix_kernel, n_mem=n_mem),
        grid=(db,),
        in_specs=[row, row, row, row, stblk, memblk, memblk, _const_spec((1, SEG))],
        out_specs=[row, row, pl.BlockSpec((None, HEADS * R_DK, HEAD_W), lambda b: (b, 0, 0))],
        out_shape=[jax.ShapeDtypeStruct((db, 1, SEG), BF16), jax.ShapeDtypeStruct((db, 1, SEG), BF16),
                   jax.ShapeDtypeStruct((db, HEADS * R_DK, HEAD_W), F32)],
        compiler_params=_cparams("parallel"),
        name="sample_mix",
    )(r3(rqk), r3(rv), r3(rg), r3(mq), state, mem_k, mem_v, gn)


N_MAPS = 2 * HEADS
PAGES_PER_STEP = 8


def _paged_attn_kernel(pt_ref, lw_ref, q_ref, kn_ref, vn_ref, hn_ref, *rest, page, past_len, lam_init):
    del pt_ref
    npg = PAGES_PER_STEP
    k_refs = rest[:npg]
    v_refs = rest[npg:2 * npg]
    o_ref, m_ref, l_ref, acc_ref = rest[2 * npg:]
    s_idx = pl.program_id(1)

    @pl.when(s_idx == 0)
    def _():
        m_ref[...] = jnp.full_like(m_ref, NEG_INF)
        l_ref[...] = jnp.zeros_like(l_ref)
        acc_ref[...] = jnp.zeros_like(acc_ref)

    r8 = lax.broadcasted_iota(jnp.int32, (N_MAPS, SEG), 0)
    l8 = lax.broadcasted_iota(jnp.int32, (N_MAPS, SEG), 1)
    sel = (l8 >= r8 * C_DQK) & (l8 < (r8 + 1) * C_DQK)
    q8f = jnp.where(sel, jnp.broadcast_to(q_ref[...].astype(F32), (N_MAPS, SEG)), 0.0)
    q8 = q8f.astype(BF16)
    hrow = jnp.right_shift(lax.broadcasted_iota(jnp.int32, (N_MAPS, 1), 0), 1)
    hrow_w = jnp.right_shift(lax.broadcasted_iota(jnp.int32, (N_MAPS, HEAD_W), 0), 1)
    slope = jnp.exp2((-8.0 / HEADS) * (hrow + 1).astype(F32)) * LOG2E
    tok = lax.broadcasted_iota(jnp.int32, (1, page), 1)

    scores = []
    for p in range(npg):
        kpos = (s_idx * npg + p) * page + tok
        bias = slope * (kpos - past_len).astype(F32)
        scores.append(_dot(q8, k_refs[p][...].astype(BF16)) + bias)
    s = jnp.concatenate(scores, axis=-1)
    m_prev = m_ref[...]
    m_new = jnp.maximum(m_prev, jnp.max(s, axis=-1, keepdims=True))
    alpha = jnp.exp2(m_prev - m_new)
    pr = jnp.exp2(s - m_new)
    l_ref[...] = alpha * l_ref[...] + jnp.sum(pr, axis=-1, keepdims=True)
    acc = alpha * acc_ref[...]
    prb = pr.astype(BF16)
    for p in range(npg):
        pp = prb[:, p * page:(p + 1) * page]
        for h in range(HEADS):
            vh = v_refs[p][pl.ds(h, page, stride=HEADS), :].astype(BF16)
            acc = acc + jnp.where(hrow_w == h, _dot(pp, vh), 0.0)
    acc_ref[...] = acc
    m_ref[...] = m_new

    @pl.when(s_idx == pl.num_programs(1) - 1)
    def _():
        s_new = jnp.sum(q8f * kn_ref[...], axis=-1, keepdims=True)
        m_fin = jnp.maximum(m_ref[...], s_new)
        a = jnp.exp2(m_ref[...] - m_fin)
        p_new = jnp.exp2(s_new - m_fin)
        l_fin = a * l_ref[...] + p_new
        vn = vn_ref[...]
        vn8 = jnp.zeros((N_MAPS, HEAD_W), F32)
        for h in range(HEADS):
            vn8 = jnp.where(hrow_w == h, jnp.broadcast_to(vn[:, h * HEAD_W:(h + 1) * HEAD_W], (N_MAPS, HEAD_W)), vn8)
        o_all = (a * acc_ref[...] + p_new * vn8) / l_fin
        lam = _diff_lambda(lw_ref[...], lam_init)
        hn = hn_ref[...]
        for h in range(HEADS):
            hs = slice(h * HEAD_W, (h + 1) * HEAD_W)
            o = _diff_finish(o_all[2 * h:2 * h + 1], o_all[2 * h + 1:2 * h + 2], lam, hn[:, hs], lam_init)
            o_ref[:, hs] = o.astype(BF16)


def _paged_attn(layer, page_table, lw, cq, ck, cv, hn, cache_kt, cache_v, lam_init):
    db, n_pages = page_table.shape
    page = cache_kt.shape[3]
    npg = PAGES_PER_STEP
    row = pl.BlockSpec((None, 1, SEG), lambda b, s, pt: (b, 0, 0))

    def page_spec(p):
        return pl.BlockSpec((None, None, SEG, page),
                            lambda b, s, pt: (layer, pt[b * n_pages + s * npg + p], 0, 0))

    r3 = lambda a: a.reshape(db, 1, SEG)
    grid_spec = pltpu.PrefetchScalarGridSpec(
        num_scalar_prefetch=1,
        grid=(db, n_pages // npg),
        in_specs=[pl.BlockSpec((4, C_DQK), lambda b, s, pt: (0, 0)), row, row, row,
                  pl.BlockSpec((1, SEG), lambda b, s, pt: (0, 0))]
                 + [page_spec(p) for p in range(npg)] * 2,
        out_specs=row,
        scratch_shapes=[pltpu.VMEM((N_MAPS, 1), F32), pltpu.VMEM((N_MAPS, 1), F32),
                        pltpu.VMEM((N_MAPS, HEAD_W), F32)],
    )
    return pl.pallas_call(
        functools.partial(_paged_attn_kernel, page=page, past_len=n_pages * page, lam_init=lam_init),
        grid_spec=grid_spec,
        out_shape=jax.ShapeDtypeStruct((db, 1, SEG), BF16),
        compiler_params=_cparams("parallel", "arbitrary"),
        name="paged_diff_attn",
    )(page_table.reshape(-1), lw, r3(cq), r3(ck), r3(cv), hn,
      *([cache_kt] * npg), *([cache_v] * npg))


def kernel(x_prompt, x_sample, mem_prompt, cache_diff_k, cache_diff_v, page_table, cache_mem_k, cache_mem_v, state_ret, norm_mix, w_in, ret_norm, cmlp_norm, cmlp_ws, cmlp_bs, diff_qn, diff_kn, diff_lambda_w, diff_hn, mem_norm, w_mem_kv, mem_qn, mem_kn, w_branch, w_out, norm_ffn, w_ffn_in, w_ffn_out):
    batch, seq, _ = x_prompt.shape
    db, dec_seq, _ = x_sample.shape
    assert dec_seq == 1, "the sample group decodes one token per sequence"
    depth = w_in.shape[0]
    n_mem = mem_prompt.shape[1]
    n_phys, page = cache_diff_k.shape[1:3]
    past_len = page_table.shape[1] * page
    n_main = N_SEG * SEG

    w9 = w_in[:, :, :n_main].reshape(depth, D_MODEL, N_SEG, SEG).transpose(0, 2, 1, 3).astype(BF16)
    wgl = w_in[:, :, n_main:].reshape(depth, D_MODEL, N_BRANCH, D_MODEL).transpose(0, 2, 1, 3).astype(BF16)
    wmem = w_mem_kv.reshape(depth, D_MODEL, 2, SEG).transpose(0, 2, 1, 3).astype(BF16)
    wb = w_branch.astype(BF16)
    wo = w_out.astype(BF16)
    wfi = w_ffn_in.astype(BF16)
    wfo = w_ffn_out.astype(BF16)
    bst = cmlp_bs.transpose(0, 2, 1)
    tile = lambda a, n: jnp.tile(a, (1, n)).reshape(depth, 1, SEG)
    qn, kn = tile(diff_qn, SEG // C_DQK), tile(diff_kn, SEG // C_DQK)
    mqn, mkn = tile(mem_qn, HEADS), tile(mem_kn, HEADS)
    row = lambda a: a.reshape(depth, 1, -1)
    g_mix, g_ffn, g_mem = row(norm_mix), row(norm_ffn), row(mem_norm)
    g_ret, g_cm, g_hn = row(ret_norm), row(cmlp_norm), row(diff_hn)

    cache_kt = cache_diff_k.transpose(0, 1, 3, 4, 5, 2).reshape(depth, n_phys, SEG, page)
    cache_v = cache_diff_v.reshape(depth, n_phys, page * HEADS, HEAD_W)
    mem_k_s = cache_mem_k.reshape(depth, db, n_mem * HEADS, M_DH)
    mem_v_s = cache_mem_v.reshape(depth, db, n_mem * HEADS, M_DH)
    state_s = state_ret.reshape(depth, db, HEADS * R_DK, HEAD_W)

    xp = x_prompt.reshape(batch * seq, D_MODEL)
    xs = x_sample.reshape(db, D_MODEL)
    mem = mem_prompt.reshape(batch * n_mem, D_MODEL)

    pk, pv, pmk, pmv, pst = [], [], [], [], []
    sk, sv, sst, scv = [], [], [], []
    for l in range(depth):
        lam_init = 0.8 - 0.6 * math.exp(-0.3 * l)
        lw = diff_lambda_w[l]

        rqk, rv, rg, o_cm, cq, ck, ckb, cv, cvb, mq = _proj_in(
            xp, g_mix[l], w9[l], g_cm[l], cmlp_ws[l], bst[l], qn[l], kn[l], mqn[l], prompt=True)
        o_ret, st_p = _retention_prompt(rqk, rv, rg, g_ret[l], batch, seq)
        o_df = _diff_attn_prompt(lw, cq, ckb, cvb, g_hn[l], batch, seq, lam_init)
        mk, mv = _mem_kv(mem, g_mem[l], wmem[l], mkn[l])
        o_mm = _mem_attn_prompt(mq, mk, mv, batch, seq, n_mem)
        xp = _merge(xp, g_mix[l], (o_ret, o_cm, o_df, o_mm), wgl[l], wb[l], wo[l])
        xp = _ffn(xp, g_ffn[l], wfi[l], wfo[l])
        pk.append(ck.reshape(batch, seq, HEADS, 2, C_DQK))
        pv.append(cv.reshape(batch, seq, HEADS, HEAD_W))
        pmk.append(mk.reshape(batch, n_mem, HEADS, M_DH))
        pmv.append(mv.reshape(batch, n_mem, HEADS, M_DH))
        pst.append(st_p.reshape(batch, HEADS, R_DK, HEAD_W))

        rqk, rv, rg, o_cm, vn_s, cq, ck, cv, mq = _proj_in(
            xs, g_mix[l], w9[l], g_cm[l], cmlp_ws[l], bst[l], qn[l], kn[l], mqn[l],
            prompt=False, off=past_len % CHUNK)
        o_ret, o_mm, st_s = _sample_mix(l, rqk, rv, rg, mq, state_s, mem_k_s, mem_v_s, g_ret[l], n_mem)
        o_df = _paged_attn(l, page_table, lw, cq, ck, cv, g_hn[l], cache_kt, cache_v, lam_init)
        xs = _merge(xs, g_mix[l], (o_ret.reshape(db, SEG), o_cm, o_df.reshape(db, SEG), o_mm.reshape(db, SEG)),
                    wgl[l], wb[l], wo[l])
        xs = _ffn(xs, g_ffn[l], wfi[l], wfo[l])
        sk.append(ck.reshape(db, 1, HEADS, 2, C_DQK))
        sv.append(cv.reshape(db, 1, HEADS, HEAD_W))
        sst.append(st_s.reshape(db, HEADS, R_DK, HEAD_W))
        scv.append(vn_s.reshape(db, 1, SEG))

    return (xp.reshape(batch, seq, D_MODEL), xs.reshape(db, 1, D_MODEL),
            jnp.stack(pk), jnp.stack(pv), jnp.stack(pmk), jnp.stack(pmv), jnp.stack(pst),
            jnp.stack(sk), jnp.stack(sv), jnp.stack(sst), jnp.stack(scv))
```

```python
import functools
import math

import jax
import jax.numpy as jnp
from jax import lax
from jax.experimental import pallas as pl
from jax.experimental.pallas import tpu as pltpu

F32 = jnp.float32
BF16 = jnp.bfloat16

EPS = 1e-6
NEG_INF = -1e30
LOG2E = math.log2(math.e)

D_MODEL = 1024
SEG = 512
N_SEG = 9
HEADS = 4
R_DK = 64
HEAD_W = 128
CHUNK = 128
C_DQK = 64
M_DH = 128
N_BRANCH = 4
D_FF = 2816
FFN_CHUNKS = ((0, 1024), (1024, 2048), (2048, 2816))

VMEM_LIMIT_BYTES = 56 * 1024 * 1024


def _cparams(*sem):
    return pltpu.CompilerParams(dimension_semantics=sem, vmem_limit_bytes=VMEM_LIMIT_BYTES)


def _dot(a, b):
    return jnp.dot(a, b, preferred_element_type=F32)


def _dot_nt(a, b):
    return lax.dot_general(a, b, (((1,), (1,)), ((), ())), preferred_element_type=F32)


def _dot_tn(a, b):
    return lax.dot_general(a, b, (((0,), (0,)), ((), ())), preferred_element_type=F32)


def _sigmoid(x):
    return 1.0 / (1.0 + jnp.exp(-x))


def _rms(x, g):
    return x * lax.rsqrt(jnp.mean(x * x, axis=-1, keepdims=True) + EPS) * g


def _layer_norm(x, g):
    xc = x - jnp.mean(x, axis=-1, keepdims=True)
    return xc * lax.rsqrt(jnp.mean(xc * xc, axis=-1, keepdims=True) + EPS) * g


def _head_rms(y, g, group):
    lane = lax.broadcasted_iota(jnp.int32, (1, HEAD_W), 1)
    outs = []
    for hb in range(SEG // HEAD_W):
        blk = y[:, hb * HEAD_W:(hb + 1) * HEAD_W]
        sq = blk * blk
        if group == HEAD_W:
            ms = jnp.mean(sq, axis=-1, keepdims=True)
        else:
            lo = jnp.sum(jnp.where(lane < group, sq, 0.0), axis=-1, keepdims=True)
            hi = jnp.sum(jnp.where(lane < group, 0.0, sq), axis=-1, keepdims=True)
            ms = jnp.where(lane < group, lo, hi) * (1.0 / group)
        outs.append(blk * lax.rsqrt(ms + EPS))
    return jnp.concatenate(outs, axis=-1) * g


def _diff_lambda(wl, lam_init):
    a = jnp.sum(wl[0:1] * wl[1:2], axis=-1, keepdims=True)
    b = jnp.sum(wl[2:3] * wl[3:4], axis=-1, keepdims=True)
    return jnp.exp(a) - jnp.exp(b) + lam_init


def _const_spec(shape):
    nd = len(shape)
    return pl.BlockSpec(shape, lambda *_: (0,) * nd, pipeline_mode=pl.Buffered(1))


def _proj_kernel(x_ref, g_ref, w_ref, cn_ref, ws_ref, bst_ref, qn_ref, kn_ref, mqn_ref, *rest,
                 prompt, tm, off):
    if prompt:
        rqk_o, rv_o, rg_o, ocm_o, cq_o, ckt_o, ckb_o, cv4_o, cvb_o, mq_o = rest[2:]
    else:
        rqk_o, rv_o, rg_o, ocm_o, vn_o, cq_o, ck_o, cv_o, mq_o = rest
    hb = _rms(x_ref[...], g_ref[...]).astype(BF16)

    def seg(s):
        return _dot(hb, w_ref[s])

    lane = lax.broadcasted_iota(jnp.int32, (1, SEG), 1)
    y = seg(0) * jnp.where(lane < HEADS * R_DK, 1.0, R_DK ** -0.5)
    rqk_o[...] = y.astype(rqk_o.dtype)
    rv_o[...] = seg(1)
    rg_o[...] = seg(2)

    gu = seg(3)
    vn = _layer_norm(seg(4), cn_ref[...])
    if prompt:
        vnb = vn.astype(BF16)
        row = lax.broadcasted_iota(jnp.int32, (CHUNK, CHUNK), 0)
        col = lax.broadcasted_iota(jnp.int32, (CHUNK, CHUNK), 1)
        for g in range(HEADS):
            gs = slice(g * HEAD_W, (g + 1) * HEAD_W)
            wt = jnp.where(row >= col, ws_ref[g], 0.0).astype(BF16)
            bcol = bst_ref[:, g:g + 1]
            for c in range(tm // CHUNK):
                cs = slice(c * CHUNK, (c + 1) * CHUNK)
                mix = _dot(wt, vnb[cs, gs]) + bcol
                ocm_o[cs, gs] = (gu[cs, gs] * mix).astype(BF16)
    else:
        vn_o[...] = vn
        for g in range(HEADS):
            gs = slice(g * HEAD_W, (g + 1) * HEAD_W)
            w00 = ws_ref[g][off:off + 1, off:off + 1]
            b0 = bst_ref[off:off + 1, g:g + 1]
            ocm_o[:, gs] = (gu[:, gs] * (w00 * vn[:, gs] + b0)).astype(BF16)

    cq_o[...] = (_head_rms(seg(5), qn_ref[...], C_DQK) * (C_DQK ** -0.5 * LOG2E)).astype(BF16)
    ck = _head_rms(seg(6), kn_ref[...], C_DQK)
    cv = seg(7)
    if prompt:
        ckt_o[...] = ck.T
        for h in range(HEADS):
            cv4_o[pl.ds(h, tm, stride=HEADS), :] = cv[:, h * HEAD_W:(h + 1) * HEAD_W]
        ckb_o[...] = ck.astype(BF16)
        cvb_o[...] = cv.astype(BF16)
    else:
        ck_o[...] = ck
        cv_o[...] = cv
    mq_o[...] = (_head_rms(seg(8), mqn_ref[...], M_DH) * (M_DH ** -0.5)).astype(BF16)


def _proj_in_specs(tm):
    row = lambda i: (i, 0)
    return [pl.BlockSpec((tm, D_MODEL), row), _const_spec((1, D_MODEL)),
            _const_spec((N_SEG, D_MODEL, SEG)), _const_spec((1, SEG)),
            _const_spec((HEADS, CHUNK, CHUNK)), _const_spec((CHUNK, HEADS)),
            _const_spec((1, SEG)), _const_spec((1, SEG)), _const_spec((1, SEG))]


def _proj_in_sample(x, g, w9, cn, ws, bst, qn, kn, mqn, *, off):
    m = x.shape[0]
    blk = pl.BlockSpec((m, SEG), lambda i: (i, 0))
    dts = (F32, F32, F32, BF16, F32, BF16, F32, F32, BF16)
    return pl.pallas_call(
        functools.partial(_proj_kernel, prompt=False, tm=m, off=off),
        grid=(1,),
        in_specs=_proj_in_specs(m),
        out_specs=[blk] * len(dts),
        out_shape=[jax.ShapeDtypeStruct((m, SEG), dt) for dt in dts],
        compiler_params=_cparams("parallel"),
        name="proj_in_sample",
    )(x, g, w9, cn, ws, bst, qn, kn, mqn)


def _proj_in_prompt(x, g, w9, cn, ws, bst, qn, kn, mqn, ckt_all, cv4_all, *, layer, seq):
    m = x.shape[0]
    tm = 512
    nq = seq // tm
    blk = pl.BlockSpec((tm, SEG), lambda i: (i, 0))
    cktblk = pl.BlockSpec((None, None, SEG, tm), lambda i: (layer, i // nq, 0, i % nq))
    cv4blk = pl.BlockSpec((None, tm * HEADS, HEAD_W), lambda i: (layer, i, 0))
    sds = lambda dt: jax.ShapeDtypeStruct((m, SEG), dt)
    n_in = len(_proj_in_specs(tm))
    return pl.pallas_call(
        functools.partial(_proj_kernel, prompt=True, tm=tm, off=0),
        grid=(m // tm,),
        in_specs=_proj_in_specs(tm) + [pl.BlockSpec(memory_space=pl.ANY)] * 2,
        out_specs=[blk, blk, blk, blk, blk, cktblk, blk, cv4blk, blk, blk],
        out_shape=[sds(BF16), sds(F32), sds(F32), sds(BF16), sds(BF16),
                   jax.ShapeDtypeStruct(ckt_all.shape, F32), sds(BF16),
                   jax.ShapeDtypeStruct(cv4_all.shape, F32), sds(BF16), sds(BF16)],
        input_output_aliases={n_in: 5, n_in + 1: 7},
        compiler_params=_cparams("parallel"),
        name="proj_in_prompt",
    )(x, g, w9, cn, ws, bst, qn, kn, mqn, ckt_all, cv4_all)


def _ret_log_decay(h):
    return math.log1p(-(2.0 ** (-5 - h)))


def _ret_finish(o, gate, gn):
    return gate * _sigmoid(gate) * _layer_norm(o, gn)


def _retention_kernel(rqk_ref, rv_ref, rg_ref, gn_ref, o_ref, st_o, st_ref, dec_ref, qd_ref, kd_ref):
    c = pl.program_id(1)

    @pl.when(c == 0)
    def _():
        st_ref[...] = jnp.zeros_like(st_ref)
        i = lax.broadcasted_iota(jnp.int32, (CHUNK, CHUNK), 0).astype(F32)
        j = lax.broadcasted_iota(jnp.int32, (CHUNK, CHUNK), 1).astype(F32)
        for h in range(HEADS):
            lg = _ret_log_decay(h)
            dec_ref[h] = jnp.where(i >= j, jnp.exp(lg * jnp.maximum(i - j, 0.0)), 0.0)
            qd_ref[h] = jnp.exp(lg * (i + 1.0))
            kd_ref[h] = jnp.exp(lg * (CHUNK - 1.0 - i))

    qk = rqk_ref[...]
    q_all = qk[:, :HEADS * R_DK]
    k_all = qk[:, HEADS * R_DK:]
    v_all = rv_ref[...]
    rg = rg_ref[...]
    gn = gn_ref[...]
    st = st_ref[...]
    stb = st.astype(BF16)
    lane = lax.broadcasted_iota(jnp.int32, (1, HEADS * R_DK), 1)
    zero = jnp.zeros_like(q_all)

    vk_parts = []
    for h in range(HEADS):
        hs = slice(h * HEAD_W, (h + 1) * HEAD_W)
        v = v_all[:, hs]
        qm = jnp.where((lane >= h * R_DK) & (lane < (h + 1) * R_DK), q_all, zero)
        s = _dot_nt(qm, k_all) * dec_ref[h]
        intra = _dot(s.astype(BF16), v.astype(BF16))
        cross = _dot(qm, stb) * qd_ref[h]
        o_ref[:, hs] = _ret_finish(intra + cross, rg[:, hs], gn[:, hs]).astype(BF16)
        vk_parts.append((v * kd_ref[h]).astype(BF16))
    kv = _dot_tn(k_all, jnp.concatenate(vk_parts, axis=-1))
    for h in range(HEADS):
        rs = slice(h * R_DK, (h + 1) * R_DK)
        st_ref[rs, :] = st[rs, :] * math.exp(_ret_log_decay(h) * CHUNK) + kv[rs, h * HEAD_W:(h + 1) * HEAD_W]

    @pl.when(c == pl.num_programs(1) - 1)
    def _():
        st_o[...] = st_ref[...]


def _retention_prompt(rqk, rv, rg, gn, batch, seq):
    nc = seq // CHUNK
    blk = pl.BlockSpec((CHUNK, SEG), lambda b, c: (b * nc + c, 0))
    tbl = pltpu.VMEM((HEADS, CHUNK, CHUNK), F32)
    return pl.pallas_call(
        _retention_kernel,
        grid=(batch, nc),
        in_specs=[blk, blk, blk, _const_spec((1, SEG))],
        out_specs=[blk, pl.BlockSpec((None, HEADS * R_DK, HEAD_W), lambda b, c: (b, 0, 0))],
        out_shape=[jax.ShapeDtypeStruct((batch * seq, SEG), BF16),
                   jax.ShapeDtypeStruct((batch, HEADS * R_DK, HEAD_W), F32)],
        scratch_shapes=[pltpu.VMEM((HEADS * R_DK, HEAD_W), F32), tbl, tbl, tbl],
        compiler_params=_cparams("parallel", "arbitrary"),
        name="retention_prompt",
    )(rqk, rv, rg, gn)


def _diff_finish(o0, o1, lam, hn, lam_init):
    o = o0 - lam * o1
    return _rms(o, hn) * (1.0 - lam_init)


def _alibi_slope_log2(h, shape):
    return jnp.exp2(jnp.full(shape, -8.0 / HEADS, F32) * (h + 1).astype(F32)) * LOG2E


def _diff_attn_kernel(lw_ref, q_ref, k_ref, v_ref, hn_ref, o_ref, vt_ref, acc_ref, s_ref, *, tq, nblk, lam_init):
    h = pl.program_id(1)
    i = pl.program_id(2)

    @pl.when(i == 0)
    def _():
        for t in range(nblk):
            vt_ref[t] = v_ref[t * tq:(t + 1) * tq, :].astype(F32).T.astype(BF16)

    q = q_ref[...]
    lane = lax.broadcasted_iota(jnp.int32, (1, HEAD_W), 1)
    zero = jnp.zeros_like(q)
    qs = (jnp.where(lane < C_DQK, q, zero), jnp.where(lane < C_DQK, zero, q))
    krow = lax.broadcasted_iota(jnp.int32, (tq, HEAD_W), 0).astype(F32)
    brep = _alibi_slope_log2(h, (tq, HEAD_W)) * krow
    bias = jnp.concatenate([brep] * (tq // HEAD_W), axis=1)
    slope_row = _alibi_slope_log2(h, (1, tq))
    rowi = lax.broadcasted_iota(jnp.int32, (tq, tq), 0)
    coli = lax.broadcasted_iota(jnp.int32, (tq, tq), 1)
    acc_ref[...] = jnp.zeros_like(acc_ref)

    def scores(j):
        kb = k_ref[pl.ds(pl.multiple_of(j * tq, tq), tq), :]
        for c in range(2):
            s_ref[j % 2, c] = _dot_nt(kb, qs[c])

    def block(j, carry, masked):
        vtb = vt_ref[j]
        off = slope_row * ((j - i) * tq).astype(F32)
        new = []
        for c in range(2):
            m_prev, l_prev = carry[2 * c], carry[2 * c + 1]
            s = s_ref[j % 2, c] + bias
            if masked:
                s = jnp.where(coli >= rowi, s, NEG_INF)
            m_new = jnp.maximum(m_prev, jnp.max(s, axis=0, keepdims=True) + off)
            alpha = jnp.exp2(m_prev - m_new)
            p = jnp.exp2(s - (m_new - off))
            l_new = alpha * l_prev + jnp.sum(p, axis=0, keepdims=True)
            acc_ref[c] = alpha * acc_ref[c] + _dot(vtb, p.astype(BF16))
            new += [m_new, l_new]
        return tuple(new)

    def body(j, carry):
        new = block(j, carry, False)
        scores(j + 1)
        return new

    neg = jnp.full((1, tq), NEG_INF, F32)
    zer = jnp.zeros((1, tq), F32)
    scores(0)
    carry = lax.fori_loop(0, i, body, (neg, zer, neg, zer))
    _, l0, _, l1 = block(i, carry, True)

    lam = _diff_lambda(lw_ref[...], lam_init)
    o = acc_ref[0] / l0 - lam * (acc_ref[1] / l1)
    y = o * lax.rsqrt(jnp.mean(o * o, axis=0, keepdims=True) + EPS)
    o_ref[...] = (y.T * hn_ref[...] * (1.0 - lam_init)).astype(BF16)


def _diff_attn_prompt(lw, cq, ckb, cvb, hn, batch, seq, lam_init):
    tq = 512
    nq = seq // tq
    qblk = pl.BlockSpec((tq, HEAD_W), lambda b, h, i: (b * nq + i, h))
    kvblk = pl.BlockSpec((seq, HEAD_W), lambda b, h, i: (b, h))
    return pl.pallas_call(
        functools.partial(_diff_attn_kernel, tq=tq, nblk=nq, lam_init=lam_init),
        grid=(batch, HEADS, nq),
        in_specs=[_const_spec((4, C_DQK)), qblk, kvblk, kvblk,
                  pl.BlockSpec((1, HEAD_W), lambda b, h, i: (0, h))],
        out_specs=qblk,
        out_shape=jax.ShapeDtypeStruct((batch * seq, SEG), BF16),
        scratch_shapes=[pltpu.VMEM((nq, HEAD_W, tq), BF16), pltpu.VMEM((2, HEAD_W, tq), F32),
                        pltpu.VMEM((2, 2, tq, tq), F32)],
        compiler_params=_cparams("parallel", "parallel", "arbitrary"),
        name="diff_attn_prompt",
    )(lw, cq, ckb, cvb, hn)


def _mem_kv_kernel(x_ref, g_ref, w_ref, kn_ref, k_all, v_all, k_o, v_o, *, tm):
    del k_all, v_all
    hb = _rms(x_ref[...], g_ref[...]).astype(BF16)
    k = _head_rms(_dot(hb, w_ref[0]), kn_ref[...], M_DH)
    v = _dot(hb, w_ref[1])
    for h in range(HEADS):
        hs = slice(h * HEAD_W, (h + 1) * HEAD_W)
        k_o[pl.ds(h, tm, stride=HEADS), :] = k[:, hs]
        v_o[pl.ds(h, tm, stride=HEADS), :] = v[:, hs]


def _mem_kv(mem, g, w2, kn, mk_all, mv_all, *, layer):
    m = mem.shape[0]
    tm = 512
    row = lambda i: (i, 0)
    blk = pl.BlockSpec((None, tm * HEADS, M_DH), lambda i: (layer, i, 0))
    return pl.pallas_call(
        functools.partial(_mem_kv_kernel, tm=tm),
        grid=(m // tm,),
        in_specs=[pl.BlockSpec((tm, D_MODEL), row), _const_spec((1, D_MODEL)),
                  _const_spec((2, D_MODEL, SEG)), _const_spec((1, SEG)),
                  pl.BlockSpec(memory_space=pl.ANY), pl.BlockSpec(memory_space=pl.ANY)],
        out_specs=[blk, blk],
        out_shape=[jax.ShapeDtypeStruct(mk_all.shape, F32)] * 2,
        input_output_aliases={4: 0, 5: 1},
        compiler_params=_cparams("parallel"),
        name="mem_kv",
    )(mem, g, w2, kn, mk_all, mv_all)


def _mem_attend_head(qh, kh, vh):
    s = _dot_nt(qh, kh.astype(BF16))
    p = jnp.exp(s - jnp.max(s, axis=-1, keepdims=True))
    o = _dot(p.astype(BF16), vh.astype(BF16))
    return o / jnp.sum(p, axis=-1, keepdims=True)


def _mem_attn_kernel(q_ref, k_ref, v_ref, o_ref, *, n_mem):
    for h in range(HEADS):
        hs = slice(h * HEAD_W, (h + 1) * HEAD_W)
        kh = k_ref[pl.ds(h, n_mem, stride=HEADS), :]
        vh = v_ref[pl.ds(h, n_mem, stride=HEADS), :]
        o_ref[:, hs] = _mem_attend_head(q_ref[:, hs], kh, vh).astype(BF16)


def _mem_attn_prompt(layer, mq, mk_all, mv_all, batch, seq, n_mem):
    tq = 512
    nq = seq // tq
    qblk = pl.BlockSpec((tq, SEG), lambda i: (i, 0))
    kvblk = pl.BlockSpec((None, n_mem * HEADS, M_DH), lambda i: (layer, i // nq, 0))
    return pl.pallas_call(
        functools.partial(_mem_attn_kernel, n_mem=n_mem),
        grid=(batch * nq,),
        in_specs=[qblk, kvblk, kvblk],
        out_specs=qblk,
        out_shape=jax.ShapeDtypeStruct((batch * seq, SEG), BF16),
        compiler_params=_cparams("parallel"),
        name="mem_attn_prompt",
    )(mq, mk_all, mv_all)


def _merge_kernel(x_ref, g_ref, b0_ref, b1_ref, b2_ref, b3_ref, wgl_ref, wb_ref, wo_ref, o_ref):
    x = x_ref[...]
    hb = _rms(x, g_ref[...]).astype(BF16)
    merged = None
    for n, b_ref in enumerate((b0_ref, b1_ref, b2_ref, b3_ref)):
        term = _sigmoid(_dot(hb, wgl_ref[n])) * _dot(b_ref[...], wb_ref[n])
        merged = term if merged is None else merged + term
    o_ref[...] = x + _dot(merged.astype(BF16), wo_ref[...])


def _merge(x, g, branches, wgl, wb, wo):
    m = x.shape[0]
    tm = min(512, m)
    row = lambda i: (i, 0)
    xblk = pl.BlockSpec((tm, D_MODEL), row)
    bblk = pl.BlockSpec((tm, SEG), row)
    return pl.pallas_call(
        _merge_kernel,
        grid=(m // tm,),
        in_specs=[xblk, _const_spec((1, D_MODEL)), bblk, bblk, bblk, bblk,
                  _const_spec((N_BRANCH, D_MODEL, D_MODEL)), _const_spec((N_BRANCH, SEG, D_MODEL)),
                  _const_spec((D_MODEL, D_MODEL))],
        out_specs=xblk,
        out_shape=jax.ShapeDtypeStruct((m, D_MODEL), F32),
        compiler_params=_cparams("parallel"),
        name="merge",
    )(x, g, *branches, wgl, wb, wo)


def _ffn_kernel(x_ref, g_ref, wi_ref, wo_ref, o_ref):
    x = x_ref[...]
    hb = _rms(x, g_ref[...]).astype(BF16)
    acc = x
    for a, b in FFN_CHUNKS:
        gate = _dot(hb, wi_ref[:, a:b])
        up = _dot(hb, wi_ref[:, D_FF + a:D_FF + b])
        act = (gate * _sigmoid(gate) * up).astype(BF16)
        acc = acc + _dot(act, wo_ref[a:b, :])
    o_ref[...] = acc


def _ffn(x, g, wi, wo):
    m = x.shape[0]
    tm = min(512, m)
    row = lambda i: (i, 0)
    xblk = pl.BlockSpec((tm, D_MODEL), row)
    return pl.pallas_call(
        _ffn_kernel,
        grid=(m // tm,),
        in_specs=[xblk, _const_spec((1, D_MODEL)), _const_spec((D_MODEL, 2 * D_FF)),
                  _const_spec((D_FF, D_MODEL))],
        out_specs=xblk,
        out_shape=jax.ShapeDtypeStruct((m, D_MODEL), F32),
        compiler_params=_cparams("parallel"),
        name="ffn",
    )(x, g, wi, wo)


PAD_ROWS = 16


def _sample_mix_kernel(rqk_ref, rv_ref, rg_ref, mq_ref, st_ref, mk_ref, mv_ref, gn_ref,
                       oret_o, omm_o, st_o, *, n_mem):
    qk = rqk_ref[...]
    q = qk[:, :HEADS * R_DK]
    k = qk[:, HEADS * R_DK:]
    v = rv_ref[...]
    rg = rg_ref[...]
    gn = gn_ref[...]
    st = st_ref[...]
    stb = st.astype(BF16)
    lane = lax.broadcasted_iota(jnp.int32, (1, HEADS * R_DK), 1)

    def first_row(a):
        r = lax.broadcasted_iota(jnp.int32, (PAD_ROWS, a.shape[1]), 0)
        return jnp.where(r == 0, jnp.broadcast_to(a, (PAD_ROWS, a.shape[1])), 0.0).astype(BF16)

    kv = _dot_tn(first_row(k), first_row(v))
    for h in range(HEADS):
        hs = slice(h * HEAD_W, (h + 1) * HEAD_W)
        rs = slice(h * R_DK, (h + 1) * R_DK)
        gamma = math.exp(_ret_log_decay(h))
        qm = jnp.where((lane >= h * R_DK) & (lane < (h + 1) * R_DK), q, 0.0)
        score = jnp.sum(qm * k, axis=-1, keepdims=True)
        cross = _dot(jnp.broadcast_to(qm, (PAD_ROWS, HEADS * R_DK)).astype(BF16), stb)[0:1] * gamma
        o = score * v[:, hs] + cross
        oret_o[:, hs] = _ret_finish(o, rg[:, hs], gn[:, hs]).astype(BF16)
        st_o[rs, :] = st[rs, :] * gamma + kv[rs, hs]

    mq = mq_ref[...]
    for h in range(HEADS):
        hs = slice(h * HEAD_W, (h + 1) * HEAD_W)
        qh = jnp.broadcast_to(mq[:, hs], (PAD_ROWS, HEAD_W))
        kh = mk_ref[pl.ds(h, n_mem, stride=HEADS), :]
        vh = mv_ref[pl.ds(h, n_mem, stride=HEADS), :]
        omm_o[:, hs] = _mem_attend_head(qh, kh, vh)[0:1].astype(BF16)


def _sample_mix(layer, rqk, rv, rg, mq, state, mem_k, mem_v, gn, n_mem):
    db = rqk.shape[0]
    row = pl.BlockSpec((None, 1, SEG), lambda b: (b, 0, 0))
    stblk = pl.BlockSpec((None, None, HEADS * R_DK, HEAD_W), lambda b: (layer, b, 0, 0))
    memblk = pl.BlockSpec((None, None, n_mem * HEADS, M_DH), lambda b: (layer, b, 0, 0))
    r3 = lambda a: a.reshape(db, 1, SEG)
    return pl.pallas_call(
        functools.partial(_sample_mix_kernel, n_mem=n_mem),
        grid=(db,),
        in_specs=[row, row, row, row, stblk, memblk, memblk, _const_spec((1, SEG))],
        out_specs=[row, row, pl.BlockSpec((None, HEADS * R_DK, HEAD_W), lambda b: (b, 0, 0))],
        out_shape=[jax.ShapeDtypeStruct((db, 1, SEG), BF16), jax.ShapeDtypeStruct((db, 1, SEG), BF16),
                   jax.ShapeDtypeStruct((db, HEADS * R_DK, HEAD_W), F32)],
        compiler_params=_cparams("parallel"),
        name="sample_mix",
    )(r3(rqk), r3(rv), r3(rg), r3(mq), state, mem_k, mem_v, gn)


N_MAPS = 2 * HEADS
PAGES_PER_STEP = 16


def _paged_attn_kernel(pt_ref, lw_ref, q_ref, kn_ref, vn_ref, hn_ref, *rest, page, past_len, lam_init):
    del pt_ref
    npg = PAGES_PER_STEP
    k_refs = rest[:npg]
    v_refs = rest[npg:2 * npg]
    o_ref, m_ref, l_ref, acc_ref = rest[2 * npg:]
    s_idx = pl.program_id(1)

    @pl.when(s_idx == 0)
    def _():
        m_ref[...] = jnp.full_like(m_ref, NEG_INF)
        l_ref[...] = jnp.zeros_like(l_ref)
        acc_ref[...] = jnp.zeros_like(acc_ref)

    r8 = lax.broadcasted_iota(jnp.int32, (N_MAPS, SEG), 0)
    l8 = lax.broadcasted_iota(jnp.int32, (N_MAPS, SEG), 1)
    sel = (l8 >= r8 * C_DQK) & (l8 < (r8 + 1) * C_DQK)
    q8f = jnp.where(sel, jnp.broadcast_to(q_ref[...].astype(F32), (N_MAPS, SEG)), 0.0)
    q8 = q8f.astype(BF16)
    hrow = jnp.right_shift(lax.broadcasted_iota(jnp.int32, (N_MAPS, 1), 0), 1)
    hrow_w = jnp.right_shift(lax.broadcasted_iota(jnp.int32, (N_MAPS, HEAD_W), 0), 1)
    slope = jnp.exp2((-8.0 / HEADS) * (hrow + 1).astype(F32)) * LOG2E
    kpos = s_idx * (npg * page) + lax.broadcasted_iota(jnp.int32, (1, npg * page), 1)
    bias = slope * (kpos - past_len).astype(F32)
    kt = jnp.concatenate([k_refs[p][...].astype(BF16) for p in range(npg)], axis=1)
    s = _dot(q8, kt) + bias
    m_prev = m_ref[...]
    m_new = jnp.maximum(m_prev, jnp.max(s, axis=-1, keepdims=True))
    alpha = jnp.exp2(m_prev - m_new)
    pr = jnp.exp2(s - m_new)
    l_ref[...] = alpha * l_ref[...] + jnp.sum(pr, axis=-1, keepdims=True)
    acc = alpha * acc_ref[...]
    prb = pr.astype(BF16)
    for h in range(HEADS):
        vh = jnp.concatenate([v_refs[p][pl.ds(h, page, stride=HEADS), :].astype(BF16) for p in range(npg)], axis=0)
        acc = acc + jnp.where(hrow_w == h, _dot(prb, vh), 0.0)
    acc_ref[...] = acc
    m_ref[...] = m_new

    @pl.when(s_idx == pl.num_programs(1) - 1)
    def _():
        s_new = jnp.sum(q8f * kn_ref[...], axis=-1, keepdims=True)
        m_fin = jnp.maximum(m_ref[...], s_new)
        a = jnp.exp2(m_ref[...] - m_fin)
        p_new = jnp.exp2(s_new - m_fin)
        l_fin = a * l_ref[...] + p_new
        vn = vn_ref[...]
        vn8 = jnp.zeros((N_MAPS, HEAD_W), F32)
        for h in range(HEADS):
            vn8 = jnp.where(hrow_w == h, jnp.broadcast_to(vn[:, h * HEAD_W:(h + 1) * HEAD_W], (N_MAPS, HEAD_W)), vn8)
        o_all = (a * acc_ref[...] + p_new * vn8) / l_fin
        lam = _diff_lambda(lw_ref[...], lam_init)
        hn = hn_ref[...]
        for h in range(HEADS):
            hs = slice(h * HEAD_W, (h + 1) * HEAD_W)
            o = _diff_finish(o_all[2 * h:2 * h + 1], o_all[2 * h + 1:2 * h + 2], lam, hn[:, hs], lam_init)
            o_ref[:, hs] = o.astype(BF16)


def _paged_attn(layer, page_table, lw, cq, ck, cv, hn, cache_kt, cache_v, lam_init):
    db, n_pages = page_table.shape
    page = cache_kt.shape[3]
    npg = PAGES_PER_STEP
    row = pl.BlockSpec((None, 1, SEG), lambda b, s, pt: (b, 0, 0))

    def page_spec(p):
        return pl.BlockSpec((None, None, SEG, page),
                            lambda b, s, pt: (layer, pt[b * n_pages + s * npg + p], 0, 0))

    r3 = lambda a: a.reshape(db, 1, SEG)
    grid_spec = pltpu.PrefetchScalarGridSpec(
        num_scalar_prefetch=1,
        grid=(db, n_pages // npg),
        in_specs=[pl.BlockSpec((4, C_DQK), lambda b, s, pt: (0, 0)), row, row, row,
                  pl.BlockSpec((1, SEG), lambda b, s, pt: (0, 0))]
                 + [page_spec(p) for p in range(npg)] * 2,
        out_specs=row,
        scratch_shapes=[pltpu.VMEM((N_MAPS, 1), F32), pltpu.VMEM((N_MAPS, 1), F32),
                        pltpu.VMEM((N_MAPS, HEAD_W), F32)],
    )
    return pl.pallas_call(
        functools.partial(_paged_attn_kernel, page=page, past_len=n_pages * page, lam_init=lam_init),
        grid_spec=grid_spec,
        out_shape=jax.ShapeDtypeStruct((db, 1, SEG), BF16),
        compiler_params=_cparams("parallel", "arbitrary"),
        name="paged_diff_attn",
    )(page_table.reshape(-1), lw, r3(cq), r3(ck), r3(cv), hn,
      *([cache_kt] * npg), *([cache_v] * npg))


def kernel(x_prompt, x_sample, mem_prompt, cache_diff_k, cache_diff_v, page_table, cache_mem_k, cache_mem_v, state_ret, norm_mix, w_in, ret_norm, cmlp_norm, cmlp_ws, cmlp_bs, diff_qn, diff_kn, diff_lambda_w, diff_hn, mem_norm, w_mem_kv, mem_qn, mem_kn, w_branch, w_out, norm_ffn, w_ffn_in, w_ffn_out):
    batch, seq, _ = x_prompt.shape
    db, dec_seq, _ = x_sample.shape
    assert dec_seq == 1, "the sample group decodes one token per sequence"
    depth = w_in.shape[0]
    n_mem = mem_prompt.shape[1]
    n_phys, page = cache_diff_k.shape[1:3]
    past_len = page_table.shape[1] * page
    n_main = N_SEG * SEG

    w9 = w_in[:, :, :n_main].reshape(depth, D_MODEL, N_SEG, SEG).transpose(0, 2, 1, 3).astype(BF16)
    wgl = w_in[:, :, n_main:].reshape(depth, D_MODEL, N_BRANCH, D_MODEL).transpose(0, 2, 1, 3).astype(BF16)
    wmem = w_mem_kv.reshape(depth, D_MODEL, 2, SEG).transpose(0, 2, 1, 3).astype(BF16)
    wb = w_branch.astype(BF16)
    wo = w_out.astype(BF16)
    wfi = w_ffn_in.astype(BF16)
    wfo = w_ffn_out.astype(BF16)
    bst = cmlp_bs.transpose(0, 2, 1)
    tile = lambda a, n: jnp.tile(a, (1, n)).reshape(depth, 1, SEG)
    qn, kn = tile(diff_qn, SEG // C_DQK), tile(diff_kn, SEG // C_DQK)
    mqn, mkn = tile(mem_qn, HEADS), tile(mem_kn, HEADS)
    row = lambda a: a.reshape(depth, 1, -1)
    g_mix, g_ffn, g_mem = row(norm_mix), row(norm_ffn), row(mem_norm)
    g_ret, g_cm, g_hn = row(ret_norm), row(cmlp_norm), row(diff_hn)

    cache_kt = cache_diff_k.transpose(0, 1, 3, 4, 5, 2).reshape(depth, n_phys, SEG, page)
    cache_v = cache_diff_v.reshape(depth, n_phys, page * HEADS, HEAD_W)
    mem_k_s = cache_mem_k.reshape(depth, db, n_mem * HEADS, M_DH)
    mem_v_s = cache_mem_v.reshape(depth, db, n_mem * HEADS, M_DH)
    state_s = state_ret.reshape(depth, db, HEADS * R_DK, HEAD_W)

    xp = x_prompt.reshape(batch * seq, D_MODEL)
    xs = x_sample.reshape(db, D_MODEL)
    mem = mem_prompt.reshape(batch * n_mem, D_MODEL)

    ckt_all = jnp.zeros((depth, batch, SEG, seq), F32)
    cv4_all = jnp.zeros((depth, batch * seq * HEADS, HEAD_W), F32)
    mk_all = jnp.zeros((depth, batch * n_mem * HEADS, M_DH), F32)
    mv_all = jnp.zeros((depth, batch * n_mem * HEADS, M_DH), F32)

    pst = []
    sk, sv, sst, scv = [], [], [], []
    for l in range(depth):
        lam_init = 0.8 - 0.6 * math.exp(-0.3 * l)
        lw = diff_lambda_w[l]

        rqk, rv, rg, o_cm, cq, ckt_all, ckb, cv4_all, cvb, mq = _proj_in_prompt(
            xp, g_mix[l], w9[l], g_cm[l], cmlp_ws[l], bst[l], qn[l], kn[l], mqn[l], ckt_all, cv4_all,
            layer=l, seq=seq)
        o_ret, st_p = _retention_prompt(rqk, rv, rg, g_ret[l], batch, seq)
        o_df = _diff_attn_prompt(lw, cq, ckb, cvb, g_hn[l], batch, seq, lam_init)
        mk_all, mv_all = _mem_kv(mem, g_mem[l], wmem[l], mkn[l], mk_all, mv_all, layer=l)
        o_mm = _mem_attn_prompt(l, mq, mk_all, mv_all, batch, seq, n_mem)
        xp = _merge(xp, g_mix[l], (o_ret, o_cm, o_df, o_mm), wgl[l], wb[l], wo[l])
        xp = _ffn(xp, g_ffn[l], wfi[l], wfo[l])
        pst.append(st_p.reshape(batch, HEADS, R_DK, HEAD_W))

        rqk, rv, rg, o_cm, vn_s, cq, ck, cv, mq = _proj_in_sample(
            xs, g_mix[l], w9[l], g_cm[l], cmlp_ws[l], bst[l], qn[l], kn[l], mqn[l],
            off=past_len % CHUNK)
        o_ret, o_mm, st_s = _sample_mix(l, rqk, rv, rg, mq, state_s, mem_k_s, mem_v_s, g_ret[l], n_mem)
        o_df = _paged_attn(l, page_table, lw, cq, ck, cv, g_hn[l], cache_kt, cache_v, lam_init)
        xs = _merge(xs, g_mix[l], (o_ret.reshape(db, SEG), o_cm, o_df.reshape(db, SEG), o_mm.reshape(db, SEG)),
                    wgl[l], wb[l], wo[l])
        xs = _ffn(xs, g_ffn[l], wfi[l], wfo[l])
        sk.append(ck.reshape(db, 1, HEADS, 2, C_DQK))
        sv.append(cv.reshape(db, 1, HEADS, HEAD_W))
        sst.append(st_s.reshape(db, HEADS, R_DK, HEAD_W))
        scv.append(vn_s.reshape(db, 1, SEG))

    new_k = ckt_all.reshape(depth, batch, HEADS, 2, C_DQK, seq).transpose(0, 1, 5, 2, 3, 4)
    new_v = cv4_all.reshape(depth, batch, seq, HEADS, HEAD_W)
    new_mk = mk_all.reshape(depth, batch, n_mem, HEADS, M_DH)
    new_mv = mv_all.reshape(depth, batch, n_mem, HEADS, M_DH)
    return (xp.reshape(batch, seq, D_MODEL), xs.reshape(db, 1, D_MODEL),
            new_k, new_v, new_mk, new_mv, jnp.stack(pst),
            jnp.stack(sk), jnp.stack(sv), jnp.stack(sst), jnp.stack(scv))
```

```python
import functools
import math

import jax
import jax.numpy as jnp
from jax import lax
from jax.experimental import pallas as pl
from jax.experimental.pallas import tpu as pltpu

F32 = jnp.float32
BF16 = jnp.bfloat16

EPS = 1e-6
NEG_INF = -1e30
LOG2E = math.log2(math.e)

D_MODEL = 1024
SEG = 512
N_SEG = 9
HEADS = 4
R_DK = 64
HEAD_W = 128
CHUNK = 128
C_DQK = 64
M_DH = 128
N_BRANCH = 4
D_FF = 2816
FFN_CHUNKS = ((0, 1024), (1024, 2048), (2048, 2816))

VMEM_LIMIT_BYTES = 56 * 1024 * 1024


def _cparams(*sem):
    return pltpu.CompilerParams(dimension_semantics=sem, vmem_limit_bytes=VMEM_LIMIT_BYTES)


def _dot(a, b):
    return jnp.dot(a, b, preferred_element_type=F32)


def _dot_nt(a, b):
    return lax.dot_general(a, b, (((1,), (1,)), ((), ())), preferred_element_type=F32)


def _dot_tn(a, b):
    return lax.dot_general(a, b, (((0,), (0,)), ((), ())), preferred_element_type=F32)


def _sigmoid(x):
    return 1.0 / (1.0 + jnp.exp(-x))


def _rms(x, g):
    return x * lax.rsqrt(jnp.mean(x * x, axis=-1, keepdims=True) + EPS) * g


def _layer_norm(x, g):
    xc = x - jnp.mean(x, axis=-1, keepdims=True)
    return xc * lax.rsqrt(jnp.mean(xc * xc, axis=-1, keepdims=True) + EPS) * g


def _head_rms(y, g, group):
    lane = lax.broadcasted_iota(jnp.int32, (1, HEAD_W), 1)
    outs = []
    for hb in range(SEG // HEAD_W):
        blk = y[:, hb * HEAD_W:(hb + 1) * HEAD_W]
        sq = blk * blk
        if group == HEAD_W:
            ms = jnp.mean(sq, axis=-1, keepdims=True)
        else:
            lo = jnp.sum(jnp.where(lane < group, sq, 0.0), axis=-1, keepdims=True)
            hi = jnp.sum(jnp.where(lane < group, 0.0, sq), axis=-1, keepdims=True)
            ms = jnp.where(lane < group, lo, hi) * (1.0 / group)
        outs.append(blk * lax.rsqrt(ms + EPS))
    return jnp.concatenate(outs, axis=-1) * g


def _diff_lambda(wl, lam_init):
    a = jnp.sum(wl[0:1] * wl[1:2], axis=-1, keepdims=True)
    b = jnp.sum(wl[2:3] * wl[3:4], axis=-1, keepdims=True)
    return jnp.exp(a) - jnp.exp(b) + lam_init


def _layer_spec(layer, shape):
    nd = len(shape)
    return pl.BlockSpec((None,) + tuple(shape), lambda *_: (layer,) + (0,) * nd,
                        pipeline_mode=pl.Buffered(1))


def _proj_kernel(x_ref, g_ref, w_ref, cn_ref, ws_ref, bst_ref, qn_ref, kn_ref, mqn_ref, *rest,
                 prompt, tm, off):
    if prompt:
        rqk_o, rv_o, rg_o, ocm_o, cq_o, ckt_o, ckb_o, cv4_o, cvb_o, mq_o = rest[2:]
    else:
        rqk_o, rv_o, rg_o, ocm_o, vn_o, cq_o, ck_o, cv_o, mq_o = rest
    hb = _rms(x_ref[...], g_ref[...]).astype(BF16)

    def seg(s):
        return _dot(hb, w_ref[:, s * SEG:(s + 1) * SEG])

    lane = lax.broadcasted_iota(jnp.int32, (1, SEG), 1)
    y = seg(0) * jnp.where(lane < HEADS * R_DK, 1.0, R_DK ** -0.5)
    rqk_o[...] = y.astype(rqk_o.dtype)
    rv_o[...] = seg(1)
    rg_o[...] = seg(2)

    gu = seg(3)
    vn = _layer_norm(seg(4), cn_ref[...])
    if prompt:
        vnb = vn.astype(BF16)
        row = lax.broadcasted_iota(jnp.int32, (CHUNK, CHUNK), 0)
        col = lax.broadcasted_iota(jnp.int32, (CHUNK, CHUNK), 1)
        for g in range(HEADS):
            gs = slice(g * HEAD_W, (g + 1) * HEAD_W)
            wt = jnp.where(row >= col, ws_ref[g], 0.0).astype(BF16)
            bcol = bst_ref[:, g:g + 1]
            for c in range(tm // CHUNK):
                cs = slice(c * CHUNK, (c + 1) * CHUNK)
                mix = _dot(wt, vnb[cs, gs]) + bcol
                ocm_o[cs, gs] = (gu[cs, gs] * mix).astype(BF16)
    else:
        vn_o[...] = vn
        for g in range(HEADS):
            gs = slice(g * HEAD_W, (g + 1) * HEAD_W)
            w00 = ws_ref[g][off:off + 1, off:off + 1]
            b0 = bst_ref[off:off + 1, g:g + 1]
            ocm_o[:, gs] = (gu[:, gs] * (w00 * vn[:, gs] + b0)).astype(BF16)

    cq_o[...] = (_head_rms(seg(5), qn_ref[...], C_DQK) * (C_DQK ** -0.5 * LOG2E)).astype(BF16)
    ck = _head_rms(seg(6), kn_ref[...], C_DQK)
    cv = seg(7)
    if prompt:
        ckt_o[...] = ck.T
        for h in range(HEADS):
            cv4_o[pl.ds(h, tm, stride=HEADS), :] = cv[:, h * HEAD_W:(h + 1) * HEAD_W]
        ckb_o[...] = ck.astype(BF16)
        cvb_o[...] = cv.astype(BF16)
    else:
        ck_o[...] = ck
        cv_o[...] = cv
    mq_o[...] = (_head_rms(seg(8), mqn_ref[...], M_DH) * (M_DH ** -0.5)).astype(BF16)


def _proj_in_specs(layer, tm):
    row = lambda i: (i, 0)
    return [pl.BlockSpec((tm, D_MODEL), row), _layer_spec(layer, (1, D_MODEL)),
            _layer_spec(layer, (D_MODEL, N_SEG * SEG)), _layer_spec(layer, (1, SEG)),
            _layer_spec(layer, (HEADS, CHUNK, CHUNK)), _layer_spec(layer, (CHUNK, HEADS)),
            _layer_spec(layer, (1, SEG)), _layer_spec(layer, (1, SEG)), _layer_spec(layer, (1, SEG))]


def _proj_in_sample(x, g, w9, cn, ws, bst, qn, kn, mqn, *, layer, off):
    m = x.shape[0]
    blk = pl.BlockSpec((m, SEG), lambda i: (i, 0))
    dts = (F32, F32, F32, BF16, F32, BF16, F32, F32, BF16)
    return pl.pallas_call(
        functools.partial(_proj_kernel, prompt=False, tm=m, off=off),
        grid=(1,),
        in_specs=_proj_in_specs(layer, m),
        out_specs=[blk] * len(dts),
        out_shape=[jax.ShapeDtypeStruct((m, SEG), dt) for dt in dts],
        compiler_params=_cparams("parallel"),
        name="proj_in_sample",
    )(x, g, w9, cn, ws, bst, qn, kn, mqn)


def _proj_in_prompt(x, g, w9, cn, ws, bst, qn, kn, mqn, ckt_all, cv4_all, *, layer, seq):
    m = x.shape[0]
    tm = 512
    nq = seq // tm
    blk = pl.BlockSpec((tm, SEG), lambda i: (i, 0))
    cktblk = pl.BlockSpec((None, None, SEG, tm), lambda i: (layer, i // nq, 0, i % nq))
    cv4blk = pl.BlockSpec((None, tm * HEADS, HEAD_W), lambda i: (layer, i, 0))
    sds = lambda dt: jax.ShapeDtypeStruct((m, SEG), dt)
    in_specs = _proj_in_specs(layer, tm)
    n_in = len(in_specs)
    return pl.pallas_call(
        functools.partial(_proj_kernel, prompt=True, tm=tm, off=0),
        grid=(m // tm,),
        in_specs=in_specs + [pl.BlockSpec(memory_space=pl.ANY)] * 2,
        out_specs=[blk, blk, blk, blk, blk, cktblk, blk, cv4blk, blk, blk],
        out_shape=[sds(BF16), sds(F32), sds(F32), sds(BF16), sds(BF16),
                   jax.ShapeDtypeStruct(ckt_all.shape, F32), sds(BF16),
                   jax.ShapeDtypeStruct(cv4_all.shape, F32), sds(BF16), sds(BF16)],
        input_output_aliases={n_in: 5, n_in + 1: 7},
        compiler_params=_cparams("parallel"),
        name="proj_in_prompt",
    )(x, g, w9, cn, ws, bst, qn, kn, mqn, ckt_all, cv4_all)


RET_BATCH = 4


def _ret_log_decay(h):
    return math.log1p(-(2.0 ** (-5 - h)))


def _ret_finish(o, gate, gn):
    return gate * _sigmoid(gate) * _layer_norm(o, gn)


def _retention_kernel(rqk_ref, rv_ref, rg_ref, gn_ref, o_ref, st_o, st_ref, dec_ref, qd_ref, kd_ref):
    c = pl.program_id(1)

    @pl.when(c == 0)
    def _():
        st_ref[...] = jnp.zeros_like(st_ref)
        i = lax.broadcasted_iota(jnp.int32, (CHUNK, CHUNK), 0).astype(F32)
        j = lax.broadcasted_iota(jnp.int32, (CHUNK, CHUNK), 1).astype(F32)
        for h in range(HEADS):
            lg = _ret_log_decay(h)
            dec_ref[h] = jnp.where(i >= j, jnp.exp(lg * jnp.maximum(i - j, 0.0)), 0.0)
            qd_ref[h] = jnp.exp(lg * (i + 1.0))
            kd_ref[h] = jnp.exp(lg * (CHUNK - 1.0 - i))

    gn = gn_ref[...]
    lane = lax.broadcasted_iota(jnp.int32, (1, HEADS * R_DK), 1)
    for b in range(RET_BATCH):
        qk = rqk_ref[b]
        q_all = qk[:, :HEADS * R_DK]
        k_all = qk[:, HEADS * R_DK:]
        v_all = rv_ref[b]
        rg = rg_ref[b]
        st = st_ref[b]
        stb = st.astype(BF16)
        zero = jnp.zeros_like(q_all)
        vk_parts = []
        for h in range(HEADS):
            hs = slice(h * HEAD_W, (h + 1) * HEAD_W)
            v = v_all[:, hs]
            qm = jnp.where((lane >= h * R_DK) & (lane < (h + 1) * R_DK), q_all, zero)
            s = _dot_nt(qm, k_all) * dec_ref[h]
            intra = _dot(s.astype(BF16), v.astype(BF16))
            cross = _dot(qm, stb) * qd_ref[h]
            o_ref[b, :, hs] = _ret_finish(intra + cross, rg[:, hs], gn[:, hs]).astype(BF16)
            vk_parts.append((v * kd_ref[h]).astype(BF16))
        kv = _dot_tn(k_all, jnp.concatenate(vk_parts, axis=-1))
        for h in range(HEADS):
            rs = slice(h * R_DK, (h + 1) * R_DK)
            st_ref[b, rs, :] = (st[rs, :] * math.exp(_ret_log_decay(h) * CHUNK)
                                + kv[rs, h * HEAD_W:(h + 1) * HEAD_W])

    @pl.when(c == pl.num_programs(1) - 1)
    def _():
        st_o[...] = st_ref[...]


def _retention_prompt(layer, rqk, rv, rg, gn, batch, seq):
    nc = seq // CHUNK
    blk = pl.BlockSpec((RET_BATCH, CHUNK, SEG), lambda b, c: (b, c, 0))
    stblk = pl.BlockSpec((RET_BATCH, HEADS * R_DK, HEAD_W), lambda b, c: (b, 0, 0))
    tbl = pltpu.VMEM((HEADS, CHUNK, CHUNK), F32)
    r3 = lambda a: a.reshape(batch, seq, SEG)
    o_ret, st = pl.pallas_call(
        _retention_kernel,
        grid=(batch // RET_BATCH, nc),
        in_specs=[blk, blk, blk, _layer_spec(layer, (1, SEG))],
        out_specs=[blk, stblk],
        out_shape=[jax.ShapeDtypeStruct((batch, seq, SEG), BF16),
                   jax.ShapeDtypeStruct((batch, HEADS * R_DK, HEAD_W), F32)],
        scratch_shapes=[pltpu.VMEM((RET_BATCH, HEADS * R_DK, HEAD_W), F32), tbl, tbl, tbl],
        compiler_params=_cparams("parallel", "arbitrary"),
        name="retention_prompt",
    )(r3(rqk), r3(rv), r3(rg), gn)
    return o_ret.reshape(batch * seq, SEG), st


def _diff_finish(o0, o1, lam, hn, lam_init):
    o = o0 - lam * o1
    return _rms(o, hn) * (1.0 - lam_init)


def _alibi_slope_log2(h, shape):
    return jnp.exp2(jnp.full(shape, -8.0 / HEADS, F32) * (h + 1).astype(F32)) * LOG2E


def _diff_attn_kernel(lw_ref, q_ref, k_ref, v_ref, hn_ref, o_ref, vt_ref, acc_ref, s_ref, *, tq, nblk, lam_init):
    h = pl.program_id(1)
    i = pl.program_id(2)

    @pl.when(i == 0)
    def _():
        for t in range(nblk):
            vt_ref[t] = v_ref[t * tq:(t + 1) * tq, :].astype(F32).T.astype(BF16)

    q = q_ref[...]
    lane = lax.broadcasted_iota(jnp.int32, (1, HEAD_W), 1)
    zero = jnp.zeros_like(q)
    qs = (jnp.where(lane < C_DQK, q, zero), jnp.where(lane < C_DQK, zero, q))
    krow = lax.broadcasted_iota(jnp.int32, (tq, HEAD_W), 0).astype(F32)
    brep = _alibi_slope_log2(h, (tq, HEAD_W)) * krow
    bias = jnp.concatenate([brep] * (tq // HEAD_W), axis=1)
    slope_row = _alibi_slope_log2(h, (1, tq))
    rowi = lax.broadcasted_iota(jnp.int32, (tq, tq), 0)
    coli = lax.broadcasted_iota(jnp.int32, (tq, tq), 1)
    acc_ref[...] = jnp.zeros_like(acc_ref)

    def scores(j):
        kb = k_ref[pl.ds(pl.multiple_of(j * tq, tq), tq), :]
        for c in range(2):
            s_ref[j % 2, c] = _dot_nt(kb, qs[c])

    def block(j, carry, masked):
        vtb = vt_ref[j]
        off = slope_row * ((j - i) * tq).astype(F32)
        new = []
        for c in range(2):
            m_prev, l_prev = carry[2 * c], carry[2 * c + 1]
            s = s_ref[j % 2, c] + bias
            if masked:
                s = jnp.where(coli >= rowi, s, NEG_INF)
            m_new = jnp.maximum(m_prev, jnp.max(s, axis=0, keepdims=True) + off)
            alpha = jnp.exp2(m_prev - m_new)
            p = jnp.exp2(s - (m_new - off))
            l_new = alpha * l_prev + jnp.sum(p, axis=0, keepdims=True)
            acc_ref[c] = alpha * acc_ref[c] + _dot(vtb, p.astype(BF16))
            new += [m_new, l_new]
        return tuple(new)

    def body(j, carry):
        new = block(j, carry, False)
        scores(j + 1)
        return new

    neg = jnp.full((1, tq), NEG_INF, F32)
    zer = jnp.zeros((1, tq), F32)
    scores(0)
    carry = lax.fori_loop(0, i, body, (neg, zer, neg, zer))
    _, l0, _, l1 = block(i, carry, True)

    lam = _diff_lambda(lw_ref[...], lam_init)
    o = acc_ref[0] / l0 - lam * (acc_ref[1] / l1)
    y = o * lax.rsqrt(jnp.mean(o * o, axis=0, keepdims=True) + EPS)
    o_ref[...] = (y.T * hn_ref[...] * (1.0 - lam_init)).astype(BF16)


def _diff_attn_prompt(layer, lw, cq, ckb, cvb, hn, batch, seq, lam_init):
    tq = 512
    nq = seq // tq
    qblk = pl.BlockSpec((tq, HEAD_W), lambda b, h, i: (b * nq + i, h))
    kvblk = pl.BlockSpec((seq, HEAD_W), lambda b, h, i: (b, h))
    return pl.pallas_call(
        functools.partial(_diff_attn_kernel, tq=tq, nblk=nq, lam_init=lam_init),
        grid=(batch, HEADS, nq),
        in_specs=[_layer_spec(layer, (4, C_DQK)), qblk, kvblk, kvblk,
                  pl.BlockSpec((None, 1, HEAD_W), lambda b, h, i: (layer, 0, h))],
        out_specs=qblk,
        out_shape=jax.ShapeDtypeStruct((batch * seq, SEG), BF16),
        scratch_shapes=[pltpu.VMEM((nq, HEAD_W, tq), BF16), pltpu.VMEM((2, HEAD_W, tq), F32),
                        pltpu.VMEM((2, 2, tq, tq), F32)],
        compiler_params=_cparams("parallel", "parallel", "arbitrary"),
        name="diff_attn_prompt",
    )(lw, cq, ckb, cvb, hn)


def _mem_kv_kernel(x_ref, g_ref, w_ref, kn_ref, k_all, v_all, k_o, v_o, *, tm):
    del k_all, v_all
    hb = _rms(x_ref[...], g_ref[...]).astype(BF16)
    k = _head_rms(_dot(hb, w_ref[:, :SEG]), kn_ref[...], M_DH)
    v = _dot(hb, w_ref[:, SEG:])
    for h in range(HEADS):
        hs = slice(h * HEAD_W, (h + 1) * HEAD_W)
        k_o[pl.ds(h, tm, stride=HEADS), :] = k[:, hs]
        v_o[pl.ds(h, tm, stride=HEADS), :] = v[:, hs]


def _mem_kv(mem, g, w2, kn, mk_all, mv_all, *, layer):
    m = mem.shape[0]
    tm = 512
    row = lambda i: (i, 0)
    blk = pl.BlockSpec((None, tm * HEADS, M_DH), lambda i: (layer, i, 0))
    return pl.pallas_call(
        functools.partial(_mem_kv_kernel, tm=tm),
        grid=(m // tm,),
        in_specs=[pl.BlockSpec((tm, D_MODEL), row), _layer_spec(layer, (1, D_MODEL)),
                  _layer_spec(layer, (D_MODEL, 2 * SEG)), _layer_spec(layer, (1, SEG)),
                  pl.BlockSpec(memory_space=pl.ANY), pl.BlockSpec(memory_space=pl.ANY)],
        out_specs=[blk, blk],
        out_shape=[jax.ShapeDtypeStruct(mk_all.shape, F32)] * 2,
        input_output_aliases={4: 0, 5: 1},
        compiler_params=_cparams("parallel"),
        name="mem_kv",
    )(mem, g, w2, kn, mk_all, mv_all)


def _mem_attend_head(qh, kh, vh):
    s = _dot_nt(qh, kh.astype(BF16))
    p = jnp.exp(s - jnp.max(s, axis=-1, keepdims=True))
    o = _dot(p.astype(BF16), vh.astype(BF16))
    return o / jnp.sum(p, axis=-1, keepdims=True)


def _mem_attn_kernel(q_ref, k_ref, v_ref, o_ref, *, n_mem):
    for h in range(HEADS):
        hs = slice(h * HEAD_W, (h + 1) * HEAD_W)
        kh = k_ref[pl.ds(h, n_mem, stride=HEADS), :]
        vh = v_ref[pl.ds(h, n_mem, stride=HEADS), :]
        o_ref[:, hs] = _mem_attend_head(q_ref[:, hs], kh, vh).astype(BF16)


def _mem_attn_prompt(layer, mq, mk_all, mv_all, batch, seq, n_mem):
    tq = 512
    nq = seq // tq
    qblk = pl.BlockSpec((tq, SEG), lambda i: (i, 0))
    kvblk = pl.BlockSpec((None, n_mem * HEADS, M_DH), lambda i: (layer, i // nq, 0))
    return pl.pallas_call(
        functools.partial(_mem_attn_kernel, n_mem=n_mem),
        grid=(batch * nq,),
        in_specs=[qblk, kvblk, kvblk],
        out_specs=qblk,
        out_shape=jax.ShapeDtypeStruct((batch * seq, SEG), BF16),
        compiler_params=_cparams("parallel"),
        name="mem_attn_prompt",
    )(mq, mk_all, mv_all)


GATE_SEGS = N_BRANCH * D_MODEL // SEG


def _merge_kernel(x_ref, g_ref, b0_ref, b1_ref, b2_ref, b3_ref, *rest):
    wg_refs = rest[:GATE_SEGS]
    wb_ref, wo_ref, o_ref = rest[GATE_SEGS:]
    per = GATE_SEGS // N_BRANCH
    x = x_ref[...]
    hb = _rms(x, g_ref[...]).astype(BF16)
    merged = None
    for n, b_ref in enumerate((b0_ref, b1_ref, b2_ref, b3_ref)):
        logits = jnp.concatenate([_dot(hb, wg_refs[n * per + k][...]) for k in range(per)], axis=1)
        gate = _sigmoid(logits)
        term = gate * _dot(b_ref[...], wb_ref[n])
        merged = term if merged is None else merged + term
    o_ref[...] = x + _dot(merged.astype(BF16), wo_ref[...])


def _merge(layer, x, g, branches, w_in, wb, wo):
    m = x.shape[0]
    tm = min(512, m)
    row = lambda i: (i, 0)
    xblk = pl.BlockSpec((tm, D_MODEL), row)
    bblk = pl.BlockSpec((tm, SEG), row)

    def gate_spec(k):
        return pl.BlockSpec((None, D_MODEL, SEG), lambda i: (layer, 0, N_SEG + k), pipeline_mode=pl.Buffered(1))

    return pl.pallas_call(
        _merge_kernel,
        grid=(m // tm,),
        in_specs=[xblk, _layer_spec(layer, (1, D_MODEL)), bblk, bblk, bblk, bblk]
                 + [gate_spec(k) for k in range(GATE_SEGS)]
                 + [_layer_spec(layer, (N_BRANCH, SEG, D_MODEL)), _layer_spec(layer, (D_MODEL, D_MODEL))],
        out_specs=xblk,
        out_shape=jax.ShapeDtypeStruct((m, D_MODEL), F32),
        compiler_params=_cparams("parallel"),
        name="merge",
    )(x, g, *branches, *([w_in] * GATE_SEGS), wb, wo)


def _ffn_kernel(x_ref, g_ref, wi_ref, wo_ref, o_ref):
    x = x_ref[...]
    hb = _rms(x, g_ref[...]).astype(BF16)
    acc = x
    for a, b in FFN_CHUNKS:
        gate = _dot(hb, wi_ref[:, a:b])
        up = _dot(hb, wi_ref[:, D_FF + a:D_FF + b])
        act = (gate * _sigmoid(gate) * up).astype(BF16)
        acc = acc + _dot(act, wo_ref[a:b, :])
    o_ref[...] = acc


def _ffn(layer, x, g, wi, wo):
    m = x.shape[0]
    tm = min(512, m)
    row = lambda i: (i, 0)
    xblk = pl.BlockSpec((tm, D_MODEL), row)
    return pl.pallas_call(
        _ffn_kernel,
        grid=(m // tm,),
        in_specs=[xblk, _layer_spec(layer, (1, D_MODEL)), _layer_spec(layer, (D_MODEL, 2 * D_FF)),
                  _layer_spec(layer, (D_FF, D_MODEL))],
        out_specs=xblk,
        out_shape=jax.ShapeDtypeStruct((m, D_MODEL), F32),
        compiler_params=_cparams("parallel"),
        name="ffn",
    )(x, g, wi, wo)


PAD_ROWS = 16


def _sample_mix_kernel(rqk_ref, rv_ref, rg_ref, mq_ref, st_ref, mk_ref, mv_ref, gn_ref,
                       oret_o, omm_o, st_o, *, n_mem):
    qk = rqk_ref[...]
    q = qk[:, :HEADS * R_DK]
    k = qk[:, HEADS * R_DK:]
    v = rv_ref[...]
    rg = rg_ref[...]
    gn = gn_ref[...]
    st = st_ref[...]
    stb = st.astype(BF16)
    lane = lax.broadcasted_iota(jnp.int32, (1, HEADS * R_DK), 1)

    def first_row(a):
        r = lax.broadcasted_iota(jnp.int32, (PAD_ROWS, a.shape[1]), 0)
        return jnp.where(r == 0, jnp.broadcast_to(a, (PAD_ROWS, a.shape[1])), 0.0).astype(BF16)

    kv = _dot_tn(first_row(k), first_row(v))
    for h in range(HEADS):
        hs = slice(h * HEAD_W, (h + 1) * HEAD_W)
        rs = slice(h * R_DK, (h + 1) * R_DK)
        gamma = math.exp(_ret_log_decay(h))
        qm = jnp.where((lane >= h * R_DK) & (lane < (h + 1) * R_DK), q, 0.0)
        score = jnp.sum(qm * k, axis=-1, keepdims=True)
        cross = _dot(jnp.broadcast_to(qm, (PAD_ROWS, HEADS * R_DK)).astype(BF16), stb)[0:1] * gamma
        o = score * v[:, hs] + cross
        oret_o[:, hs] = _ret_finish(o, rg[:, hs], gn[:, hs]).astype(BF16)
        st_o[rs, :] = st[rs, :] * gamma + kv[rs, hs]

    mq = mq_ref[...]
    for h in range(HEADS):
        hs = slice(h * HEAD_W, (h + 1) * HEAD_W)
        qh = jnp.broadcast_to(mq[:, hs], (PAD_ROWS, HEAD_W))
        kh = mk_ref[pl.ds(h, n_mem, stride=HEADS), :]
        vh = mv_ref[pl.ds(h, n_mem, stride=HEADS), :]
        omm_o[:, hs] = _mem_attend_head(qh, kh, vh)[0:1].astype(BF16)


def _sample_mix(layer, rqk, rv, rg, mq, state, mem_k, mem_v, gn, n_mem):
    db = rqk.shape[0]
    row = pl.BlockSpec((None, 1, SEG), lambda b: (b, 0, 0))
    stblk = pl.BlockSpec((None, None, HEADS * R_DK, HEAD_W), lambda b: (layer, b, 0, 0))
    memblk = pl.BlockSpec((None, None, n_mem * HEADS, M_DH), lambda b: (layer, b, 0, 0))
    r3 = lambda a: a.reshape(db, 1, SEG)
    return pl.pallas_call(
        functools.partial(_sample_mix_kernel, n_mem=n_mem),
        grid=(db,),
        in_specs=[row, row, row, row, stblk, memblk, memblk, _layer_spec(layer, (1, SEG))],
        out_specs=[row, row, pl.BlockSpec((None, HEADS * R_DK, HEAD_W), lambda b: (b, 0, 0))],
        out_shape=[jax.ShapeDtypeStruct((db, 1, SEG), BF16), jax.ShapeDtypeStruct((db, 1, SEG), BF16),
                   jax.ShapeDtypeStruct((db, HEADS * R_DK, HEAD_W), F32)],
        compiler_params=_cparams("parallel"),
        name="sample_mix",
    )(r3(rqk), r3(rv), r3(rg), r3(mq), state, mem_k, mem_v, gn)


N_MAPS = 2 * HEADS
PAGES_PER_STEP = 16


def _paged_attn_kernel(pt_ref, lw_ref, q_ref, kn_ref, vn_ref, hn_ref, *rest, page, past_len, lam_init):
    del pt_ref
    npg = PAGES_PER_STEP
    k_refs = rest[:npg]
    v_refs = rest[npg:2 * npg]
    o_ref, m_ref, l_ref, acc_ref = rest[2 * npg:]
    s_idx = pl.program_id(1)

    @pl.when(s_idx == 0)
    def _():
        m_ref[...] = jnp.full_like(m_ref, NEG_INF)
        l_ref[...] = jnp.zeros_like(l_ref)
        acc_ref[...] = jnp.zeros_like(acc_ref)

    r8 = lax.broadcasted_iota(jnp.int32, (N_MAPS, SEG), 0)
    l8 = lax.broadcasted_iota(jnp.int32, (N_MAPS, SEG), 1)
    sel = (l8 >= r8 * C_DQK) & (l8 < (r8 + 1) * C_DQK)
    q8f = jnp.where(sel, jnp.broadcast_to(q_ref[...].astype(F32), (N_MAPS, SEG)), 0.0)
    q8 = q8f.astype(BF16)
    hrow = jnp.right_shift(lax.broadcasted_iota(jnp.int32, (N_MAPS, 1), 0), 1)
    hrow_w = jnp.right_shift(lax.broadcasted_iota(jnp.int32, (N_MAPS, HEAD_W), 0), 1)
    slope = jnp.exp2((-8.0 / HEADS) * (hrow + 1).astype(F32)) * LOG2E
    kpos = s_idx * (npg * page) + lax.broadcasted_iota(jnp.int32, (1, npg * page), 1)
    bias = slope * (kpos - past_len).astype(F32)
    kt = jnp.concatenate([k_refs[p][...].astype(BF16) for p in range(npg)], axis=1)
    s = _dot(q8, kt) + bias
    m_prev = m_ref[...]
    m_new = jnp.maximum(m_prev, jnp.max(s, axis=-1, keepdims=True))
    alpha = jnp.exp2(m_prev - m_new)
    pr = jnp.exp2(s - m_new)
    l_ref[...] = alpha * l_ref[...] + jnp.sum(pr, axis=-1, keepdims=True)
    acc = alpha * acc_ref[...]
    prb = pr.astype(BF16)
    for h in range(HEADS):
        vh = jnp.concatenate([v_refs[p][pl.ds(h, page, stride=HEADS), :].astype(BF16) for p in range(npg)], axis=0)
        acc = acc + jnp.where(hrow_w == h, _dot(prb, vh), 0.0)
    acc_ref[...] = acc
    m_ref[...] = m_new

    @pl.when(s_idx == pl.num_programs(1) - 1)
    def _():
        s_new = jnp.sum(q8f * kn_ref[...], axis=-1, keepdims=True)
        m_fin = jnp.maximum(m_ref[...], s_new)
        a = jnp.exp2(m_ref[...] - m_fin)
        p_new = jnp.exp2(s_new - m_fin)
        l_fin = a * l_ref[...] + p_new
        vn = vn_ref[...]
        vn8 = jnp.zeros((N_MAPS, HEAD_W), F32)
        for h in range(HEADS):
            vn8 = jnp.where(hrow_w == h, jnp.broadcast_to(vn[:, h * HEAD_W:(h + 1) * HEAD_W], (N_MAPS, HEAD_W)), vn8)
        o_all = (a * acc_ref[...] + p_new * vn8) / l_fin
        lam = _diff_lambda(lw_ref[...], lam_init)
        hn = hn_ref[...]
        for h in range(HEADS):
            hs = slice(h * HEAD_W, (h + 1) * HEAD_W)
            o = _diff_finish(o_all[2 * h:2 * h + 1], o_all[2 * h + 1:2 * h + 2], lam, hn[:, hs], lam_init)
            o_ref[:, hs] = o.astype(BF16)


def _paged_attn(layer, page_table, lw, cq, ck, cv, hn, cache_kt, cache_v, lam_init):
    db, n_pages = page_table.shape
    page = cache_kt.shape[3]
    npg = PAGES_PER_STEP
    row = pl.BlockSpec((None, 1, SEG), lambda b, s, pt: (b, 0, 0))

    def page_spec(p):
        return pl.BlockSpec((None, None, SEG, page),
                            lambda b, s, pt: (layer, pt[b * n_pages + s * npg + p], 0, 0))

    r3 = lambda a: a.reshape(db, 1, SEG)
    grid_spec = pltpu.PrefetchScalarGridSpec(
        num_scalar_prefetch=1,
        grid=(db, n_pages // npg),
        in_specs=[pl.BlockSpec((None, 4, C_DQK), lambda b, s, pt: (layer, 0, 0)), row, row, row,
                  pl.BlockSpec((None, 1, SEG), lambda b, s, pt: (layer, 0, 0))]
                 + [page_spec(p) for p in range(npg)] * 2,
        out_specs=row,
        scratch_shapes=[pltpu.VMEM((N_MAPS, 1), F32), pltpu.VMEM((N_MAPS, 1), F32),
                        pltpu.VMEM((N_MAPS, HEAD_W), F32)],
    )
    return pl.pallas_call(
        functools.partial(_paged_attn_kernel, page=page, past_len=n_pages * page, lam_init=lam_init),
        grid_spec=grid_spec,
        out_shape=jax.ShapeDtypeStruct((db, 1, SEG), BF16),
        compiler_params=_cparams("parallel", "arbitrary"),
        name="paged_diff_attn",
    )(page_table.reshape(-1), lw, r3(cq), r3(ck), r3(cv), hn,
      *([cache_kt] * npg), *([cache_v] * npg))


def kernel(x_prompt, x_sample, mem_prompt, cache_diff_k, cache_diff_v, page_table, cache_mem_k, cache_mem_v, state_ret, norm_mix, w_in, ret_norm, cmlp_norm, cmlp_ws, cmlp_bs, diff_qn, diff_kn, diff_lambda_w, diff_hn, mem_norm, w_mem_kv, mem_qn, mem_kn, w_branch, w_out, norm_ffn, w_ffn_in, w_ffn_out):
    batch, seq, _ = x_prompt.shape
    db, dec_seq, _ = x_sample.shape
    assert dec_seq == 1, "the sample group decodes one token per sequence"
    depth = w_in.shape[0]
    n_mem = mem_prompt.shape[1]
    n_phys, page = cache_diff_k.shape[1:3]
    past_len = page_table.shape[1] * page
    n_main = N_SEG * SEG

    w9 = wgl = w_in.astype(BF16)
    wmem = w_mem_kv.astype(BF16)
    wb = w_branch.astype(BF16)
    wo = w_out.astype(BF16)
    wfi = w_ffn_in.astype(BF16)
    wfo = w_ffn_out.astype(BF16)
    bst = cmlp_bs.transpose(0, 2, 1)
    tile = lambda a, n: jnp.tile(a, (1, n)).reshape(depth, 1, SEG)
    qn, kn = tile(diff_qn, SEG // C_DQK), tile(diff_kn, SEG // C_DQK)
    mqn, mkn = tile(mem_qn, HEADS), tile(mem_kn, HEADS)
    row = lambda a: a.reshape(depth, 1, -1)
    g_mix, g_ffn, g_mem = row(norm_mix), row(norm_ffn), row(mem_norm)
    g_ret, g_cm, g_hn = row(ret_norm), row(cmlp_norm), row(diff_hn)

    cache_kt = cache_diff_k.transpose(0, 1, 3, 4, 5, 2).reshape(depth, n_phys, SEG, page)
    cache_v = cache_diff_v.reshape(depth, n_phys, page * HEADS, HEAD_W)
    mem_k_s = cache_mem_k.reshape(depth, db, n_mem * HEADS, M_DH)
    mem_v_s = cache_mem_v.reshape(depth, db, n_mem * HEADS, M_DH)
    state_s = state_ret.reshape(depth, db, HEADS * R_DK, HEAD_W)

    xp = x_prompt.reshape(batch * seq, D_MODEL)
    xs = x_sample.reshape(db, D_MODEL)
    mem = mem_prompt.reshape(batch * n_mem, D_MODEL)

    ckt_all = jnp.zeros((depth, batch, SEG, seq), F32)
    cv4_all = jnp.zeros((depth, batch * seq * HEADS, HEAD_W), F32)
    mk_all = jnp.zeros((depth, batch * n_mem * HEADS, M_DH), F32)
    mv_all = jnp.zeros((depth, batch * n_mem * HEADS, M_DH), F32)

    pst = []
    sk, sv, sst, scv = [], [], [], []
    for l in range(depth):
        lam_init = 0.8 - 0.6 * math.exp(-0.3 * l)
        lw = diff_lambda_w

        rqk, rv, rg, o_cm, cq, ckt_all, ckb, cv4_all, cvb, mq = _proj_in_prompt(
            xp, g_mix, w9, g_cm, cmlp_ws, bst, qn, kn, mqn, ckt_all, cv4_all, layer=l, seq=seq)
        o_ret, st_p = _retention_prompt(l, rqk, rv, rg, g_ret, batch, seq)
        o_df = _diff_attn_prompt(l, lw, cq, ckb, cvb, g_hn, batch, seq, lam_init)
        mk_all, mv_all = _mem_kv(mem, g_mem, wmem, mkn, mk_all, mv_all, layer=l)
        o_mm = _mem_attn_prompt(l, mq, mk_all, mv_all, batch, seq, n_mem)
        xp = _merge(l, xp, g_mix, (o_ret, o_cm, o_df, o_mm), wgl, wb, wo)
        xp = _ffn(l, xp, g_ffn, wfi, wfo)
        pst.append(st_p.reshape(batch, HEADS, R_DK, HEAD_W))

        rqk, rv, rg, o_cm, vn_s, cq, ck, cv, mq = _proj_in_sample(
            xs, g_mix, w9, g_cm, cmlp_ws, bst, qn, kn, mqn, layer=l, off=past_len % CHUNK)
        o_ret, o_mm, st_s = _sample_mix(l, rqk, rv, rg, mq, state_s, mem_k_s, mem_v_s, g_ret, n_mem)
        o_df = _paged_attn(l, page_table, lw, cq, ck, cv, g_hn, cache_kt, cache_v, lam_init)
        xs = _merge(l, xs, g_mix, (o_ret.reshape(db, SEG), o_cm, o_df.reshape(db, SEG), o_mm.reshape(db, SEG)),
                    wgl, wb, wo)
        xs = _ffn(l, xs, g_ffn, wfi, wfo)
        sk.append(ck.reshape(db, 1, HEADS, 2, C_DQK))
        sv.append(cv.reshape(db, 1, HEADS, HEAD_W))
        sst.append(st_s.reshape(db, HEADS, R_DK, HEAD_W))
        scv.append(vn_s.reshape(db, 1, SEG))

    new_k = ckt_all.reshape(depth, batch, HEADS, 2, C_DQK, seq).transpose(0, 1, 5, 2, 3, 4)
    new_v = cv4_all.reshape(depth, batch, seq, HEADS, HEAD_W)
    new_mk = mk_all.reshape(depth, batch, n_mem, HEADS, M_DH)
    new_mv = mv_all.reshape(depth, batch, n_mem, HEADS, M_DH)
    return (xp.reshape(batch, seq, D_MODEL), xs.reshape(db, 1, D_MODEL),
            new_k, new_v, new_mk, new_mv, jnp.stack(pst),
            jnp.stack(sk), jnp.stack(sv), jnp.stack(sst), jnp.stack(scv))
```

```python
import functools
import math

import jax
import jax.numpy as jnp
from jax import lax
from jax.experimental import pallas as pl
from jax.experimental.pallas import tpu as pltpu

F32 = jnp.float32
BF16 = jnp.bfloat16

EPS = 1e-6
NEG_INF = -1e30
LOG2E = math.log2(math.e)

D_MODEL = 1024
SEG = 512
N_SEG = 9
HEADS = 4
R_DK = 64
HEAD_W = 128
CHUNK = 128
C_DQK = 64
M_DH = 128
N_BRANCH = 4
D_FF = 2816
FFN_CHUNKS = ((0, 1024), (1024, 2048), (2048, 2816))

VMEM_LIMIT_BYTES = 56 * 1024 * 1024


def _cparams(*sem):
    return pltpu.CompilerParams(dimension_semantics=sem, vmem_limit_bytes=VMEM_LIMIT_BYTES)


def _dot(a, b):
    return jnp.dot(a, b, preferred_element_type=F32)


def _dot_nt(a, b):
    return lax.dot_general(a, b, (((1,), (1,)), ((), ())), preferred_element_type=F32)


def _dot_tn(a, b):
    return lax.dot_general(a, b, (((0,), (0,)), ((), ())), preferred_element_type=F32)


def _sigmoid(x):
    return 1.0 / (1.0 + jnp.exp(-x))


def _rms(x, g):
    return x * lax.rsqrt(jnp.mean(x * x, axis=-1, keepdims=True) + EPS) * g


def _layer_norm(x, g):
    xc = x - jnp.mean(x, axis=-1, keepdims=True)
    return xc * lax.rsqrt(jnp.mean(xc * xc, axis=-1, keepdims=True) + EPS) * g


def _head_rms(y, g, group):
    lane = lax.broadcasted_iota(jnp.int32, (1, HEAD_W), 1)
    outs = []
    for hb in range(SEG // HEAD_W):
        blk = y[:, hb * HEAD_W:(hb + 1) * HEAD_W]
        sq = blk * blk
        if group == HEAD_W:
            ms = jnp.mean(sq, axis=-1, keepdims=True)
        else:
            lo = jnp.sum(jnp.where(lane < group, sq, 0.0), axis=-1, keepdims=True)
            hi = jnp.sum(jnp.where(lane < group, 0.0, sq), axis=-1, keepdims=True)
            ms = jnp.where(lane < group, lo, hi) * (1.0 / group)
        outs.append(blk * lax.rsqrt(ms + EPS))
    return jnp.concatenate(outs, axis=-1) * g


def _diff_lambda(wl, lam_init):
    a = jnp.sum(wl[0:1] * wl[1:2], axis=-1, keepdims=True)
    b = jnp.sum(wl[2:3] * wl[3:4], axis=-1, keepdims=True)
    return jnp.exp(a) - jnp.exp(b) + lam_init


def _layer_spec(layer, shape):
    nd = len(shape)
    return pl.BlockSpec((None,) + tuple(shape), lambda *_: (layer,) + (0,) * nd,
                        pipeline_mode=pl.Buffered(1))


def _proj_kernel(x_ref, g_ref, w_ref, cn_ref, ws_ref, bst_ref, qn_ref, kn_ref, mqn_ref, *rest,
                 prompt, tm, off):
    if prompt:
        rqk_o, rv_o, rg_o, ocm_o, cq_o, ckt_o, ckb_o, cv4_o, cvb_o, mq_o = rest[2:]
    else:
        rqk_o, rv_o, rg_o, ocm_o, vn_o, cq_o, ck_o, cv_o, mq_o = rest
    hb = _rms(x_ref[...], g_ref[...]).astype(BF16)

    def seg(s):
        return _dot(hb, w_ref[:, s * SEG:(s + 1) * SEG])

    lane = lax.broadcasted_iota(jnp.int32, (1, SEG), 1)
    y = seg(0) * jnp.where(lane < HEADS * R_DK, 1.0, R_DK ** -0.5)
    rqk_o[...] = y.astype(rqk_o.dtype)
    rv_o[...] = seg(1)
    rg_o[...] = seg(2)

    gu = seg(3)
    vn = _layer_norm(seg(4), cn_ref[...])
    if prompt:
        vnb = vn.astype(BF16)
        row = lax.broadcasted_iota(jnp.int32, (CHUNK, CHUNK), 0)
        col = lax.broadcasted_iota(jnp.int32, (CHUNK, CHUNK), 1)
        for g in range(HEADS):
            gs = slice(g * HEAD_W, (g + 1) * HEAD_W)
            wt = jnp.where(row >= col, ws_ref[g], 0.0).astype(BF16)
            bcol = bst_ref[:, g:g + 1]
            for c in range(tm // CHUNK):
                cs = slice(c * CHUNK, (c + 1) * CHUNK)
                mix = _dot(wt, vnb[cs, gs]) + bcol
                ocm_o[cs, gs] = (gu[cs, gs] * mix).astype(BF16)
    else:
        vn_o[...] = vn
        for g in range(HEADS):
            gs = slice(g * HEAD_W, (g + 1) * HEAD_W)
            w00 = ws_ref[g][off:off + 1, off:off + 1]
            b0 = bst_ref[off:off + 1, g:g + 1]
            ocm_o[:, gs] = (gu[:, gs] * (w00 * vn[:, gs] + b0)).astype(BF16)

    cq_o[...] = (_head_rms(seg(5), qn_ref[...], C_DQK) * (C_DQK ** -0.5 * LOG2E)).astype(BF16)
    ck = _head_rms(seg(6), kn_ref[...], C_DQK)
    cv = seg(7)
    if prompt:
        ckt_o[...] = ck.T
        for h in range(HEADS):
            cv4_o[pl.ds(h, tm, stride=HEADS), :] = cv[:, h * HEAD_W:(h + 1) * HEAD_W]
        ckb_o[...] = ck.astype(BF16)
        cvb_o[...] = cv.astype(BF16)
    else:
        ck_o[...] = ck
        cv_o[...] = cv
    mq_o[...] = (_head_rms(seg(8), mqn_ref[...], M_DH) * (M_DH ** -0.5)).astype(BF16)


def _proj_in_specs(layer, tm):
    row = lambda i: (i, 0)
    return [pl.BlockSpec((tm, D_MODEL), row), _layer_spec(layer, (1, D_MODEL)),
            _layer_spec(layer, (D_MODEL, N_SEG * SEG)), _layer_spec(layer, (1, SEG)),
            _layer_spec(layer, (HEADS, CHUNK, CHUNK)), _layer_spec(layer, (CHUNK, HEADS)),
            _layer_spec(layer, (1, SEG)), _layer_spec(layer, (1, SEG)), _layer_spec(layer, (1, SEG))]


def _proj_in_sample(x, g, w9, cn, ws, bst, qn, kn, mqn, *, layer, off):
    m = x.shape[0]
    blk = pl.BlockSpec((m, SEG), lambda i: (i, 0))
    dts = (F32, F32, F32, BF16, F32, BF16, F32, F32, BF16)
    return pl.pallas_call(
        functools.partial(_proj_kernel, prompt=False, tm=m, off=off),
        grid=(1,),
        in_specs=_proj_in_specs(layer, m),
        out_specs=[blk] * len(dts),
        out_shape=[jax.ShapeDtypeStruct((m, SEG), dt) for dt in dts],
        compiler_params=_cparams("parallel"),
        name="proj_in_sample",
    )(x, g, w9, cn, ws, bst, qn, kn, mqn)


def _proj_in_prompt(x, g, w9, cn, ws, bst, qn, kn, mqn, ckt_all, cv4_all, *, layer, seq):
    m = x.shape[0]
    tm = 512
    nq = seq // tm
    blk = pl.BlockSpec((tm, SEG), lambda i: (i, 0))
    cktblk = pl.BlockSpec((None, None, SEG, tm), lambda i: (layer, i // nq, 0, i % nq))
    cv4blk = pl.BlockSpec((None, tm * HEADS, HEAD_W), lambda i: (layer, i, 0))
    sds = lambda dt: jax.ShapeDtypeStruct((m, SEG), dt)
    in_specs = _proj_in_specs(layer, tm)
    n_in = len(in_specs)
    return pl.pallas_call(
        functools.partial(_proj_kernel, prompt=True, tm=tm, off=0),
        grid=(m // tm,),
        in_specs=in_specs + [pl.BlockSpec(memory_space=pl.ANY)] * 2,
        out_specs=[blk, blk, blk, blk, blk, cktblk, blk, cv4blk, blk, blk],
        out_shape=[sds(BF16), sds(F32), sds(F32), sds(BF16), sds(BF16),
                   jax.ShapeDtypeStruct(ckt_all.shape, F32), sds(BF16),
                   jax.ShapeDtypeStruct(cv4_all.shape, F32), sds(BF16), sds(BF16)],
        input_output_aliases={n_in: 5, n_in + 1: 7},
        compiler_params=_cparams("parallel"),
        name="proj_in_prompt",
    )(x, g, w9, cn, ws, bst, qn, kn, mqn, ckt_all, cv4_all)


RET_BATCH = 4


def _ret_log_decay(h):
    return math.log1p(-(2.0 ** (-5 - h)))


def _ret_finish(o, gate, gn):
    return gate * _sigmoid(gate) * _layer_norm(o, gn)


def _retention_kernel(rqk_ref, rv_ref, rg_ref, gn_ref, o_ref, st_o, st_ref, dec_ref, qd_ref, kd_ref):
    c = pl.program_id(1)

    @pl.when(c == 0)
    def _():
        st_ref[...] = jnp.zeros_like(st_ref)
        i = lax.broadcasted_iota(jnp.int32, (CHUNK, CHUNK), 0).astype(F32)
        j = lax.broadcasted_iota(jnp.int32, (CHUNK, CHUNK), 1).astype(F32)
        for h in range(HEADS):
            lg = _ret_log_decay(h)
            dec_ref[h] = jnp.where(i >= j, jnp.exp(lg * jnp.maximum(i - j, 0.0)), 0.0)
            qd_ref[h] = jnp.exp(lg * (i + 1.0))
            kd_ref[h] = jnp.exp(lg * (CHUNK - 1.0 - i))

    gn = gn_ref[...]
    lane = lax.broadcasted_iota(jnp.int32, (1, HEADS * R_DK), 1)
    for b in range(RET_BATCH):
        qk = rqk_ref[b]
        q_all = qk[:, :HEADS * R_DK]
        k_all = qk[:, HEADS * R_DK:]
        v_all = rv_ref[b]
        rg = rg_ref[b]
        st = st_ref[b]
        stb = st.astype(BF16)
        zero = jnp.zeros_like(q_all)
        vk_parts = []
        for h in range(HEADS):
            hs = slice(h * HEAD_W, (h + 1) * HEAD_W)
            v = v_all[:, hs]
            qm = jnp.where((lane >= h * R_DK) & (lane < (h + 1) * R_DK), q_all, zero)
            s = _dot_nt(qm, k_all) * dec_ref[h]
            intra = _dot(s.astype(BF16), v.astype(BF16))
            cross = _dot(qm, stb) * qd_ref[h]
            o_ref[b, :, hs] = _ret_finish(intra + cross, rg[:, hs], gn[:, hs]).astype(BF16)
            vk_parts.append((v * kd_ref[h]).astype(BF16))
        kv = _dot_tn(k_all, jnp.concatenate(vk_parts, axis=-1))
        for h in range(HEADS):
            rs = slice(h * R_DK, (h + 1) * R_DK)
            st_ref[b, rs, :] = (st[rs, :] * math.exp(_ret_log_decay(h) * CHUNK)
                                + kv[rs, h * HEAD_W:(h + 1) * HEAD_W])

    @pl.when(c == pl.num_programs(1) - 1)
    def _():
        st_o[...] = st_ref[...]


def _retention_prompt(layer, rqk, rv, rg, gn, batch, seq):
    nc = seq // CHUNK
    blk = pl.BlockSpec((RET_BATCH, CHUNK, SEG), lambda b, c: (b, c, 0))
    stblk = pl.BlockSpec((RET_BATCH, HEADS * R_DK, HEAD_W), lambda b, c: (b, 0, 0))
    tbl = pltpu.VMEM((HEADS, CHUNK, CHUNK), F32)
    r3 = lambda a: a.reshape(batch, seq, SEG)
    o_ret, st = pl.pallas_call(
        _retention_kernel,
        grid=(batch // RET_BATCH, nc),
        in_specs=[blk, blk, blk, _layer_spec(layer, (1, SEG))],
        out_specs=[blk, stblk],
        out_shape=[jax.ShapeDtypeStruct((batch, seq, SEG), BF16),
                   jax.ShapeDtypeStruct((batch, HEADS * R_DK, HEAD_W), F32)],
        scratch_shapes=[pltpu.VMEM((RET_BATCH, HEADS * R_DK, HEAD_W), F32), tbl, tbl, tbl],
        compiler_params=_cparams("parallel", "arbitrary"),
        name="retention_prompt",
    )(r3(rqk), r3(rv), r3(rg), gn)
    return o_ret.reshape(batch * seq, SEG), st


def _diff_finish(o0, o1, lam, hn, lam_init):
    o = o0 - lam * o1
    return _rms(o, hn) * (1.0 - lam_init)


def _alibi_slope_log2(h, shape):
    return jnp.exp2(jnp.full(shape, -8.0 / HEADS, F32) * (h + 1).astype(F32)) * LOG2E


def _diff_attn_kernel(lw_ref, q_ref, k_ref, v_ref, hn_ref, o_ref, vt_ref, acc_ref, s_ref, ml_ref,
                      *, tq, nblk, lam_init):
    h = pl.program_id(1)
    i = pl.program_id(2)

    @pl.when(i == 0)
    def _():
        for t in range(nblk):
            vt_ref[t] = v_ref[t * tq:(t + 1) * tq, :].astype(F32).T.astype(BF16)

    q = q_ref[...]
    lane = lax.broadcasted_iota(jnp.int32, (1, HEAD_W), 1)
    zero = jnp.zeros_like(q)
    qs = (jnp.where(lane < C_DQK, q, zero), jnp.where(lane < C_DQK, zero, q))
    krow = lax.broadcasted_iota(jnp.int32, (tq, HEAD_W), 0).astype(F32)
    brep = _alibi_slope_log2(h, (tq, HEAD_W)) * krow
    bias = jnp.concatenate([brep] * (tq // HEAD_W), axis=1)
    slope_row = _alibi_slope_log2(h, (1, tq))
    rowi = lax.broadcasted_iota(jnp.int32, (tq, tq), 0)
    coli = lax.broadcasted_iota(jnp.int32, (tq, tq), 1)
    acc_ref[...] = jnp.zeros_like(acc_ref)

    def scores(j, c, slot):
        s_ref[slot, c] = _dot_nt(k_ref[pl.ds(pl.multiple_of(j * tq, tq), tq), :], qs[c])

    def block(j, carry, slot, masked):
        vtb = vt_ref[j]
        off = slope_row * ((j - i) * tq).astype(F32)
        new = []
        for c in range(2):
            if not masked:
                scores(j + 1, c, 1 - slot)
            m_prev, l_prev = carry[2 * c], carry[2 * c + 1]
            s = s_ref[slot, c] + bias
            if masked:
                s = jnp.where(coli >= rowi, s, NEG_INF)
            m_new = jnp.maximum(m_prev, jnp.max(s, axis=0, keepdims=True) + off)
            alpha = jnp.exp2(m_prev - m_new)
            p = jnp.exp2(s - (m_new - off))
            l_new = alpha * l_prev + jnp.sum(p, axis=0, keepdims=True)
            acc_ref[c] = alpha * acc_ref[c] + _dot(vtb, p.astype(BF16))
            new += [m_new, l_new]
        return tuple(new)

    def pair(t, carry):
        return block(2 * t + 1, block(2 * t, carry, 0, False), 1, False)

    def finish(carry):
        lam = _diff_lambda(lw_ref[...], lam_init)
        o = acc_ref[0] / carry[1] - lam * (acc_ref[1] / carry[3])
        y = o * lax.rsqrt(jnp.mean(o * o, axis=0, keepdims=True) + EPS)
        o_ref[...] = (y.T * hn_ref[...] * (1.0 - lam_init)).astype(BF16)

    neg = jnp.full((1, tq), NEG_INF, F32)
    zer = jnp.zeros((1, tq), F32)
    scores(0, 0, 0)
    scores(0, 1, 0)
    carry = lax.fori_loop(0, lax.shift_right_logical(i, 1), pair, (neg, zer, neg, zer))
    for r in range(4):
        ml_ref[r:r + 1, :] = carry[r]
    stats = lambda: tuple(ml_ref[r:r + 1, :] for r in range(4))

    @pl.when(jnp.bitwise_and(i, 1) == 0)
    def _():
        finish(block(i, stats(), 0, True))

    @pl.when(jnp.bitwise_and(i, 1) == 1)
    def _():
        finish(block(i, block(i - 1, stats(), 0, False), 1, True))


def _diff_attn_prompt(layer, lw, cq, ckb, cvb, hn, batch, seq, lam_init):
    tq = 512
    nq = seq // tq
    qblk = pl.BlockSpec((tq, HEAD_W), lambda b, h, i: (b * nq + i, h))
    kvblk = pl.BlockSpec((seq, HEAD_W), lambda b, h, i: (b, h))
    return pl.pallas_call(
        functools.partial(_diff_attn_kernel, tq=tq, nblk=nq, lam_init=lam_init),
        grid=(batch, HEADS, nq),
        in_specs=[_layer_spec(layer, (4, C_DQK)), qblk, kvblk, kvblk,
                  pl.BlockSpec((None, 1, HEAD_W), lambda b, h, i: (layer, 0, h))],
        out_specs=qblk,
        out_shape=jax.ShapeDtypeStruct((batch * seq, SEG), BF16),
        scratch_shapes=[pltpu.VMEM((nq, HEAD_W, tq), BF16), pltpu.VMEM((2, HEAD_W, tq), F32),
                        pltpu.VMEM((2, 2, tq, tq), F32), pltpu.VMEM((4, tq), F32)],
        compiler_params=_cparams("parallel", "parallel", "arbitrary"),
        name="diff_attn_prompt",
    )(lw, cq, ckb, cvb, hn)


def _mem_kv_kernel(x_ref, g_ref, w_ref, kn_ref, k_all, v_all, k_o, v_o, *, tm):
    del k_all, v_all
    hb = _rms(x_ref[...], g_ref[...]).astype(BF16)
    k = _head_rms(_dot(hb, w_ref[:, :SEG]), kn_ref[...], M_DH)
    v = _dot(hb, w_ref[:, SEG:])
    for h in range(HEADS):
        hs = slice(h * HEAD_W, (h + 1) * HEAD_W)
        k_o[pl.ds(h, tm, stride=HEADS), :] = k[:, hs]
        v_o[pl.ds(h, tm, stride=HEADS), :] = v[:, hs]


def _mem_kv(mem, g, w2, kn, mk_all, mv_all, *, layer):
    m = mem.shape[0]
    tm = 512
    row = lambda i: (i, 0)
    blk = pl.BlockSpec((None, tm * HEADS, M_DH), lambda i: (layer, i, 0))
    return pl.pallas_call(
        functools.partial(_mem_kv_kernel, tm=tm),
        grid=(m // tm,),
        in_specs=[pl.BlockSpec((tm, D_MODEL), row), _layer_spec(layer, (1, D_MODEL)),
                  _layer_spec(layer, (D_MODEL, 2 * SEG)), _layer_spec(layer, (1, SEG)),
                  pl.BlockSpec(memory_space=pl.ANY), pl.BlockSpec(memory_space=pl.ANY)],
        out_specs=[blk, blk],
        out_shape=[jax.ShapeDtypeStruct(mk_all.shape, F32)] * 2,
        input_output_aliases={4: 0, 5: 1},
        compiler_params=_cparams("parallel"),
        name="mem_kv",
    )(mem, g, w2, kn, mk_all, mv_all)


def _mem_attend_head(qh, kh, vh):
    s = _dot_nt(qh, kh.astype(BF16))
    p = jnp.exp(s - jnp.max(s, axis=-1, keepdims=True))
    o = _dot(p.astype(BF16), vh.astype(BF16))
    return o / jnp.sum(p, axis=-1, keepdims=True)


def _mem_attn_kernel(q_ref, k_ref, v_ref, o_ref, *, n_mem):
    for h in range(HEADS):
        hs = slice(h * HEAD_W, (h + 1) * HEAD_W)
        kh = k_ref[pl.ds(h, n_mem, stride=HEADS), :]
        vh = v_ref[pl.ds(h, n_mem, stride=HEADS), :]
        o_ref[:, hs] = _mem_attend_head(q_ref[:, hs], kh, vh).astype(BF16)


def _mem_attn_prompt(layer, mq, mk_all, mv_all, batch, seq, n_mem):
    tq = 512
    nq = seq // tq
    qblk = pl.BlockSpec((tq, SEG), lambda i: (i, 0))
    kvblk = pl.BlockSpec((None, n_mem * HEADS, M_DH), lambda i: (layer, i // nq, 0))
    return pl.pallas_call(
        functools.partial(_mem_attn_kernel, n_mem=n_mem),
        grid=(batch * nq,),
        in_specs=[qblk, kvblk, kvblk],
        out_specs=qblk,
        out_shape=jax.ShapeDtypeStruct((batch * seq, SEG), BF16),
        compiler_params=_cparams("parallel"),
        name="mem_attn_prompt",
    )(mq, mk_all, mv_all)


GATE_SEGS = N_BRANCH * D_MODEL // SEG


def _merge_kernel(x_ref, g_ref, b0_ref, b1_ref, b2_ref, b3_ref, *rest):
    wg_refs = rest[:GATE_SEGS]
    wb_ref, wo_ref, o_ref = rest[GATE_SEGS:]
    per = GATE_SEGS // N_BRANCH
    x = x_ref[...]
    hb = _rms(x, g_ref[...]).astype(BF16)
    merged = None
    for n, b_ref in enumerate((b0_ref, b1_ref, b2_ref, b3_ref)):
        logits = jnp.concatenate([_dot(hb, wg_refs[n * per + k][...]) for k in range(per)], axis=1)
        gate = _sigmoid(logits)
        term = gate * _dot(b_ref[...], wb_ref[n])
        merged = term if merged is None else merged + term
    o_ref[...] = x + _dot(merged.astype(BF16), wo_ref[...])


def _merge(layer, x, g, branches, w_in, wb, wo):
    m = x.shape[0]
    tm = min(512, m)
    row = lambda i: (i, 0)
    xblk = pl.BlockSpec((tm, D_MODEL), row)
    bblk = pl.BlockSpec((tm, SEG), row)

    def gate_spec(k):
        return pl.BlockSpec((None, D_MODEL, SEG), lambda i: (layer, 0, N_SEG + k), pipeline_mode=pl.Buffered(1))

    return pl.pallas_call(
        _merge_kernel,
        grid=(m // tm,),
        in_specs=[xblk, _layer_spec(layer, (1, D_MODEL)), bblk, bblk, bblk, bblk]
                 + [gate_spec(k) for k in range(GATE_SEGS)]
                 + [_layer_spec(layer, (N_BRANCH, SEG, D_MODEL)), _layer_spec(layer, (D_MODEL, D_MODEL))],
        out_specs=xblk,
        out_shape=jax.ShapeDtypeStruct((m, D_MODEL), F32),
        compiler_params=_cparams("parallel"),
        name="merge",
    )(x, g, *branches, *([w_in] * GATE_SEGS), wb, wo)


def _ffn_kernel(x_ref, g_ref, wi_ref, wo_ref, o_ref):
    x = x_ref[...]
    hb = _rms(x, g_ref[...]).astype(BF16)
    acc = x
    for a, b in FFN_CHUNKS:
        gate = _dot(hb, wi_ref[:, a:b])
        up = _dot(hb, wi_ref[:, D_FF + a:D_FF + b])
        act = (gate * _sigmoid(gate) * up).astype(BF16)
        acc = acc + _dot(act, wo_ref[a:b, :])
    o_ref[...] = acc


def _ffn(layer, x, g, wi, wo):
    m = x.shape[0]
    tm = min(512, m)
    row = lambda i: (i, 0)
    xblk = pl.BlockSpec((tm, D_MODEL), row)
    return pl.pallas_call(
        _ffn_kernel,
        grid=(m // tm,),
        in_specs=[xblk, _layer_spec(layer, (1, D_MODEL)), _layer_spec(layer, (D_MODEL, 2 * D_FF)),
                  _layer_spec(layer, (D_FF, D_MODEL))],
        out_specs=xblk,
        out_shape=jax.ShapeDtypeStruct((m, D_MODEL), F32),
        compiler_params=_cparams("parallel"),
        name="ffn",
    )(x, g, wi, wo)


PAD_ROWS = 16
SAMPLE_BATCH = 4


def _sample_mix_kernel(rqk_ref, rv_ref, rg_ref, mq_ref, st_ref, mk_ref, mv_ref, gn_ref,
                       oret_o, omm_o, st_o, *, n_mem):
    gn = gn_ref[...]
    lane = lax.broadcasted_iota(jnp.int32, (1, HEADS * R_DK), 1)

    def first_row(a):
        r = lax.broadcasted_iota(jnp.int32, (PAD_ROWS, a.shape[1]), 0)
        return jnp.where(r == 0, jnp.broadcast_to(a, (PAD_ROWS, a.shape[1])), 0.0).astype(BF16)

    for b in range(SAMPLE_BATCH):
        qk = rqk_ref[b]
        q = qk[:, :HEADS * R_DK]
        k = qk[:, HEADS * R_DK:]
        v = rv_ref[b]
        rg = rg_ref[b]
        st = st_ref[b]
        stb = st.astype(BF16)
        kv = _dot_tn(first_row(k), first_row(v))
        for h in range(HEADS):
            hs = slice(h * HEAD_W, (h + 1) * HEAD_W)
            rs = slice(h * R_DK, (h + 1) * R_DK)
            gamma = math.exp(_ret_log_decay(h))
            qm = jnp.where((lane >= h * R_DK) & (lane < (h + 1) * R_DK), q, 0.0)
            score = jnp.sum(qm * k, axis=-1, keepdims=True)
            cross = _dot(jnp.broadcast_to(qm, (PAD_ROWS, HEADS * R_DK)).astype(BF16), stb)[0:1] * gamma
            o = score * v[:, hs] + cross
            oret_o[b, :, hs] = _ret_finish(o, rg[:, hs], gn[:, hs]).astype(BF16)
            st_o[b, rs, :] = st[rs, :] * gamma + kv[rs, hs]

        mq = mq_ref[b]
        for h in range(HEADS):
            hs = slice(h * HEAD_W, (h + 1) * HEAD_W)
            qh = jnp.broadcast_to(mq[:, hs], (PAD_ROWS, HEAD_W))
            kh = mk_ref[b, pl.ds(h, n_mem, stride=HEADS), :]
            vh = mv_ref[b, pl.ds(h, n_mem, stride=HEADS), :]
            omm_o[b, :, hs] = _mem_attend_head(qh, kh, vh)[0:1].astype(BF16)


def _sample_mix(layer, rqk, rv, rg, mq, state, mem_k, mem_v, gn, n_mem):
    db = rqk.shape[0]
    sb = SAMPLE_BATCH
    row = pl.BlockSpec((sb, 1, SEG), lambda b: (b, 0, 0))
    stblk = pl.BlockSpec((None, sb, HEADS * R_DK, HEAD_W), lambda b: (layer, b, 0, 0))
    memblk = pl.BlockSpec((None, sb, n_mem * HEADS, M_DH), lambda b: (layer, b, 0, 0))
    r3 = lambda a: a.reshape(db, 1, SEG)
    return pl.pallas_call(
        functools.partial(_sample_mix_kernel, n_mem=n_mem),
        grid=(db // sb,),
        in_specs=[row, row, row, row, stblk, memblk, memblk, _layer_spec(layer, (1, SEG))],
        out_specs=[row, row, pl.BlockSpec((sb, HEADS * R_DK, HEAD_W), lambda b: (b, 0, 0))],
        out_shape=[jax.ShapeDtypeStruct((db, 1, SEG), BF16), jax.ShapeDtypeStruct((db, 1, SEG), BF16),
                   jax.ShapeDtypeStruct((db, HEADS * R_DK, HEAD_W), F32)],
        compiler_params=_cparams("parallel"),
        name="sample_mix",
    )(r3(rqk), r3(rv), r3(rg), r3(mq), state, mem_k, mem_v, gn)


N_MAPS = 2 * HEADS
PAGES_PER_STEP = 16


def _paged_attn_kernel(pt_ref, lw_ref, q_ref, kn_ref, vn_ref, hn_ref, *rest, page, past_len, lam_init):
    del pt_ref
    npg = PAGES_PER_STEP
    k_refs = rest[:npg]
    v_refs = rest[npg:2 * npg]
    o_ref, m_ref, l_ref, acc_ref = rest[2 * npg:]
    s_idx = pl.program_id(1)

    @pl.when(s_idx == 0)
    def _():
        m_ref[...] = jnp.full_like(m_ref, NEG_INF)
        l_ref[...] = jnp.zeros_like(l_ref)
        acc_ref[...] = jnp.zeros_like(acc_ref)

    r8 = lax.broadcasted_iota(jnp.int32, (N_MAPS, SEG), 0)
    l8 = lax.broadcasted_iota(jnp.int32, (N_MAPS, SEG), 1)
    sel = (l8 >= r8 * C_DQK) & (l8 < (r8 + 1) * C_DQK)
    q8f = jnp.where(sel, jnp.broadcast_to(q_ref[...].astype(F32), (N_MAPS, SEG)), 0.0)
    q8 = q8f.astype(BF16)
    hrow = jnp.right_shift(lax.broadcasted_iota(jnp.int32, (N_MAPS, 1), 0), 1)
    hrow_w = jnp.right_shift(lax.broadcasted_iota(jnp.int32, (N_MAPS, HEAD_W), 0), 1)
    slope = jnp.exp2((-8.0 / HEADS) * (hrow + 1).astype(F32)) * LOG2E
    kpos = s_idx * (npg * page) + lax.broadcasted_iota(jnp.int32, (1, npg * page), 1)
    bias = slope * (kpos - past_len).astype(F32)
    kt = jnp.concatenate([k_refs[p][...].astype(BF16) for p in range(npg)], axis=1)
    s = _dot(q8, kt) + bias
    m_prev = m_ref[...]
    m_new = jnp.maximum(m_prev, jnp.max(s, axis=-1, keepdims=True))
    alpha = jnp.exp2(m_prev - m_new)
    pr = jnp.exp2(s - m_new)
    l_ref[...] = alpha * l_ref[...] + jnp.sum(pr, axis=-1, keepdims=True)
    acc = alpha * acc_ref[...]
    prb = pr.astype(BF16)
    for h in range(HEADS):
        vh = jnp.concatenate([v_refs[p][pl.ds(h, page, stride=HEADS), :].astype(BF16) for p in range(npg)], axis=0)
        acc = acc + jnp.where(hrow_w == h, _dot(prb, vh), 0.0)
    acc_ref[...] = acc
    m_ref[...] = m_new

    @pl.when(s_idx == pl.num_programs(1) - 1)
    def _():
        s_new = jnp.sum(q8f * kn_ref[...], axis=-1, keepdims=True)
        m_fin = jnp.maximum(m_ref[...], s_new)
        a = jnp.exp2(m_ref[...] - m_fin)
        p_new = jnp.exp2(s_new - m_fin)
        l_fin = a * l_ref[...] + p_new
        vn = vn_ref[...]
        vn8 = jnp.zeros((N_MAPS, HEAD_W), F32)
        for h in range(HEADS):
            vn8 = jnp.where(hrow_w == h, jnp.broadcast_to(vn[:, h * HEAD_W:(h + 1) * HEAD_W], (N_MAPS, HEAD_W)), vn8)
        o_all = (a * acc_ref[...] + p_new * vn8) / l_fin
        lam = _diff_lambda(lw_ref[...], lam_init)
        hn = hn_ref[...]
        for h in range(HEADS):
            hs = slice(h * HEAD_W, (h + 1) * HEAD_W)
            o = _diff_finish(o_all[2 * h:2 * h + 1], o_all[2 * h + 1:2 * h + 2], lam, hn[:, hs], lam_init)
            o_ref[:, hs] = o.astype(BF16)


def _paged_attn(layer, page_table, lw, cq, ck, cv, hn, cache_kt, cache_v, lam_init):
    db, n_pages = page_table.shape
    page = cache_kt.shape[3]
    npg = PAGES_PER_STEP
    row = pl.BlockSpec((None, 1, SEG), lambda b, s, pt: (b, 0, 0))

    def page_spec(p):
        return pl.BlockSpec((None, None, SEG, page),
                            lambda b, s, pt: (layer, pt[b * n_pages + s * npg + p], 0, 0))

    r3 = lambda a: a.reshape(db, 1, SEG)
    grid_spec = pltpu.PrefetchScalarGridSpec(
        num_scalar_prefetch=1,
        grid=(db, n_pages // npg),
        in_specs=[pl.BlockSpec((None, 4, C_DQK), lambda b, s, pt: (layer, 0, 0)), row, row, row,
                  pl.BlockSpec((None, 1, SEG), lambda b, s, pt: (layer, 0, 0))]
                 + [page_spec(p) for p in range(npg)] * 2,
        out_specs=row,
        scratch_shapes=[pltpu.VMEM((N_MAPS, 1), F32), pltpu.VMEM((N_MAPS, 1), F32),
                        pltpu.VMEM((N_MAPS, HEAD_W), F32)],
    )
    return pl.pallas_call(
        functools.partial(_paged_attn_kernel, page=page, past_len=n_pages * page, lam_init=lam_init),
        grid_spec=grid_spec,
        out_shape=jax.ShapeDtypeStruct((db, 1, SEG), BF16),
        compiler_params=_cparams("parallel", "arbitrary"),
        name="paged_diff_attn",
    )(page_table.reshape(-1), lw, r3(cq), r3(ck), r3(cv), hn,
      *([cache_kt] * npg), *([cache_v] * npg))


def kernel(x_prompt, x_sample, mem_prompt, cache_diff_k, cache_diff_v, page_table, cache_mem_k, cache_mem_v, state_ret, norm_mix, w_in, ret_norm, cmlp_norm, cmlp_ws, cmlp_bs, diff_qn, diff_kn, diff_lambda_w, diff_hn, mem_norm, w_mem_kv, mem_qn, mem_kn, w_branch, w_out, norm_ffn, w_ffn_in, w_ffn_out):
    batch, seq, _ = x_prompt.shape
    db, dec_seq, _ = x_sample.shape
    assert dec_seq == 1, "the sample group decodes one token per sequence"
    depth = w_in.shape[0]
    n_mem = mem_prompt.shape[1]
    n_phys, page = cache_diff_k.shape[1:3]
    past_len = page_table.shape[1] * page
    n_main = N_SEG * SEG

    w9 = wgl = w_in.astype(BF16)
    wmem = w_mem_kv.astype(BF16)
    wb = w_branch.astype(BF16)
    wo = w_out.astype(BF16)
    wfi = w_ffn_in.astype(BF16)
    wfo = w_ffn_out.astype(BF16)
    bst = cmlp_bs.transpose(0, 2, 1)
    tile = lambda a, n: jnp.tile(a, (1, n)).reshape(depth, 1, SEG)
    qn, kn = tile(diff_qn, SEG // C_DQK), tile(diff_kn, SEG // C_DQK)
    mqn, mkn = tile(mem_qn, HEADS), tile(mem_kn, HEADS)
    row = lambda a: a.reshape(depth, 1, -1)
    g_mix, g_ffn, g_mem = row(norm_mix), row(norm_ffn), row(mem_norm)
    g_ret, g_cm, g_hn = row(ret_norm), row(cmlp_norm), row(diff_hn)

    cache_kt = cache_diff_k.transpose(0, 1, 3, 4, 5, 2).reshape(depth, n_phys, SEG, page)
    cache_v = cache_diff_v.reshape(depth, n_phys, page * HEADS, HEAD_W)
    mem_k_s = cache_mem_k.reshape(depth, db, n_mem * HEADS, M_DH)
    mem_v_s = cache_mem_v.reshape(depth, db, n_mem * HEADS, M_DH)
    state_s = state_ret.reshape(depth, db, HEADS * R_DK, HEAD_W)

    xp = x_prompt.reshape(batch * seq, D_MODEL)
    xs = x_sample.reshape(db, D_MODEL)
    mem = mem_prompt.reshape(batch * n_mem, D_MODEL)

    ckt_all = jnp.zeros((depth, batch, SEG, seq), F32)
    cv4_all = jnp.zeros((depth, batch * seq * HEADS, HEAD_W), F32)
    mk_all = jnp.zeros((depth, batch * n_mem * HEADS, M_DH), F32)
    mv_all = jnp.zeros((depth, batch * n_mem * HEADS, M_DH), F32)

    pst = []
    sk, sv, sst, scv = [], [], [], []
    for l in range(depth):
        lam_init = 0.8 - 0.6 * math.exp(-0.3 * l)
        lw = diff_lambda_w

        rqk, rv, rg, o_cm, cq, ckt_all, ckb, cv4_all, cvb, mq = _proj_in_prompt(
            xp, g_mix, w9, g_cm, cmlp_ws, bst, qn, kn, mqn, ckt_all, cv4_all, layer=l, seq=seq)
        o_ret, st_p = _retention_prompt(l, rqk, rv, rg, g_ret, batch, seq)
        o_df = _diff_attn_prompt(l, lw, cq, ckb, cvb, g_hn, batch, seq, lam_init)
        mk_all, mv_all = _mem_kv(mem, g_mem, wmem, mkn, mk_all, mv_all, layer=l)
        o_mm = _mem_attn_prompt(l, mq, mk_all, mv_all, batch, seq, n_mem)
        xp = _merge(l, xp, g_mix, (o_ret, o_cm, o_df, o_mm), wgl, wb, wo)
        xp = _ffn(l, xp, g_ffn, wfi, wfo)
        pst.append(st_p.reshape(batch, HEADS, R_DK, HEAD_W))

        rqk, rv, rg, o_cm, vn_s, cq, ck, cv, mq = _proj_in_sample(
            xs, g_mix, w9, g_cm, cmlp_ws, bst, qn, kn, mqn, layer=l, off=past_len % CHUNK)
        o_ret, o_mm, st_s = _sample_mix(l, rqk, rv, rg, mq, state_s, mem_k_s, mem_v_s, g_ret, n_mem)
        o_df = _paged_attn(l, page_table, lw, cq, ck, cv, g_hn, cache_kt, cache_v, lam_init)
        xs = _merge(l, xs, g_mix, (o_ret.reshape(db, SEG), o_cm, o_df.reshape(db, SEG), o_mm.reshape(db, SEG)),
                    wgl, wb, wo)
        xs = _ffn(l, xs, g_ffn, wfi, wfo)
        sk.append(ck.reshape(db, 1, HEADS, 2, C_DQK))
        sv.append(cv.reshape(db, 1, HEADS, HEAD_W))
        sst.append(st_s.reshape(db, HEADS, R_DK, HEAD_W))
        scv.append(vn_s.reshape(db, 1, SEG))

    new_k = ckt_all.reshape(depth, batch, HEADS, 2, C_DQK, seq).transpose(0, 1, 5, 2, 3, 4)
    new_v = cv4_all.reshape(depth, batch, seq, HEADS, HEAD_W)
    new_mk = mk_all.reshape(depth, batch, n_mem, HEADS, M_DH)
    new_mv = mv_all.reshape(depth, batch, n_mem, HEADS, M_DH)
    return (xp.reshape(batch, seq, D_MODEL), xs.reshape(db, 1, D_MODEL),
            new_k, new_v, new_mk, new_mv, jnp.stack(pst),
            jnp.stack(sk), jnp.stack(sv), jnp.stack(sst), jnp.stack(scv))
```

```python
import functools
import math

import jax
import jax.numpy as jnp
from jax import lax
from jax.experimental import pallas as pl
from jax.experimental.pallas import tpu as pltpu

F32 = jnp.float32
BF16 = jnp.bfloat16

EPS = 1e-6
NEG_INF = -1e30
LOG2E = math.log2(math.e)

D_MODEL = 1024
SEG = 512
N_SEG = 9
HEADS = 4
R_DK = 64
HEAD_W = 128
CHUNK = 128
C_DQK = 64
M_DH = 128
N_BRANCH = 4
D_FF = 2816
FFN_CHUNKS = ((0, 1024), (1024, 2048), (2048, 2816))

VMEM_LIMIT_BYTES = 56 * 1024 * 1024


def _cparams(*sem):
    return pltpu.CompilerParams(dimension_semantics=sem, vmem_limit_bytes=VMEM_LIMIT_BYTES)


def _dot(a, b):
    return jnp.dot(a, b, preferred_element_type=F32)


def _dot_nt(a, b):
    return lax.dot_general(a, b, (((1,), (1,)), ((), ())), preferred_element_type=F32)


def _dot_tn(a, b):
    return lax.dot_general(a, b, (((0,), (0,)), ((), ())), preferred_element_type=F32)


def _sigmoid(x):
    return 1.0 / (1.0 + jnp.exp(-x))


def _rms(x, g):
    return x * lax.rsqrt(jnp.mean(x * x, axis=-1, keepdims=True) + EPS) * g


def _layer_norm(x, g):
    xc = x - jnp.mean(x, axis=-1, keepdims=True)
    return xc * lax.rsqrt(jnp.mean(xc * xc, axis=-1, keepdims=True) + EPS) * g


def _head_rms(y, g, group):
    lane = lax.broadcasted_iota(jnp.int32, (1, HEAD_W), 1)
    outs = []
    for hb in range(SEG // HEAD_W):
        blk = y[:, hb * HEAD_W:(hb + 1) * HEAD_W]
        sq = blk * blk
        if group == HEAD_W:
            ms = jnp.mean(sq, axis=-1, keepdims=True)
        else:
            lo = jnp.sum(jnp.where(lane < group, sq, 0.0), axis=-1, keepdims=True)
            hi = jnp.sum(jnp.where(lane < group, 0.0, sq), axis=-1, keepdims=True)
            ms = jnp.where(lane < group, lo, hi) * (1.0 / group)
        outs.append(blk * lax.rsqrt(ms + EPS))
    return jnp.concatenate(outs, axis=-1) * g


def _diff_lambda(wl, lam_init):
    a = jnp.sum(wl[0:1] * wl[1:2], axis=-1, keepdims=True)
    b = jnp.sum(wl[2:3] * wl[3:4], axis=-1, keepdims=True)
    return jnp.exp(a) - jnp.exp(b) + lam_init


def _layer_spec(layer, shape):
    nd = len(shape)
    return pl.BlockSpec((None,) + tuple(shape), lambda *_: (layer,) + (0,) * nd,
                        pipeline_mode=pl.Buffered(1))


def _proj_kernel(x_ref, g_ref, w_ref, cn_ref, ws_ref, bst_ref, qn_ref, kn_ref, mqn_ref, *rest,
                 prompt, tm, off):
    if prompt:
        rqk_o, rv_o, rg_o, ocm_o, cq_o, ckt_o, ckb_o, cv4_o, cvb_o, mq_o = rest[2:]
    else:
        rqk_o, rv_o, rg_o, ocm_o, vn_o, cq_o, ck_o, cv_o, mq_o = rest
    hb = _rms(x_ref[...], g_ref[...]).astype(BF16)

    def seg(s):
        return _dot(hb, w_ref[:, s * SEG:(s + 1) * SEG])

    lane = lax.broadcasted_iota(jnp.int32, (1, SEG), 1)
    y = seg(0) * jnp.where(lane < HEADS * R_DK, 1.0, R_DK ** -0.5)
    rqk_o[...] = y.astype(rqk_o.dtype)
    rv_o[...] = seg(1)
    rg_o[...] = seg(2)

    gu = seg(3)
    vn = _layer_norm(seg(4), cn_ref[...])
    if prompt:
        vnb = vn.astype(BF16)
        row = lax.broadcasted_iota(jnp.int32, (CHUNK, CHUNK), 0)
        col = lax.broadcasted_iota(jnp.int32, (CHUNK, CHUNK), 1)
        for g in range(HEADS):
            gs = slice(g * HEAD_W, (g + 1) * HEAD_W)
            wt = jnp.where(row >= col, ws_ref[g], 0.0).astype(BF16)
            bcol = bst_ref[:, g:g + 1]
            for c in range(tm // CHUNK):
                cs = slice(c * CHUNK, (c + 1) * CHUNK)
                mix = _dot(wt, vnb[cs, gs]) + bcol
                ocm_o[cs, gs] = (gu[cs, gs] * mix).astype(BF16)
    else:
        vn_o[...] = vn
        for g in range(HEADS):
            gs = slice(g * HEAD_W, (g + 1) * HEAD_W)
            w00 = ws_ref[g][off:off + 1, off:off + 1]
            b0 = bst_ref[off:off + 1, g:g + 1]
            ocm_o[:, gs] = (gu[:, gs] * (w00 * vn[:, gs] + b0)).astype(BF16)

    cq_o[...] = (_head_rms(seg(5), qn_ref[...], C_DQK) * (C_DQK ** -0.5 * LOG2E)).astype(BF16)
    ck = _head_rms(seg(6), kn_ref[...], C_DQK)
    cv = seg(7)
    if prompt:
        ckt_o[...] = ck.T
        for h in range(HEADS):
            cv4_o[pl.ds(h, tm, stride=HEADS), :] = cv[:, h * HEAD_W:(h + 1) * HEAD_W]
        ckb_o[...] = ck.astype(BF16)
        cvb_o[...] = cv.astype(BF16)
    else:
        ck_o[...] = ck
        cv_o[...] = cv
    mq_o[...] = (_head_rms(seg(8), mqn_ref[...], M_DH) * (M_DH ** -0.5)).astype(BF16)


def _proj_in_specs(layer, tm):
    row = lambda i: (i, 0)
    return [pl.BlockSpec((tm, D_MODEL), row), _layer_spec(layer, (1, D_MODEL)),
            _layer_spec(layer, (D_MODEL, N_SEG * SEG)), _layer_spec(layer, (1, SEG)),
            _layer_spec(layer, (HEADS, CHUNK, CHUNK)), _layer_spec(layer, (CHUNK, HEADS)),
            _layer_spec(layer, (1, SEG)), _layer_spec(layer, (1, SEG)), _layer_spec(layer, (1, SEG))]


def _proj_in_sample(x, g, w9, cn, ws, bst, qn, kn, mqn, *, layer, off):
    m = x.shape[0]
    blk = pl.BlockSpec((m, SEG), lambda i: (i, 0))
    dts = (F32, F32, F32, BF16, F32, BF16, F32, F32, BF16)
    return pl.pallas_call(
        functools.partial(_proj_kernel, prompt=False, tm=m, off=off),
        grid=(1,),
        in_specs=_proj_in_specs(layer, m),
        out_specs=[blk] * len(dts),
        out_shape=[jax.ShapeDtypeStruct((m, SEG), dt) for dt in dts],
        compiler_params=_cparams("parallel"),
        name="proj_in_sample",
    )(x, g, w9, cn, ws, bst, qn, kn, mqn)


def _proj_in_prompt(x, g, w9, cn, ws, bst, qn, kn, mqn, ckt_all, cv4_all, *, layer, seq):
    m = x.shape[0]
    tm = 512
    nq = seq // tm
    blk = pl.BlockSpec((tm, SEG), lambda i: (i, 0))
    cktblk = pl.BlockSpec((None, None, SEG, tm), lambda i: (layer, i // nq, 0, i % nq))
    cv4blk = pl.BlockSpec((None, tm * HEADS, HEAD_W), lambda i: (layer, i, 0))
    sds = lambda dt: jax.ShapeDtypeStruct((m, SEG), dt)
    in_specs = _proj_in_specs(layer, tm)
    n_in = len(in_specs)
    return pl.pallas_call(
        functools.partial(_proj_kernel, prompt=True, tm=tm, off=0),
        grid=(m // tm,),
        in_specs=in_specs + [pl.BlockSpec(memory_space=pl.ANY)] * 2,
        out_specs=[blk, blk, blk, blk, blk, cktblk, blk, cv4blk, blk, blk],
        out_shape=[sds(BF16), sds(F32), sds(F32), sds(BF16), sds(BF16),
                   jax.ShapeDtypeStruct(ckt_all.shape, F32), sds(BF16),
                   jax.ShapeDtypeStruct(cv4_all.shape, F32), sds(BF16), sds(BF16)],
        input_output_aliases={n_in: 5, n_in + 1: 7},
        compiler_params=_cparams("parallel"),
        name="proj_in_prompt",
    )(x, g, w9, cn, ws, bst, qn, kn, mqn, ckt_all, cv4_all)


RET_BATCH = 4


def _ret_log_decay(h):
    return math.log1p(-(2.0 ** (-5 - h)))


def _ret_finish(o, gate, gn):
    return gate * _sigmoid(gate) * _layer_norm(o, gn)


def _retention_kernel(rqk_ref, rv_ref, rg_ref, gn_ref, o_ref, st_o, st_ref, dec_ref, qd_ref, kd_ref):
    c = pl.program_id(1)

    @pl.when(c == 0)
    def _():
        st_ref[...] = jnp.zeros_like(st_ref)
        i = lax.broadcasted_iota(jnp.int32, (CHUNK, CHUNK), 0).astype(F32)
        j = lax.broadcasted_iota(jnp.int32, (CHUNK, CHUNK), 1).astype(F32)
        for h in range(HEADS):
            lg = _ret_log_decay(h)
            dec_ref[h] = jnp.where(i >= j, jnp.exp(lg * jnp.maximum(i - j, 0.0)), 0.0)
            qd_ref[h] = jnp.exp(lg * (i + 1.0))
            kd_ref[h] = jnp.exp(lg * (CHUNK - 1.0 - i))

    gn = gn_ref[...]
    lane = lax.broadcasted_iota(jnp.int32, (1, HEADS * R_DK), 1)
    for b in range(RET_BATCH):
        qk = rqk_ref[b]
        q_all = qk[:, :HEADS * R_DK]
        k_all = qk[:, HEADS * R_DK:]
        v_all = rv_ref[b]
        rg = rg_ref[b]
        st = st_ref[b]
        stb = st.astype(BF16)
        zero = jnp.zeros_like(q_all)
        vk_parts = []
        for h in range(HEADS):
            hs = slice(h * HEAD_W, (h + 1) * HEAD_W)
            v = v_all[:, hs]
            qm = jnp.where((lane >= h * R_DK) & (lane < (h + 1) * R_DK), q_all, zero)
            s = _dot_nt(qm, k_all) * dec_ref[h]
            intra = _dot(s.astype(BF16), v.astype(BF16))
            cross = _dot(qm, stb) * qd_ref[h]
            o_ref[b, :, hs] = _ret_finish(intra + cross, rg[:, hs], gn[:, hs]).astype(BF16)
            vk_parts.append((v * kd_ref[h]).astype(BF16))
        kv = _dot_tn(k_all, jnp.concatenate(vk_parts, axis=-1))
        for h in range(HEADS):
            rs = slice(h * R_DK, (h + 1) * R_DK)
            st_ref[b, rs, :] = (st[rs, :] * math.exp(_ret_log_decay(h) * CHUNK)
                                + kv[rs, h * HEAD_W:(h + 1) * HEAD_W])

    @pl.when(c == pl.num_programs(1) - 1)
    def _():
        st_o[...] = st_ref[...]


def _retention_prompt(layer, rqk, rv, rg, gn, batch, seq):
    nc = seq // CHUNK
    blk = pl.BlockSpec((RET_BATCH, CHUNK, SEG), lambda b, c: (b, c, 0))
    stblk = pl.BlockSpec((RET_BATCH, HEADS * R_DK, HEAD_W), lambda b, c: (b, 0, 0))
    tbl = pltpu.VMEM((HEADS, CHUNK, CHUNK), F32)
    r3 = lambda a: a.reshape(batch, seq, SEG)
    o_ret, st = pl.pallas_call(
        _retention_kernel,
        grid=(batch // RET_BATCH, nc),
        in_specs=[blk, blk, blk, _layer_spec(layer, (1, SEG))],
        out_specs=[blk, stblk],
        out_shape=[jax.ShapeDtypeStruct((batch, seq, SEG), BF16),
                   jax.ShapeDtypeStruct((batch, HEADS * R_DK, HEAD_W), F32)],
        scratch_shapes=[pltpu.VMEM((RET_BATCH, HEADS * R_DK, HEAD_W), F32), tbl, tbl, tbl],
        compiler_params=_cparams("parallel", "arbitrary"),
        name="retention_prompt",
    )(r3(rqk), r3(rv), r3(rg), gn)
    return o_ret.reshape(batch * seq, SEG), st


def _diff_finish(o0, o1, lam, hn, lam_init):
    o = o0 - lam * o1
    return _rms(o, hn) * (1.0 - lam_init)


def _alibi_slope_log2(h, shape):
    return jnp.exp2(jnp.full(shape, -8.0 / HEADS, F32) * (h + 1).astype(F32)) * LOG2E


ONES_ROWS = 16


def _diff_attn_kernel(lw_ref, q_ref, k_ref, v_ref, hn_ref, o_ref, vt_ref, acc_ref, s_ref, *, tq, nblk, lam_init):
    h = pl.program_id(1)
    ones = jnp.ones((ONES_ROWS, tq), BF16)
    for t in range(nblk):
        vt_ref[t, :HEAD_W, :] = v_ref[t * tq:(t + 1) * tq, :].astype(F32).T.astype(BF16)
        vt_ref[t, HEAD_W:, :] = ones

    lane = lax.broadcasted_iota(jnp.int32, (1, HEAD_W), 1)
    krow = lax.broadcasted_iota(jnp.int32, (tq, HEAD_W), 0).astype(F32)
    brep = _alibi_slope_log2(h, (tq, HEAD_W)) * krow
    bias = jnp.concatenate([brep] * (tq // HEAD_W), axis=1)
    slope_row = _alibi_slope_log2(h, (1, tq))
    rowi = lax.broadcasted_iota(jnp.int32, (tq, tq), 0)
    coli = lax.broadcasted_iota(jnp.int32, (tq, tq), 1)
    lam = _diff_lambda(lw_ref[...], lam_init)
    hn = hn_ref[...]

    q_maps = {}

    def maps_of(i):
        if i not in q_maps:
            q = q_ref[i * tq:(i + 1) * tq, :]
            zero = jnp.zeros_like(q)
            q_maps[i] = (jnp.where(lane < C_DQK, q, zero), jnp.where(lane < C_DQK, zero, q))
        return q_maps[i]

    def scores(i, j, c, slot):
        s_ref[slot, c] = bias + _dot_nt(k_ref[j * tq:(j + 1) * tq, :], maps_of(i)[c])

    schedule = [(i, j) for i in range(nblk) for j in range(i + 1)]
    scores(0, 0, 0, 0)
    scores(0, 0, 1, 0)
    m = [None, None]
    for n, (i, j) in enumerate(schedule):
        slot = n % 2
        nxt = schedule[n + 1] if n + 1 < len(schedule) else None
        off = slope_row * float((j - i) * tq)
        for c in range(2):
            if nxt is not None:
                scores(nxt[0], nxt[1], c, 1 - slot)
            s = s_ref[slot, c]
            if j == i:
                s = jnp.where(coli >= rowi, s, NEG_INF)
            blk_max = jnp.max(s, axis=0, keepdims=True) + off
            if j == 0:
                m_new = blk_max
                acc_ref[c] = _dot(vt_ref[j], jnp.exp2(s - (m_new - off)).astype(BF16))
            else:
                m_new = jnp.maximum(m[c], blk_max)
                p = jnp.exp2(s - (m_new - off)).astype(BF16)
                acc_ref[c] = jnp.exp2(m[c] - m_new) * acc_ref[c] + _dot(vt_ref[j], p)
            m[c] = m_new
        if j == i:
            o0 = acc_ref[0, :HEAD_W, :] / acc_ref[0, HEAD_W:HEAD_W + 1, :]
            o1 = acc_ref[1, :HEAD_W, :] / acc_ref[1, HEAD_W:HEAD_W + 1, :]
            o = o0 - lam * o1
            y = o * lax.rsqrt(jnp.mean(o * o, axis=0, keepdims=True) + EPS)
            o_ref[i * tq:(i + 1) * tq, :] = (y.T * hn * (1.0 - lam_init)).astype(BF16)


def _diff_attn_prompt(layer, lw, cq, ckb, cvb, hn, batch, seq, lam_init):
    tq = 512
    nq = seq // tq
    blk = pl.BlockSpec((seq, HEAD_W), lambda b, h: (b, h))
    return pl.pallas_call(
        functools.partial(_diff_attn_kernel, tq=tq, nblk=nq, lam_init=lam_init),
        grid=(batch, HEADS),
        in_specs=[_layer_spec(layer, (4, C_DQK)), blk, blk, blk,
                  pl.BlockSpec((None, 1, HEAD_W), lambda b, h: (layer, 0, h))],
        out_specs=blk,
        out_shape=jax.ShapeDtypeStruct((batch * seq, SEG), BF16),
        scratch_shapes=[pltpu.VMEM((nq, HEAD_W + ONES_ROWS, tq), BF16),
                        pltpu.VMEM((2, HEAD_W + ONES_ROWS, tq), F32),
                        pltpu.VMEM((2, 2, tq, tq), F32)],
        compiler_params=_cparams("parallel", "parallel"),
        name="diff_attn_prompt",
    )(lw, cq, ckb, cvb, hn)


def _mem_kv_kernel(x_ref, g_ref, w_ref, kn_ref, k_all, v_all, k_o, v_o, *, tm):
    del k_all, v_all
    hb = _rms(x_ref[...], g_ref[...]).astype(BF16)
    k = _head_rms(_dot(hb, w_ref[:, :SEG]), kn_ref[...], M_DH)
    v = _dot(hb, w_ref[:, SEG:])
    for h in range(HEADS):
        hs = slice(h * HEAD_W, (h + 1) * HEAD_W)
        k_o[pl.ds(h, tm, stride=HEADS), :] = k[:, hs]
        v_o[pl.ds(h, tm, stride=HEADS), :] = v[:, hs]


def _mem_kv(mem, g, w2, kn, mk_all, mv_all, *, layer):
    m = mem.shape[0]
    tm = 512
    row = lambda i: (i, 0)
    blk = pl.BlockSpec((None, tm * HEADS, M_DH), lambda i: (layer, i, 0))
    return pl.pallas_call(
        functools.partial(_mem_kv_kernel, tm=tm),
        grid=(m // tm,),
        in_specs=[pl.BlockSpec((tm, D_MODEL), row), _layer_spec(layer, (1, D_MODEL)),
                  _layer_spec(layer, (D_MODEL, 2 * SEG)), _layer_spec(layer, (1, SEG)),
                  pl.BlockSpec(memory_space=pl.ANY), pl.BlockSpec(memory_space=pl.ANY)],
        out_specs=[blk, blk],
        out_shape=[jax.ShapeDtypeStruct(mk_all.shape, F32)] * 2,
        input_output_aliases={4: 0, 5: 1},
        compiler_params=_cparams("parallel"),
        name="mem_kv",
    )(mem, g, w2, kn, mk_all, mv_all)


def _mem_attend_head(qh, kh, vh):
    s = _dot_nt(qh, kh.astype(BF16))
    p = jnp.exp(s - jnp.max(s, axis=-1, keepdims=True))
    o = _dot(p.astype(BF16), vh.astype(BF16))
    return o / jnp.sum(p, axis=-1, keepdims=True)


def _mem_attn_kernel(q_ref, k_ref, v_ref, o_ref, *, n_mem):
    for h in range(HEADS):
        hs = slice(h * HEAD_W, (h + 1) * HEAD_W)
        kh = k_ref[pl.ds(h, n_mem, stride=HEADS), :]
        vh = v_ref[pl.ds(h, n_mem, stride=HEADS), :]
        o_ref[:, hs] = _mem_attend_head(q_ref[:, hs], kh, vh).astype(BF16)


def _mem_attn_prompt(layer, mq, mk_all, mv_all, batch, seq, n_mem):
    tq = 512
    nq = seq // tq
    qblk = pl.BlockSpec((tq, SEG), lambda i: (i, 0))
    kvblk = pl.BlockSpec((None, n_mem * HEADS, M_DH), lambda i: (layer, i // nq, 0))
    return pl.pallas_call(
        functools.partial(_mem_attn_kernel, n_mem=n_mem),
        grid=(batch * nq,),
        in_specs=[qblk, kvblk, kvblk],
        out_specs=qblk,
        out_shape=jax.ShapeDtypeStruct((batch * seq, SEG), BF16),
        compiler_params=_cparams("parallel"),
        name="mem_attn_prompt",
    )(mq, mk_all, mv_all)


GATE_SEGS = N_BRANCH * D_MODEL // SEG


def _merge_kernel(x_ref, g_ref, b0_ref, b1_ref, b2_ref, b3_ref, *rest):
    wg_refs = rest[:GATE_SEGS]
    wb_ref, wo_ref, o_ref = rest[GATE_SEGS:]
    per = GATE_SEGS // N_BRANCH
    x = x_ref[...]
    hb = _rms(x, g_ref[...]).astype(BF16)
    merged = None
    for n, b_ref in enumerate((b0_ref, b1_ref, b2_ref, b3_ref)):
        logits = jnp.concatenate([_dot(hb, wg_refs[n * per + k][...]) for k in range(per)], axis=1)
        gate = _sigmoid(logits)
        term = gate * _dot(b_ref[...], wb_ref[n])
        merged = term if merged is None else merged + term
    o_ref[...] = x + _dot(merged.astype(BF16), wo_ref[...])


def _merge(layer, x, g, branches, w_in, wb, wo):
    m = x.shape[0]
    tm = min(512, m)
    row = lambda i: (i, 0)
    xblk = pl.BlockSpec((tm, D_MODEL), row)
    bblk = pl.BlockSpec((tm, SEG), row)

    def gate_spec(k):
        return pl.BlockSpec((None, D_MODEL, SEG), lambda i: (layer, 0, N_SEG + k), pipeline_mode=pl.Buffered(1))

    return pl.pallas_call(
        _merge_kernel,
        grid=(m // tm,),
        in_specs=[xblk, _layer_spec(layer, (1, D_MODEL)), bblk, bblk, bblk, bblk]
                 + [gate_spec(k) for k in range(GATE_SEGS)]
                 + [_layer_spec(layer, (N_BRANCH, SEG, D_MODEL)), _layer_spec(layer, (D_MODEL, D_MODEL))],
        out_specs=xblk,
        out_shape=jax.ShapeDtypeStruct((m, D_MODEL), F32),
        compiler_params=_cparams("parallel"),
        name="merge",
    )(x, g, *branches, *([w_in] * GATE_SEGS), wb, wo)


def _ffn_kernel(x_ref, g_ref, wi_ref, wo_ref, o_ref):
    x = x_ref[...]
    hb = _rms(x, g_ref[...]).astype(BF16)
    acc = x
    for a, b in FFN_CHUNKS:
        gate = _dot(hb, wi_ref[:, a:b])
        up = _dot(hb, wi_ref[:, D_FF + a:D_FF + b])
        act = (gate * _sigmoid(gate) * up).astype(BF16)
        acc = acc + _dot(act, wo_ref[a:b, :])
    o_ref[...] = acc


def _ffn(layer, x, g, wi, wo):
    m = x.shape[0]
    tm = min(512, m)
    row = lambda i: (i, 0)
    xblk = pl.BlockSpec((tm, D_MODEL), row)
    return pl.pallas_call(
        _ffn_kernel,
        grid=(m // tm,),
        in_specs=[xblk, _layer_spec(layer, (1, D_MODEL)), _layer_spec(layer, (D_MODEL, 2 * D_FF)),
                  _layer_spec(layer, (D_FF, D_MODEL))],
        out_specs=xblk,
        out_shape=jax.ShapeDtypeStruct((m, D_MODEL), F32),
        compiler_params=_cparams("parallel"),
        name="ffn",
    )(x, g, wi, wo)


PAD_ROWS = 16
SAMPLE_BATCH = 4


def _sample_mix_kernel(rqk_ref, rv_ref, rg_ref, mq_ref, st_ref, mk_ref, mv_ref, gn_ref,
                       oret_o, omm_o, st_o, *, n_mem):
    gn = gn_ref[...]
    lane = lax.broadcasted_iota(jnp.int32, (1, HEADS * R_DK), 1)

    def first_row(a):
        r = lax.broadcasted_iota(jnp.int32, (PAD_ROWS, a.shape[1]), 0)
        return jnp.where(r == 0, jnp.broadcast_to(a, (PAD_ROWS, a.shape[1])), 0.0).astype(BF16)

    for b in range(SAMPLE_BATCH):
        qk = rqk_ref[b]
        q = qk[:, :HEADS * R_DK]
        k = qk[:, HEADS * R_DK:]
        v = rv_ref[b]
        rg = rg_ref[b]
        st = st_ref[b]
        stb = st.astype(BF16)
        kv = _dot_tn(first_row(k), first_row(v))
        for h in range(HEADS):
            hs = slice(h * HEAD_W, (h + 1) * HEAD_W)
            rs = slice(h * R_DK, (h + 1) * R_DK)
            gamma = math.exp(_ret_log_decay(h))
            qm = jnp.where((lane >= h * R_DK) & (lane < (h + 1) * R_DK), q, 0.0)
            score = jnp.sum(qm * k, axis=-1, keepdims=True)
            cross = _dot(jnp.broadcast_to(qm, (PAD_ROWS, HEADS * R_DK)).astype(BF16), stb)[0:1] * gamma
            o = score * v[:, hs] + cross
            oret_o[b, :, hs] = _ret_finish(o, rg[:, hs], gn[:, hs]).astype(BF16)
            st_o[b, rs, :] = st[rs, :] * gamma + kv[rs, hs]

        mq = mq_ref[b]
        for h in range(HEADS):
            hs = slice(h * HEAD_W, (h + 1) * HEAD_W)
            qh = jnp.broadcast_to(mq[:, hs], (PAD_ROWS, HEAD_W))
            kh = mk_ref[b, pl.ds(h, n_mem, stride=HEADS), :]
            vh = mv_ref[b, pl.ds(h, n_mem, stride=HEADS), :]
            omm_o[b, :, hs] = _mem_attend_head(qh, kh, vh)[0:1].astype(BF16)


def _sample_mix(layer, rqk, rv, rg, mq, state, mem_k, mem_v, gn, n_mem):
    db = rqk.shape[0]
    sb = SAMPLE_BATCH
    row = pl.BlockSpec((sb, 1, SEG), lambda b: (b, 0, 0))
    stblk = pl.BlockSpec((None, sb, HEADS * R_DK, HEAD_W), lambda b: (layer, b, 0, 0))
    memblk = pl.BlockSpec((None, sb, n_mem * HEADS, M_DH), lambda b: (layer, b, 0, 0))
    r3 = lambda a: a.reshape(db, 1, SEG)
    return pl.pallas_call(
        functools.partial(_sample_mix_kernel, n_mem=n_mem),
        grid=(db // sb,),
        in_specs=[row, row, row, row, stblk, memblk, memblk, _layer_spec(layer, (1, SEG))],
        out_specs=[row, row, pl.BlockSpec((sb, HEADS * R_DK, HEAD_W), lambda b: (b, 0, 0))],
        out_shape=[jax.ShapeDtypeStruct((db, 1, SEG), BF16), jax.ShapeDtypeStruct((db, 1, SEG), BF16),
                   jax.ShapeDtypeStruct((db, HEADS * R_DK, HEAD_W), F32)],
        compiler_params=_cparams("parallel"),
        name="sample_mix",
    )(r3(rqk), r3(rv), r3(rg), r3(mq), state, mem_k, mem_v, gn)


N_MAPS = 2 * HEADS
PAGES_PER_STEP = 16


def _paged_attn_kernel(pt_ref, lw_ref, q_ref, kn_ref, vn_ref, hn_ref, *rest, page, past_len, lam_init):
    del pt_ref
    npg = PAGES_PER_STEP
    k_refs = rest[:npg]
    v_refs = rest[npg:2 * npg]
    o_ref, m_ref, l_ref, acc_ref = rest[2 * npg:]
    s_idx = pl.program_id(1)

    @pl.when(s_idx == 0)
    def _():
        m_ref[...] = jnp.full_like(m_ref, NEG_INF)
        l_ref[...] = jnp.zeros_like(l_ref)
        acc_ref[...] = jnp.zeros_like(acc_ref)

    r8 = lax.broadcasted_iota(jnp.int32, (N_MAPS, SEG), 0)
    l8 = lax.broadcasted_iota(jnp.int32, (N_MAPS, SEG), 1)
    sel = (l8 >= r8 * C_DQK) & (l8 < (r8 + 1) * C_DQK)
    q8f = jnp.where(sel, jnp.broadcast_to(q_ref[...].astype(F32), (N_MAPS, SEG)), 0.0)
    q8 = q8f.astype(BF16)
    hrow = jnp.right_shift(lax.broadcasted_iota(jnp.int32, (N_MAPS, 1), 0), 1)
    hrow_w = jnp.right_shift(lax.broadcasted_iota(jnp.int32, (N_MAPS, HEAD_W), 0), 1)
    slope = jnp.exp2((-8.0 / HEADS) * (hrow + 1).astype(F32)) * LOG2E
    kpos = s_idx * (npg * page) + lax.broadcasted_iota(jnp.int32, (1, npg * page), 1)
    bias = slope * (kpos - past_len).astype(F32)
    kt = jnp.concatenate([k_refs[p][...].astype(BF16) for p in range(npg)], axis=1)
    s = _dot(q8, kt) + bias
    m_prev = m_ref[...]
    m_new = jnp.maximum(m_prev, jnp.max(s, axis=-1, keepdims=True))
    alpha = jnp.exp2(m_prev - m_new)
    pr = jnp.exp2(s - m_new)
    l_ref[...] = alpha * l_ref[...] + jnp.sum(pr, axis=-1, keepdims=True)
    acc = alpha * acc_ref[...]
    prb = pr.astype(BF16)
    for h in range(HEADS):
        vh = jnp.concatenate([v_refs[p][pl.ds(h, page, stride=HEADS), :].astype(BF16) for p in range(npg)], axis=0)
        acc = acc + jnp.where(hrow_w == h, _dot(prb, vh), 0.0)
    acc_ref[...] = acc
    m_ref[...] = m_new

    @pl.when(s_idx == pl.num_programs(1) - 1)
    def _():
        s_new = jnp.sum(q8f * kn_ref[...], axis=-1, keepdims=True)
        m_fin = jnp.maximum(m_ref[...], s_new)
        a = jnp.exp2(m_ref[...] - m_fin)
        p_new = jnp.exp2(s_new - m_fin)
        l_fin = a * l_ref[...] + p_new
        vn = vn_ref[...]
        vn8 = jnp.zeros((N_MAPS, HEAD_W), F32)
        for h in range(HEADS):
            vn8 = jnp.where(hrow_w == h, jnp.broadcast_to(vn[:, h * HEAD_W:(h + 1) * HEAD_W], (N_MAPS, HEAD_W)), vn8)
        o_all = (a * acc_ref[...] + p_new * vn8) / l_fin
        lam = _diff_lambda(lw_ref[...], lam_init)
        hn = hn_ref[...]
        for h in range(HEADS):
            hs = slice(h * HEAD_W, (h + 1) * HEAD_W)
            o = _diff_finish(o_all[2 * h:2 * h + 1], o_all[2 * h + 1:2 * h + 2], lam, hn[:, hs], lam_init)
            o_ref[:, hs] = o.astype(BF16)


def _paged_attn(layer, page_table, lw, cq, ck, cv, hn, cache_kt, cache_v, lam_init):
    db, n_pages = page_table.shape
    page = cache_kt.shape[3]
    npg = PAGES_PER_STEP
    row = pl.BlockSpec((None, 1, SEG), lambda b, s, pt: (b, 0, 0))

    def page_spec(p):
        return pl.BlockSpec((None, None, SEG, page),
                            lambda b, s, pt: (layer, pt[b * n_pages + s * npg + p], 0, 0))

    r3 = lambda a: a.reshape(db, 1, SEG)
    grid_spec = pltpu.PrefetchScalarGridSpec(
        num_scalar_prefetch=1,
        grid=(db, n_pages // npg),
        in_specs=[pl.BlockSpec((None, 4, C_DQK), lambda b, s, pt: (layer, 0, 0)), row, row, row,
                  pl.BlockSpec((None, 1, SEG), lambda b, s, pt: (layer, 0, 0))]
                 + [page_spec(p) for p in range(npg)] * 2,
        out_specs=row,
        scratch_shapes=[pltpu.VMEM((N_MAPS, 1), F32), pltpu.VMEM((N_MAPS, 1), F32),
                        pltpu.VMEM((N_MAPS, HEAD_W), F32)],
    )
    return pl.pallas_call(
        functools.partial(_paged_attn_kernel, page=page, past_len=n_pages * page, lam_init=lam_init),
        grid_spec=grid_spec,
        out_shape=jax.ShapeDtypeStruct((db, 1, SEG), BF16),
        compiler_params=_cparams("parallel", "arbitrary"),
        name="paged_diff_attn",
    )(page_table.reshape(-1), lw, r3(cq), r3(ck), r3(cv), hn,
      *([cache_kt] * npg), *([cache_v] * npg))


def kernel(x_prompt, x_sample, mem_prompt, cache_diff_k, cache_diff_v, page_table, cache_mem_k, cache_mem_v, state_ret, norm_mix, w_in, ret_norm, cmlp_norm, cmlp_ws, cmlp_bs, diff_qn, diff_kn, diff_lambda_w, diff_hn, mem_norm, w_mem_kv, mem_qn, mem_kn, w_branch, w_out, norm_ffn, w_ffn_in, w_ffn_out):
    batch, seq, _ = x_prompt.shape
    db, dec_seq, _ = x_sample.shape
    assert dec_seq == 1, "the sample group decodes one token per sequence"
    depth = w_in.shape[0]
    n_mem = mem_prompt.shape[1]
    n_phys, page = cache_diff_k.shape[1:3]
    past_len = page_table.shape[1] * page
    n_main = N_SEG * SEG

    w9 = wgl = w_in.astype(BF16)
    wmem = w_mem_kv.astype(BF16)
    wb = w_branch.astype(BF16)
    wo = w_out.astype(BF16)
    wfi = w_ffn_in.astype(BF16)
    wfo = w_ffn_out.astype(BF16)
    bst = cmlp_bs.transpose(0, 2, 1)
    tile = lambda a, n: jnp.tile(a, (1, n)).reshape(depth, 1, SEG)
    qn, kn = tile(diff_qn, SEG // C_DQK), tile(diff_kn, SEG // C_DQK)
    mqn, mkn = tile(mem_qn, HEADS), tile(mem_kn, HEADS)
    row = lambda a: a.reshape(depth, 1, -1)
    g_mix, g_ffn, g_mem = row(norm_mix), row(norm_ffn), row(mem_norm)
    g_ret, g_cm, g_hn = row(ret_norm), row(cmlp_norm), row(diff_hn)

    cache_kt = cache_diff_k.transpose(0, 1, 3, 4, 5, 2).reshape(depth, n_phys, SEG, page)
    cache_v = cache_diff_v.reshape(depth, n_phys, page * HEADS, HEAD_W)
    mem_k_s = cache_mem_k.reshape(depth, db, n_mem * HEADS, M_DH)
    mem_v_s = cache_mem_v.reshape(depth, db, n_mem * HEADS, M_DH)
    state_s = state_ret.reshape(depth, db, HEADS * R_DK, HEAD_W)

    xp = x_prompt.reshape(batch * seq, D_MODEL)
    xs = x_sample.reshape(db, D_MODEL)
    mem = mem_prompt.reshape(batch * n_mem, D_MODEL)

    ckt_all = jnp.zeros((depth, batch, SEG, seq), F32)
    cv4_all = jnp.zeros((depth, batch * seq * HEADS, HEAD_W), F32)
    mk_all = jnp.zeros((depth, batch * n_mem * HEADS, M_DH), F32)
    mv_all = jnp.zeros((depth, batch * n_mem * HEADS, M_DH), F32)

    pst = []
    sk, sv, sst, scv = [], [], [], []
    for l in range(depth):
        lam_init = 0.8 - 0.6 * math.exp(-0.3 * l)
        lw = diff_lambda_w

        rqk, rv, rg, o_cm, cq, ckt_all, ckb, cv4_all, cvb, mq = _proj_in_prompt(
            xp, g_mix, w9, g_cm, cmlp_ws, bst, qn, kn, mqn, ckt_all, cv4_all, layer=l, seq=seq)
        o_ret, st_p = _retention_prompt(l, rqk, rv, rg, g_ret, batch, seq)
        o_df = _diff_attn_prompt(l, lw, cq, ckb, cvb, g_hn, batch, seq, lam_init)
        mk_all, mv_all = _mem_kv(mem, g_mem, wmem, mkn, mk_all, mv_all, layer=l)
        o_mm = _mem_attn_prompt(l, mq, mk_all, mv_all, batch, seq, n_mem)
        xp = _merge(l, xp, g_mix, (o_ret, o_cm, o_df, o_mm), wgl, wb, wo)
        xp = _ffn(l, xp, g_ffn, wfi, wfo)
        pst.append(st_p.reshape(batch, HEADS, R_DK, HEAD_W))

        rqk, rv, rg, o_cm, vn_s, cq, ck, cv, mq = _proj_in_sample(
            xs, g_mix, w9, g_cm, cmlp_ws, bst, qn, kn, mqn, layer=l, off=past_len % CHUNK)
        o_ret, o_mm, st_s = _sample_mix(l, rqk, rv, rg, mq, state_s, mem_k_s, mem_v_s, g_ret, n_mem)
        o_df = _paged_attn(l, page_table, lw, cq, ck, cv, g_hn, cache_kt, cache_v, lam_init)
        xs = _merge(l, xs, g_mix, (o_ret.reshape(db, SEG), o_cm, o_df.reshape(db, SEG), o_mm.reshape(db, SEG)),
                    wgl, wb, wo)
        xs = _ffn(l, xs, g_ffn, wfi, wfo)
        sk.append(ck.reshape(db, 1, HEADS, 2, C_DQK))
        sv.append(cv.reshape(db, 1, HEADS, HEAD_W))
        sst.append(st_s.reshape(db, HEADS, R_DK, HEAD_W))
        scv.append(vn_s.reshape(db, 1, SEG))

    new_k = ckt_all.reshape(depth, batch, HEADS, 2, C_DQK, seq).transpose(0, 1, 5, 2, 3, 4)
    new_v = cv4_all.reshape(depth, batch, seq, HEADS, HEAD_W)
    new_mk = mk_all.reshape(depth, batch, n_mem, HEADS, M_DH)
    new_mv = mv_all.reshape(depth, batch, n_mem, HEADS, M_DH)
    return (xp.reshape(batch, seq, D_MODEL), xs.reshape(db, 1, D_MODEL),
            new_k, new_v, new_mk, new_mv, jnp.stack(pst),
            jnp.stack(sk), jnp.stack(sv), jnp.stack(sst), jnp.stack(scv))
```

```python
import functools
import math

import jax
import jax.numpy as jnp
from jax import lax
from jax.experimental import pallas as pl
from jax.experimental.pallas import tpu as pltpu

F32 = jnp.float32
BF16 = jnp.bfloat16

EPS = 1e-6
NEG_INF = -1e30
LOG2E = math.log2(math.e)

D_MODEL = 1024
SEG = 512
N_SEG = 9
HEADS = 4
R_DK = 64
HEAD_W = 128
CHUNK = 128
C_DQK = 64
M_DH = 128
N_BRANCH = 4
D_FF = 2816
FFN_CHUNKS = ((0, 1024), (1024, 2048), (2048, 2816))

VMEM_LIMIT_BYTES = 56 * 1024 * 1024


def _cparams(*sem):
    return pltpu.CompilerParams(dimension_semantics=sem, vmem_limit_bytes=VMEM_LIMIT_BYTES)


def _dot(a, b):
    return jnp.dot(a, b, preferred_element_type=F32)


def _dot_nt(a, b):
    return lax.dot_general(a, b, (((1,), (1,)), ((), ())), preferred_element_type=F32)


def _dot_tn(a, b):
    return lax.dot_general(a, b, (((0,), (0,)), ((), ())), preferred_element_type=F32)


def _sigmoid(x):
    return 1.0 / (1.0 + jnp.exp(-x))


def _rms(x, g):
    return x * lax.rsqrt(jnp.mean(x * x, axis=-1, keepdims=True) + EPS) * g


def _layer_norm(x, g):
    xc = x - jnp.mean(x, axis=-1, keepdims=True)
    return xc * lax.rsqrt(jnp.mean(xc * xc, axis=-1, keepdims=True) + EPS) * g


def _head_rms(y, g, group):
    lane = lax.broadcasted_iota(jnp.int32, (1, HEAD_W), 1)
    outs = []
    for hb in range(SEG // HEAD_W):
        blk = y[:, hb * HEAD_W:(hb + 1) * HEAD_W]
        sq = blk * blk
        if group == HEAD_W:
            ms = jnp.mean(sq, axis=-1, keepdims=True)
        else:
            lo = jnp.sum(jnp.where(lane < group, sq, 0.0), axis=-1, keepdims=True)
            hi = jnp.sum(jnp.where(lane < group, 0.0, sq), axis=-1, keepdims=True)
            ms = jnp.where(lane < group, lo, hi) * (1.0 / group)
        outs.append(blk * lax.rsqrt(ms + EPS))
    return jnp.concatenate(outs, axis=-1) * g


def _diff_lambda(wl, lam_init):
    a = jnp.sum(wl[0:1] * wl[1:2], axis=-1, keepdims=True)
    b = jnp.sum(wl[2:3] * wl[3:4], axis=-1, keepdims=True)
    return jnp.exp(a) - jnp.exp(b) + lam_init


def _layer_spec(layer, shape):
    nd = len(shape)
    return pl.BlockSpec((None,) + tuple(shape), lambda *_: (layer,) + (0,) * nd,
                        pipeline_mode=pl.Buffered(1))


def _proj_kernel(x_ref, g_ref, w_ref, cn_ref, ws_ref, bst_ref, qn_ref, kn_ref, mqn_ref, *rest,
                 prompt, tm, off):
    if prompt:
        rqk_o, rv_o, rg_o, ocm_o, cq_o, ckt_o, ckb_o, cv4_o, cvb_o, mq_o = rest[2:]
    else:
        rqk_o, rv_o, rg_o, ocm_o, vn_o, cq_o, ck_o, cv_o, mq_o = rest
    hb = _rms(x_ref[...], g_ref[...]).astype(BF16)

    def seg(s):
        return _dot(hb, w_ref[:, s * SEG:(s + 1) * SEG])

    lane = lax.broadcasted_iota(jnp.int32, (1, SEG), 1)
    y = seg(0) * jnp.where(lane < HEADS * R_DK, 1.0, R_DK ** -0.5)
    rqk_o[...] = y.astype(rqk_o.dtype)
    rv_o[...] = seg(1)
    rg_o[...] = seg(2)

    gu = seg(3)
    vn = _layer_norm(seg(4), cn_ref[...])
    if prompt:
        vnb = vn.astype(BF16)
        row = lax.broadcasted_iota(jnp.int32, (CHUNK, CHUNK), 0)
        col = lax.broadcasted_iota(jnp.int32, (CHUNK, CHUNK), 1)
        for g in range(HEADS):
            gs = slice(g * HEAD_W, (g + 1) * HEAD_W)
            wt = jnp.where(row >= col, ws_ref[g], 0.0).astype(BF16)
            bcol = bst_ref[:, g:g + 1]
            for c in range(tm // CHUNK):
                cs = slice(c * CHUNK, (c + 1) * CHUNK)
                mix = _dot(wt, vnb[cs, gs]) + bcol
                ocm_o[cs, gs] = (gu[cs, gs] * mix).astype(BF16)
    else:
        vn_o[...] = vn
        for g in range(HEADS):
            gs = slice(g * HEAD_W, (g + 1) * HEAD_W)
            w00 = ws_ref[g][off:off + 1, off:off + 1]
            b0 = bst_ref[off:off + 1, g:g + 1]
            ocm_o[:, gs] = (gu[:, gs] * (w00 * vn[:, gs] + b0)).astype(BF16)

    cq_o[...] = (_head_rms(seg(5), qn_ref[...], C_DQK) * (C_DQK ** -0.5 * LOG2E)).astype(BF16)
    ck = _head_rms(seg(6), kn_ref[...], C_DQK)
    cv = seg(7)
    if prompt:
        ckt_o[...] = ck.T
        for h in range(HEADS):
            cv4_o[pl.ds(h, tm, stride=HEADS), :] = cv[:, h * HEAD_W:(h + 1) * HEAD_W]
        ckb_o[...] = ck.astype(BF16)
        cvb_o[...] = cv.astype(BF16)
    else:
        ck_o[...] = ck
        cv_o[...] = cv
    mq_o[...] = (_head_rms(seg(8), mqn_ref[...], M_DH) * (M_DH ** -0.5)).astype(BF16)


def _proj_in_specs(layer, tm):
    row = lambda i: (i, 0)
    return [pl.BlockSpec((tm, D_MODEL), row), _layer_spec(layer, (1, D_MODEL)),
            _layer_spec(layer, (D_MODEL, N_SEG * SEG)), _layer_spec(layer, (1, SEG)),
            _layer_spec(layer, (HEADS, CHUNK, CHUNK)), _layer_spec(layer, (CHUNK, HEADS)),
            _layer_spec(layer, (1, SEG)), _layer_spec(layer, (1, SEG)), _layer_spec(layer, (1, SEG))]


def _proj_in_sample(x, g, w9, cn, ws, bst, qn, kn, mqn, *, layer, off):
    m = x.shape[0]
    blk = pl.BlockSpec((m, SEG), lambda i: (i, 0))
    dts = (F32, F32, F32, BF16, F32, BF16, F32, F32, BF16)
    return pl.pallas_call(
        functools.partial(_proj_kernel, prompt=False, tm=m, off=off),
        grid=(1,),
        in_specs=_proj_in_specs(layer, m),
        out_specs=[blk] * len(dts),
        out_shape=[jax.ShapeDtypeStruct((m, SEG), dt) for dt in dts],
        compiler_params=_cparams("parallel"),
        name="proj_in_sample",
    )(x, g, w9, cn, ws, bst, qn, kn, mqn)


def _proj_in_prompt(x, g, w9, cn, ws, bst, qn, kn, mqn, ckt_all, cv4_all, *, layer, seq):
    m = x.shape[0]
    tm = 512
    nq = seq // tm
    blk = pl.BlockSpec((tm, SEG), lambda i: (i, 0))
    cktblk = pl.BlockSpec((None, None, SEG, tm), lambda i: (layer, i // nq, 0, i % nq))
    cv4blk = pl.BlockSpec((None, tm * HEADS, HEAD_W), lambda i: (layer, i, 0))
    sds = lambda dt: jax.ShapeDtypeStruct((m, SEG), dt)
    in_specs = _proj_in_specs(layer, tm)
    n_in = len(in_specs)
    return pl.pallas_call(
        functools.partial(_proj_kernel, prompt=True, tm=tm, off=0),
        grid=(m // tm,),
        in_specs=in_specs + [pl.BlockSpec(memory_space=pl.ANY)] * 2,
        out_specs=[blk, blk, blk, blk, blk, cktblk, blk, cv4blk, blk, blk],
        out_shape=[sds(BF16), sds(F32), sds(F32), sds(BF16), sds(BF16),
                   jax.ShapeDtypeStruct(ckt_all.shape, F32), sds(BF16),
                   jax.ShapeDtypeStruct(cv4_all.shape, F32), sds(BF16), sds(BF16)],
        input_output_aliases={n_in: 5, n_in + 1: 7},
        compiler_params=_cparams("parallel"),
        name="proj_in_prompt",
    )(x, g, w9, cn, ws, bst, qn, kn, mqn, ckt_all, cv4_all)


RET_BATCH = 8


def _ret_log_decay(h):
    return math.log1p(-(2.0 ** (-5 - h)))


def _ret_finish(o, gate, gn):
    return gate * _sigmoid(gate) * _layer_norm(o, gn)


def _retention_kernel(rqk_ref, rv_ref, rg_ref, gn_ref, o_ref, st_o, st_ref, dec_ref, qd_ref, kd_ref):
    c = pl.program_id(1)

    @pl.when(c == 0)
    def _():
        st_ref[...] = jnp.zeros_like(st_ref)
        i = lax.broadcasted_iota(jnp.int32, (CHUNK, CHUNK), 0).astype(F32)
        j = lax.broadcasted_iota(jnp.int32, (CHUNK, CHUNK), 1).astype(F32)
        for h in range(HEADS):
            lg = _ret_log_decay(h)
            dec_ref[h] = jnp.where(i >= j, jnp.exp(lg * jnp.maximum(i - j, 0.0)), 0.0)
            qd_ref[h] = jnp.exp(lg * (i + 1.0))
            kd_ref[h] = jnp.exp(lg * (CHUNK - 1.0 - i))

    gn = gn_ref[...]
    lane = lax.broadcasted_iota(jnp.int32, (1, HEADS * R_DK), 1)
    for b in range(RET_BATCH):
        qk = rqk_ref[b]
        q_all = qk[:, :HEADS * R_DK]
        k_all = qk[:, HEADS * R_DK:]
        v_all = rv_ref[b]
        rg = rg_ref[b]
        st = st_ref[b]
        stb = st.astype(BF16)
        zero = jnp.zeros_like(q_all)
        vk_parts = []
        for h in range(HEADS):
            hs = slice(h * HEAD_W, (h + 1) * HEAD_W)
            v = v_all[:, hs]
            qm = jnp.where((lane >= h * R_DK) & (lane < (h + 1) * R_DK), q_all, zero)
            s = _dot_nt(qm, k_all) * dec_ref[h]
            intra = _dot(s.astype(BF16), v.astype(BF16))
            cross = _dot(qm, stb) * qd_ref[h]
            o_ref[b, :, hs] = _ret_finish(intra + cross, rg[:, hs], gn[:, hs]).astype(BF16)
            vk_parts.append((v * kd_ref[h]).astype(BF16))
        kv = _dot_tn(k_all, jnp.concatenate(vk_parts, axis=-1))
        for h in range(HEADS):
            rs = slice(h * R_DK, (h + 1) * R_DK)
            st_ref[b, rs, :] = (st[rs, :] * math.exp(_ret_log_decay(h) * CHUNK)
                                + kv[rs, h * HEAD_W:(h + 1) * HEAD_W])

    @pl.when(c == pl.num_programs(1) - 1)
    def _():
        st_o[...] = st_ref[...]


def _retention_prompt(layer, rqk, rv, rg, gn, batch, seq):
    nc = seq // CHUNK
    blk = pl.BlockSpec((RET_BATCH, CHUNK, SEG), lambda b, c: (b, c, 0))
    stblk = pl.BlockSpec((RET_BATCH, HEADS * R_DK, HEAD_W), lambda b, c: (b, 0, 0))
    tbl = pltpu.VMEM((HEADS, CHUNK, CHUNK), F32)
    r3 = lambda a: a.reshape(batch, seq, SEG)
    o_ret, st = pl.pallas_call(
        _retention_kernel,
        grid=(batch // RET_BATCH, nc),
        in_specs=[blk, blk, blk, _layer_spec(layer, (1, SEG))],
        out_specs=[blk, stblk],
        out_shape=[jax.ShapeDtypeStruct((batch, seq, SEG), BF16),
                   jax.ShapeDtypeStruct((batch, HEADS * R_DK, HEAD_W), F32)],
        scratch_shapes=[pltpu.VMEM((RET_BATCH, HEADS * R_DK, HEAD_W), F32), tbl, tbl, tbl],
        compiler_params=_cparams("parallel", "arbitrary"),
        name="retention_prompt",
    )(r3(rqk), r3(rv), r3(rg), gn)
    return o_ret.reshape(batch * seq, SEG), st


def _diff_finish(o0, o1, lam, hn, lam_init):
    o = o0 - lam * o1
    return _rms(o, hn) * (1.0 - lam_init)


def _alibi_slope_log2(h, shape):
    return jnp.exp2(jnp.full(shape, -8.0 / HEADS, F32) * (h + 1).astype(F32)) * LOG2E


ONES_ROWS = 16


def _diff_attn_kernel(lw_ref, q_ref, k_ref, v_ref, hn_ref, o_ref, vt_ref, acc_ref, s_ref, *, tq, nblk, lam_init):
    h = pl.program_id(1)
    ones = jnp.ones((ONES_ROWS, tq), BF16)
    for t in range(nblk):
        vt_ref[t, :HEAD_W, :] = v_ref[t * tq:(t + 1) * tq, :].astype(F32).T.astype(BF16)
        vt_ref[t, HEAD_W:, :] = ones

    lane = lax.broadcasted_iota(jnp.int32, (1, HEAD_W), 1)
    krow = lax.broadcasted_iota(jnp.int32, (tq, HEAD_W), 0).astype(F32)
    brep = _alibi_slope_log2(h, (tq, HEAD_W)) * krow
    bias = jnp.concatenate([brep] * (tq // HEAD_W), axis=1)
    slope_row = _alibi_slope_log2(h, (1, tq))
    rowi = lax.broadcasted_iota(jnp.int32, (tq, tq), 0)
    coli = lax.broadcasted_iota(jnp.int32, (tq, tq), 1)
    lam = _diff_lambda(lw_ref[...], lam_init)
    hn = hn_ref[...]

    q_maps = {}

    def maps_of(i):
        if i not in q_maps:
            q = q_ref[i * tq:(i + 1) * tq, :]
            zero = jnp.zeros_like(q)
            q_maps[i] = (jnp.where(lane < C_DQK, q, zero), jnp.where(lane < C_DQK, zero, q))
        return q_maps[i]

    def scores(i, j, c, slot):
        s_ref[slot, c] = bias + _dot_nt(k_ref[j * tq:(j + 1) * tq, :], maps_of(i)[c])

    schedule = [(i, j) for i in range(nblk) for j in range(i + 1)]
    scores(0, 0, 0, 0)
    scores(0, 0, 1, 0)
    m = [None, None]
    for n, (i, j) in enumerate(schedule):
        slot = n % 2
        nxt = schedule[n + 1] if n + 1 < len(schedule) else None
        off = slope_row * float((j - i) * tq)
        for c in range(2):
            if nxt is not None:
                scores(nxt[0], nxt[1], c, 1 - slot)
            s = s_ref[slot, c]
            if j == i:
                s = jnp.where(coli >= rowi, s, NEG_INF)
            blk_max = jnp.max(s, axis=0, keepdims=True) + off
            if j == 0:
                m_new = blk_max
                acc_ref[c] = _dot(vt_ref[j], jnp.exp2(s - (m_new - off)).astype(BF16))
            else:
                m_new = jnp.maximum(m[c], blk_max)
                p = jnp.exp2(s - (m_new - off)).astype(BF16)
                acc_ref[c] = jnp.exp2(m[c] - m_new) * acc_ref[c] + _dot(vt_ref[j], p)
            m[c] = m_new
        if j == i:
            o0 = acc_ref[0, :HEAD_W, :] / acc_ref[0, HEAD_W:HEAD_W + 1, :]
            o1 = acc_ref[1, :HEAD_W, :] / acc_ref[1, HEAD_W:HEAD_W + 1, :]
            o = o0 - lam * o1
            y = o * lax.rsqrt(jnp.mean(o * o, axis=0, keepdims=True) + EPS)
            o_ref[i * tq:(i + 1) * tq, :] = (y.T * hn * (1.0 - lam_init)).astype(BF16)


def _diff_attn_prompt(layer, lw, cq, ckb, cvb, hn, batch, seq, lam_init):
    tq = 512
    nq = seq // tq
    blk = pl.BlockSpec((seq, HEAD_W), lambda b, h: (b, h))
    return pl.pallas_call(
        functools.partial(_diff_attn_kernel, tq=tq, nblk=nq, lam_init=lam_init),
        grid=(batch, HEADS),
        in_specs=[_layer_spec(layer, (4, C_DQK)), blk, blk, blk,
                  pl.BlockSpec((None, 1, HEAD_W), lambda b, h: (layer, 0, h))],
        out_specs=blk,
        out_shape=jax.ShapeDtypeStruct((batch * seq, SEG), BF16),
        scratch_shapes=[pltpu.VMEM((nq, HEAD_W + ONES_ROWS, tq), BF16),
                        pltpu.VMEM((2, HEAD_W + ONES_ROWS, tq), F32),
                        pltpu.VMEM((2, 2, tq, tq), F32)],
        compiler_params=_cparams("parallel", "parallel"),
        name="diff_attn_prompt",
    )(lw, cq, ckb, cvb, hn)


def _mem_kv_kernel(x_ref, g_ref, w_ref, kn_ref, k_all, v_all, k_o, v_o, *, tm):
    del k_all, v_all
    hb = _rms(x_ref[...], g_ref[...]).astype(BF16)
    k = _head_rms(_dot(hb, w_ref[:, :SEG]), kn_ref[...], M_DH)
    v = _dot(hb, w_ref[:, SEG:])
    for h in range(HEADS):
        hs = slice(h * HEAD_W, (h + 1) * HEAD_W)
        k_o[pl.ds(h, tm, stride=HEADS), :] = k[:, hs]
        v_o[pl.ds(h, tm, stride=HEADS), :] = v[:, hs]


def _mem_kv(mem, g, w2, kn, mk_all, mv_all, *, layer):
    m = mem.shape[0]
    tm = 512
    row = lambda i: (i, 0)
    blk = pl.BlockSpec((None, tm * HEADS, M_DH), lambda i: (layer, i, 0))
    return pl.pallas_call(
        functools.partial(_mem_kv_kernel, tm=tm),
        grid=(m // tm,),
        in_specs=[pl.BlockSpec((tm, D_MODEL), row), _layer_spec(layer, (1, D_MODEL)),
                  _layer_spec(layer, (D_MODEL, 2 * SEG)), _layer_spec(layer, (1, SEG)),
                  pl.BlockSpec(memory_space=pl.ANY), pl.BlockSpec(memory_space=pl.ANY)],
        out_specs=[blk, blk],
        out_shape=[jax.ShapeDtypeStruct(mk_all.shape, F32)] * 2,
        input_output_aliases={4: 0, 5: 1},
        compiler_params=_cparams("parallel"),
        name="mem_kv",
    )(mem, g, w2, kn, mk_all, mv_all)


def _mem_attend_head(qh, kh, vh):
    s = _dot_nt(qh, kh.astype(BF16))
    p = jnp.exp(s - jnp.max(s, axis=-1, keepdims=True))
    o = _dot(p.astype(BF16), vh.astype(BF16))
    return o / jnp.sum(p, axis=-1, keepdims=True)


def _mem_attn_kernel(q_ref, k_ref, v_ref, o_ref, *, n_mem):
    for h in range(HEADS):
        hs = slice(h * HEAD_W, (h + 1) * HEAD_W)
        kh = k_ref[pl.ds(h, n_mem, stride=HEADS), :]
        vh = v_ref[pl.ds(h, n_mem, stride=HEADS), :]
        o_ref[:, hs] = _mem_attend_head(q_ref[:, hs], kh, vh).astype(BF16)


def _mem_attn_prompt(layer, mq, mk_all, mv_all, batch, seq, n_mem):
    tq = 512
    nq = seq // tq
    qblk = pl.BlockSpec((tq, SEG), lambda i: (i, 0))
    kvblk = pl.BlockSpec((None, n_mem * HEADS, M_DH), lambda i: (layer, i // nq, 0))
    return pl.pallas_call(
        functools.partial(_mem_attn_kernel, n_mem=n_mem),
        grid=(batch * nq,),
        in_specs=[qblk, kvblk, kvblk],
        out_specs=qblk,
        out_shape=jax.ShapeDtypeStruct((batch * seq, SEG), BF16),
        compiler_params=_cparams("parallel"),
        name="mem_attn_prompt",
    )(mq, mk_all, mv_all)


GATE_SEGS = N_BRANCH * D_MODEL // SEG


def _merge_kernel(x_ref, g_ref, b0_ref, b1_ref, b2_ref, b3_ref, *rest):
    wg_refs = rest[:GATE_SEGS]
    wb_ref, wo_ref, o_ref = rest[GATE_SEGS:]
    per = GATE_SEGS // N_BRANCH
    x = x_ref[...]
    hb = _rms(x, g_ref[...]).astype(BF16)
    merged = None
    for n, b_ref in enumerate((b0_ref, b1_ref, b2_ref, b3_ref)):
        logits = jnp.concatenate([_dot(hb, wg_refs[n * per + k][...]) for k in range(per)], axis=1)
        gate = _sigmoid(logits)
        term = gate * _dot(b_ref[...], wb_ref[n])
        merged = term if merged is None else merged + term
    o_ref[...] = x + _dot(merged.astype(BF16), wo_ref[...])


def _merge(layer, x, g, branches, w_in, wb, wo):
    m = x.shape[0]
    tm = min(512, m)
    row = lambda i: (i, 0)
    xblk = pl.BlockSpec((tm, D_MODEL), row)
    bblk = pl.BlockSpec((tm, SEG), row)

    def gate_spec(k):
        return pl.BlockSpec((None, D_MODEL, SEG), lambda i: (layer, 0, N_SEG + k), pipeline_mode=pl.Buffered(1))

    return pl.pallas_call(
        _merge_kernel,
        grid=(m // tm,),
        in_specs=[xblk, _layer_spec(layer, (1, D_MODEL)), bblk, bblk, bblk, bblk]
                 + [gate_spec(k) for k in range(GATE_SEGS)]
                 + [_layer_spec(layer, (N_BRANCH, SEG, D_MODEL)), _layer_spec(layer, (D_MODEL, D_MODEL))],
        out_specs=xblk,
        out_shape=jax.ShapeDtypeStruct((m, D_MODEL), F32),
        compiler_params=_cparams("parallel"),
        name="merge",
    )(x, g, *branches, *([w_in] * GATE_SEGS), wb, wo)


def _ffn_kernel(x_ref, g_ref, wi_ref, wo_ref, o_ref):
    x = x_ref[...]
    hb = _rms(x, g_ref[...]).astype(BF16)
    acc = x
    for a, b in FFN_CHUNKS:
        gate = _dot(hb, wi_ref[:, a:b])
        up = _dot(hb, wi_ref[:, D_FF + a:D_FF + b])
        act = (gate * _sigmoid(gate) * up).astype(BF16)
        acc = acc + _dot(act, wo_ref[a:b, :])
    o_ref[...] = acc


def _ffn(layer, x, g, wi, wo):
    m = x.shape[0]
    tm = min(512, m)
    row = lambda i: (i, 0)
    xblk = pl.BlockSpec((tm, D_MODEL), row)
    return pl.pallas_call(
        _ffn_kernel,
        grid=(m // tm,),
        in_specs=[xblk, _layer_spec(layer, (1, D_MODEL)), _layer_spec(layer, (D_MODEL, 2 * D_FF)),
                  _layer_spec(layer, (D_FF, D_MODEL))],
        out_specs=xblk,
        out_shape=jax.ShapeDtypeStruct((m, D_MODEL), F32),
        compiler_params=_cparams("parallel"),
        name="ffn",
    )(x, g, wi, wo)


PAD_ROWS = 16
SAMPLE_BATCH = 4


def _sample_mix_kernel(rqk_ref, rv_ref, rg_ref, mq_ref, st_ref, mk_ref, mv_ref, gn_ref,
                       oret_o, omm_o, st_o, *, n_mem):
    gn = gn_ref[...]
    lane = lax.broadcasted_iota(jnp.int32, (1, HEADS * R_DK), 1)

    def first_row(a):
        r = lax.broadcasted_iota(jnp.int32, (PAD_ROWS, a.shape[1]), 0)
        return jnp.where(r == 0, jnp.broadcast_to(a, (PAD_ROWS, a.shape[1])), 0.0).astype(BF16)

    for b in range(SAMPLE_BATCH):
        qk = rqk_ref[b]
        q = qk[:, :HEADS * R_DK]
        k = qk[:, HEADS * R_DK:]
        v = rv_ref[b]
        rg = rg_ref[b]
        st = st_ref[b]
        stb = st.astype(BF16)
        kv = _dot_tn(first_row(k), first_row(v))
        for h in range(HEADS):
            hs = slice(h * HEAD_W, (h + 1) * HEAD_W)
            rs = slice(h * R_DK, (h + 1) * R_DK)
            gamma = math.exp(_ret_log_decay(h))
            qm = jnp.where((lane >= h * R_DK) & (lane < (h + 1) * R_DK), q, 0.0)
            score = jnp.sum(qm * k, axis=-1, keepdims=True)
            cross = _dot(jnp.broadcast_to(qm, (PAD_ROWS, HEADS * R_DK)).astype(BF16), stb)[0:1] * gamma
            o = score * v[:, hs] + cross
            oret_o[b, :, hs] = _ret_finish(o, rg[:, hs], gn[:, hs]).astype(BF16)
            st_o[b, rs, :] = st[rs, :] * gamma + kv[rs, hs]

        mq = mq_ref[b]
        for h in range(HEADS):
            hs = slice(h * HEAD_W, (h + 1) * HEAD_W)
            qh = jnp.broadcast_to(mq[:, hs], (PAD_ROWS, HEAD_W))
            kh = mk_ref[b, pl.ds(h, n_mem, stride=HEADS), :]
            vh = mv_ref[b, pl.ds(h, n_mem, stride=HEADS), :]
            omm_o[b, :, hs] = _mem_attend_head(qh, kh, vh)[0:1].astype(BF16)


def _sample_mix(layer, rqk, rv, rg, mq, state, mem_k, mem_v, gn, n_mem):
    db = rqk.shape[0]
    sb = SAMPLE_BATCH
    row = pl.BlockSpec((sb, 1, SEG), lambda b: (b, 0, 0))
    stblk = pl.BlockSpec((None, sb, HEADS * R_DK, HEAD_W), lambda b: (layer, b, 0, 0))
    memblk = pl.BlockSpec((None, sb, n_mem * HEADS, M_DH), lambda b: (layer, b, 0, 0))
    r3 = lambda a: a.reshape(db, 1, SEG)
    return pl.pallas_call(
        functools.partial(_sample_mix_kernel, n_mem=n_mem),
        grid=(db // sb,),
        in_specs=[row, row, row, row, stblk, memblk, memblk, _layer_spec(layer, (1, SEG))],
        out_specs=[row, row, pl.BlockSpec((sb, HEADS * R_DK, HEAD_W), lambda b: (b, 0, 0))],
        out_shape=[jax.ShapeDtypeStruct((db, 1, SEG), BF16), jax.ShapeDtypeStruct((db, 1, SEG), BF16),
                   jax.ShapeDtypeStruct((db, HEADS * R_DK, HEAD_W), F32)],
        compiler_params=_cparams("parallel"),
        name="sample_mix",
    )(r3(rqk), r3(rv), r3(rg), r3(mq), state, mem_k, mem_v, gn)


N_MAPS = 2 * HEADS
PAGES_PER_STEP = 32


def _paged_attn_kernel(pt_ref, lw_ref, q_ref, kn_ref, vn_ref, hn_ref, *rest, page, past_len, lam_init):
    del pt_ref
    npg = PAGES_PER_STEP
    k_refs = rest[:npg]
    v_refs = rest[npg:2 * npg]
    o_ref, m_ref, l_ref, acc_ref = rest[2 * npg:]
    s_idx = pl.program_id(1)

    @pl.when(s_idx == 0)
    def _():
        m_ref[...] = jnp.full_like(m_ref, NEG_INF)
        l_ref[...] = jnp.zeros_like(l_ref)
        acc_ref[...] = jnp.zeros_like(acc_ref)

    r8 = lax.broadcasted_iota(jnp.int32, (N_MAPS, SEG), 0)
    l8 = lax.broadcasted_iota(jnp.int32, (N_MAPS, SEG), 1)
    sel = (l8 >= r8 * C_DQK) & (l8 < (r8 + 1) * C_DQK)
    q8f = jnp.where(sel, jnp.broadcast_to(q_ref[...].astype(F32), (N_MAPS, SEG)), 0.0)
    q8 = q8f.astype(BF16)
    hrow = jnp.right_shift(lax.broadcasted_iota(jnp.int32, (N_MAPS, 1), 0), 1)
    hrow_w = jnp.right_shift(lax.broadcasted_iota(jnp.int32, (N_MAPS, HEAD_W), 0), 1)
    slope = jnp.exp2((-8.0 / HEADS) * (hrow + 1).astype(F32)) * LOG2E
    kpos = s_idx * (npg * page) + lax.broadcasted_iota(jnp.int32, (1, npg * page), 1)
    bias = slope * (kpos - past_len).astype(F32)
    kt = jnp.concatenate([k_refs[p][...].astype(BF16) for p in range(npg)], axis=1)
    s = _dot(q8, kt) + bias
    m_prev = m_ref[...]
    m_new = jnp.maximum(m_prev, jnp.max(s, axis=-1, keepdims=True))
    alpha = jnp.exp2(m_prev - m_new)
    pr = jnp.exp2(s - m_new)
    l_ref[...] = alpha * l_ref[...] + jnp.sum(pr, axis=-1, keepdims=True)
    acc = alpha * acc_ref[...]
    prb = pr.astype(BF16)
    for h in range(HEADS):
        vh = jnp.concatenate([v_refs[p][pl.ds(h, page, stride=HEADS), :].astype(BF16) for p in range(npg)], axis=0)
        acc = acc + jnp.where(hrow_w == h, _dot(prb, vh), 0.0)
    acc_ref[...] = acc
    m_ref[...] = m_new

    @pl.when(s_idx == pl.num_programs(1) - 1)
    def _():
        s_new = jnp.sum(q8f * kn_ref[...], axis=-1, keepdims=True)
        m_fin = jnp.maximum(m_ref[...], s_new)
        a = jnp.exp2(m_ref[...] - m_fin)
        p_new = jnp.exp2(s_new - m_fin)
        l_fin = a * l_ref[...] + p_new
        vn = vn_ref[...]
        vn8 = jnp.zeros((N_MAPS, HEAD_W), F32)
        for h in range(HEADS):
            vn8 = jnp.where(hrow_w == h, jnp.broadcast_to(vn[:, h * HEAD_W:(h + 1) * HEAD_W], (N_MAPS, HEAD_W)), vn8)
        o_all = (a * acc_ref[...] + p_new * vn8) / l_fin
        lam = _diff_lambda(lw_ref[...], lam_init)
        hn = hn_ref[...]
        for h in range(HEADS):
            hs = slice(h * HEAD_W, (h + 1) * HEAD_W)
            o = _diff_finish(o_all[2 * h:2 * h + 1], o_all[2 * h + 1:2 * h + 2], lam, hn[:, hs], lam_init)
            o_ref[:, hs] = o.astype(BF16)


def _paged_attn(layer, page_table, lw, cq, ck, cv, hn, cache_kt, cache_v, lam_init):
    db, n_pages = page_table.shape
    page = cache_kt.shape[3]
    npg = PAGES_PER_STEP
    row = pl.BlockSpec((None, 1, SEG), lambda b, s, pt: (b, 0, 0))

    def page_spec(p):
        return pl.BlockSpec((None, None, SEG, page),
                            lambda b, s, pt: (layer, pt[b * n_pages + s * npg + p], 0, 0))

    r3 = lambda a: a.reshape(db, 1, SEG)
    grid_spec = pltpu.PrefetchScalarGridSpec(
        num_scalar_prefetch=1,
        grid=(db, n_pages // npg),
        in_specs=[pl.BlockSpec((None, 4, C_DQK), lambda b, s, pt: (layer, 0, 0)), row, row, row,
                  pl.BlockSpec((None, 1, SEG), lambda b, s, pt: (layer, 0, 0))]
                 + [page_spec(p) for p in range(npg)] * 2,
        out_specs=row,
        scratch_shapes=[pltpu.VMEM((N_MAPS, 1), F32), pltpu.VMEM((N_MAPS, 1), F32),
                        pltpu.VMEM((N_MAPS, HEAD_W), F32)],
    )
    return pl.pallas_call(
        functools.partial(_paged_attn_kernel, page=page, past_len=n_pages * page, lam_init=lam_init),
        grid_spec=grid_spec,
        out_shape=jax.ShapeDtypeStruct((db, 1, SEG), BF16),
        compiler_params=_cparams("parallel", "arbitrary"),
        name="paged_diff_attn",
    )(page_table.reshape(-1), lw, r3(cq), r3(ck), r3(cv), hn,
      *([cache_kt] * npg), *([cache_v] * npg))


def kernel(x_prompt, x_sample, mem_prompt, cache_diff_k, cache_diff_v, page_table, cache_mem_k, cache_mem_v, state_ret, norm_mix, w_in, ret_norm, cmlp_norm, cmlp_ws, cmlp_bs, diff_qn, diff_kn, diff_lambda_w, diff_hn, mem_norm, w_mem_kv, mem_qn, mem_kn, w_branch, w_out, norm_ffn, w_ffn_in, w_ffn_out):
    batch, seq, _ = x_prompt.shape
    db, dec_seq, _ = x_sample.shape
    assert dec_seq == 1, "the sample group decodes one token per sequence"
    depth = w_in.shape[0]
    n_mem = mem_prompt.shape[1]
    n_phys, page = cache_diff_k.shape[1:3]
    past_len = page_table.shape[1] * page
    n_main = N_SEG * SEG

    w9 = wgl = w_in.astype(BF16)
    wmem = w_mem_kv.astype(BF16)
    wb = w_branch.astype(BF16)
    wo = w_out.astype(BF16)
    wfi = w_ffn_in.astype(BF16)
    wfo = w_ffn_out.astype(BF16)
    bst = cmlp_bs.transpose(0, 2, 1)
    tile = lambda a, n: jnp.tile(a, (1, n)).reshape(depth, 1, SEG)
    qn, kn = tile(diff_qn, SEG // C_DQK), tile(diff_kn, SEG // C_DQK)
    mqn, mkn = tile(mem_qn, HEADS), tile(mem_kn, HEADS)
    row = lambda a: a.reshape(depth, 1, -1)
    g_mix, g_ffn, g_mem = row(norm_mix), row(norm_ffn), row(mem_norm)
    g_ret, g_cm, g_hn = row(ret_norm), row(cmlp_norm), row(diff_hn)

    cache_kt = cache_diff_k.transpose(0, 1, 3, 4, 5, 2).reshape(depth, n_phys, SEG, page)
    cache_v = cache_diff_v.reshape(depth, n_phys, page * HEADS, HEAD_W)
    mem_k_s = cache_mem_k.reshape(depth, db, n_mem * HEADS, M_DH)
    mem_v_s = cache_mem_v.reshape(depth, db, n_mem * HEADS, M_DH)
    state_s = state_ret.reshape(depth, db, HEADS * R_DK, HEAD_W)

    xp = x_prompt.reshape(batch * seq, D_MODEL)
    xs = x_sample.reshape(db, D_MODEL)
    mem = mem_prompt.reshape(batch * n_mem, D_MODEL)

    ckt_all = jnp.zeros((depth, batch, SEG, seq), F32)
    cv4_all = jnp.zeros((depth, batch * seq * HEADS, HEAD_W), F32)
    mk_all = jnp.zeros((depth, batch * n_mem * HEADS, M_DH), F32)
    mv_all = jnp.zeros((depth, batch * n_mem * HEADS, M_DH), F32)

    pst = []
    sk, sv, sst, scv = [], [], [], []
    for l in range(depth):
        lam_init = 0.8 - 0.6 * math.exp(-0.3 * l)
        lw = diff_lambda_w

        rqk, rv, rg, o_cm, cq, ckt_all, ckb, cv4_all, cvb, mq = _proj_in_prompt(
            xp, g_mix, w9, g_cm, cmlp_ws, bst, qn, kn, mqn, ckt_all, cv4_all, layer=l, seq=seq)
        o_ret, st_p = _retention_prompt(l, rqk, rv, rg, g_ret, batch, seq)
        o_df = _diff_attn_prompt(l, lw, cq, ckb, cvb, g_hn, batch, seq, lam_init)
        mk_all, mv_all = _mem_kv(mem, g_mem, wmem, mkn, mk_all, mv_all, layer=l)
        o_mm = _mem_attn_prompt(l, mq, mk_all, mv_all, batch, seq, n_mem)
        xp = _merge(l, xp, g_mix, (o_ret, o_cm, o_df, o_mm), wgl, wb, wo)
        xp = _ffn(l, xp, g_ffn, wfi, wfo)
        pst.append(st_p.reshape(batch, HEADS, R_DK, HEAD_W))

        rqk, rv, rg, o_cm, vn_s, cq, ck, cv, mq = _proj_in_sample(
            xs, g_mix, w9, g_cm, cmlp_ws, bst, qn, kn, mqn, layer=l, off=past_len % CHUNK)
        o_ret, o_mm, st_s = _sample_mix(l, rqk, rv, rg, mq, state_s, mem_k_s, mem_v_s, g_ret, n_mem)
        o_df = _paged_attn(l, page_table, lw, cq, ck, cv, g_hn, cache_kt, cache_v, lam_init)
        xs = _merge(l, xs, g_mix, (o_ret.reshape(db, SEG), o_cm, o_df.reshape(db, SEG), o_mm.reshape(db, SEG)),
                    wgl, wb, wo)
        xs = _ffn(l, xs, g_ffn, wfi, wfo)
        sk.append(ck.reshape(db, 1, HEADS, 2, C_DQK))
        sv.append(cv.reshape(db, 1, HEADS, HEAD_W))
        sst.append(st_s.reshape(db, HEADS, R_DK, HEAD_W))
        scv.append(vn_s.reshape(db, 1, SEG))

    new_k = ckt_all.reshape(depth, batch, HEADS, 2, C_DQK, seq).transpose(0, 1, 5, 2, 3, 4)
    new_v = cv4_all.reshape(depth, batch, seq, HEADS, HEAD_W)
    new_mk = mk_all.reshape(depth, batch, n_mem, HEADS, M_DH)
    new_mv = mv_all.reshape(depth, batch, n_mem, HEADS, M_DH)
    return (xp.reshape(batch, seq, D_MODEL), xs.reshape(db, 1, D_MODEL),
            new_k, new_v, new_mk, new_mv, jnp.stack(pst),
            jnp.stack(sk), jnp.stack(sv), jnp.stack(sst), jnp.stack(scv))
```

```python
import functools
import math

import jax
import jax.numpy as jnp
from jax import lax
from jax.experimental import pallas as pl
from jax.experimental.pallas import tpu as pltpu

F32 = jnp.float32
BF16 = jnp.bfloat16

EPS = 1e-6
NEG_INF = -1e30
LOG2E = math.log2(math.e)

D_MODEL = 1024
SEG = 512
N_SEG = 9
HEADS = 4
R_DK = 64
HEAD_W = 128
CHUNK = 128
C_DQK = 64
M_DH = 128
N_BRANCH = 4
D_FF = 2816
FFN_CHUNKS = ((0, 1024), (1024, 2048), (2048, 2816))

VMEM_LIMIT_BYTES = 56 * 1024 * 1024


def _cparams(*sem):
    return pltpu.CompilerParams(dimension_semantics=sem, vmem_limit_bytes=VMEM_LIMIT_BYTES)


def _dot(a, b):
    return jnp.dot(a, b, preferred_element_type=F32)


def _dot_nt(a, b):
    return lax.dot_general(a, b, (((1,), (1,)), ((), ())), preferred_element_type=F32)


def _dot_tn(a, b):
    return lax.dot_general(a, b, (((0,), (0,)), ((), ())), preferred_element_type=F32)


def _sigmoid(x):
    return 1.0 / (1.0 + jnp.exp(-x))


def _rms(x, g):
    return x * lax.rsqrt(jnp.mean(x * x, axis=-1, keepdims=True) + EPS) * g


def _layer_norm(x, g):
    xc = x - jnp.mean(x, axis=-1, keepdims=True)
    return xc * lax.rsqrt(jnp.mean(xc * xc, axis=-1, keepdims=True) + EPS) * g


def _head_rms(y, g, group):
    lane = lax.broadcasted_iota(jnp.int32, (1, HEAD_W), 1)
    outs = []
    for hb in range(SEG // HEAD_W):
        blk = y[:, hb * HEAD_W:(hb + 1) * HEAD_W]
        sq = blk * blk
        if group == HEAD_W:
            ms = jnp.mean(sq, axis=-1, keepdims=True)
        else:
            lo = jnp.sum(jnp.where(lane < group, sq, 0.0), axis=-1, keepdims=True)
            hi = jnp.sum(jnp.where(lane < group, 0.0, sq), axis=-1, keepdims=True)
            ms = jnp.where(lane < group, lo, hi) * (1.0 / group)
        outs.append(blk * lax.rsqrt(ms + EPS))
    return jnp.concatenate(outs, axis=-1) * g


def _diff_lambda(wl, lam_init):
    a = jnp.sum(wl[0:1] * wl[1:2], axis=-1, keepdims=True)
    b = jnp.sum(wl[2:3] * wl[3:4], axis=-1, keepdims=True)
    return jnp.exp(a) - jnp.exp(b) + lam_init


def _layer_spec(layer, shape):
    nd = len(shape)
    return pl.BlockSpec((None,) + tuple(shape), lambda *_: (layer,) + (0,) * nd,
                        pipeline_mode=pl.Buffered(1))


def _proj_kernel(x_ref, g_ref, w_ref, cn_ref, ws_ref, bst_ref, qn_ref, kn_ref, mqn_ref, *rest,
                 prompt, tm, off):
    if prompt:
        rqk_o, rv_o, rg_o, ocm_o, cq_o, ckt_o, ckb_o, cv4_o, cvb_o, mq_o = rest[2:]
    else:
        rqk_o, rv_o, rg_o, ocm_o, vn_o, cq_o, ck_o, cv_o, mq_o = rest
    hb = _rms(x_ref[...], g_ref[...]).astype(BF16)

    def seg(s):
        return _dot(hb, w_ref[:, s * SEG:(s + 1) * SEG])

    lane = lax.broadcasted_iota(jnp.int32, (1, SEG), 1)
    y = seg(0) * jnp.where(lane < HEADS * R_DK, 1.0, R_DK ** -0.5)
    rqk_o[...] = y.astype(rqk_o.dtype)
    rv_o[...] = seg(1)
    rg_o[...] = seg(2)

    gu = seg(3)
    vn = _layer_norm(seg(4), cn_ref[...])
    if prompt:
        vnb = vn.astype(BF16)
        row = lax.broadcasted_iota(jnp.int32, (CHUNK, CHUNK), 0)
        col = lax.broadcasted_iota(jnp.int32, (CHUNK, CHUNK), 1)
        for g in range(HEADS):
            gs = slice(g * HEAD_W, (g + 1) * HEAD_W)
            wt = jnp.where(row >= col, ws_ref[g], 0.0).astype(BF16)
            bcol = bst_ref[:, g:g + 1]
            for c in range(tm // CHUNK):
                cs = slice(c * CHUNK, (c + 1) * CHUNK)
                mix = _dot(wt, vnb[cs, gs]) + bcol
                ocm_o[cs, gs] = (gu[cs, gs] * mix).astype(BF16)
    else:
        vn_o[...] = vn
        for g in range(HEADS):
            gs = slice(g * HEAD_W, (g + 1) * HEAD_W)
            w00 = ws_ref[g][off:off + 1, off:off + 1]
            b0 = bst_ref[off:off + 1, g:g + 1]
            ocm_o[:, gs] = (gu[:, gs] * (w00 * vn[:, gs] + b0)).astype(BF16)

    cq_o[...] = (_head_rms(seg(5), qn_ref[...], C_DQK) * (C_DQK ** -0.5 * LOG2E)).astype(BF16)
    ck = _head_rms(seg(6), kn_ref[...], C_DQK)
    cv = seg(7)
    if prompt:
        ckt_o[...] = ck.T
        for h in range(HEADS):
            cv4_o[pl.ds(h, tm, stride=HEADS), :] = cv[:, h * HEAD_W:(h + 1) * HEAD_W]
        ckb_o[...] = ck.astype(BF16)
        cvb_o[...] = cv.astype(BF16)
    else:
        ck_o[...] = ck
        cv_o[...] = cv
    mq_o[...] = (_head_rms(seg(8), mqn_ref[...], M_DH) * (M_DH ** -0.5)).astype(BF16)


def _proj_in_specs(layer, tm):
    row = lambda i: (i, 0)
    return [pl.BlockSpec((tm, D_MODEL), row), _layer_spec(layer, (1, D_MODEL)),
            _layer_spec(layer, (D_MODEL, N_SEG * SEG)), _layer_spec(layer, (1, SEG)),
            _layer_spec(layer, (HEADS, CHUNK, CHUNK)), _layer_spec(layer, (CHUNK, HEADS)),
            _layer_spec(layer, (1, SEG)), _layer_spec(layer, (1, SEG)), _layer_spec(layer, (1, SEG))]


def _proj_in_sample(x, g, w9, cn, ws, bst, qn, kn, mqn, *, layer, off):
    m = x.shape[0]
    blk = pl.BlockSpec((m, SEG), lambda i: (i, 0))
    dts = (F32, F32, F32, BF16, F32, BF16, F32, F32, BF16)
    return pl.pallas_call(
        functools.partial(_proj_kernel, prompt=False, tm=m, off=off),
        grid=(1,),
        in_specs=_proj_in_specs(layer, m),
        out_specs=[blk] * len(dts),
        out_shape=[jax.ShapeDtypeStruct((m, SEG), dt) for dt in dts],
        compiler_params=_cparams("parallel"),
        name="proj_in_sample",
    )(x, g, w9, cn, ws, bst, qn, kn, mqn)


def _proj_in_prompt(x, g, w9, cn, ws, bst, qn, kn, mqn, ckt_all, cv4_all, *, layer, seq):
    m = x.shape[0]
    tm = 512
    nq = seq // tm
    blk = pl.BlockSpec((tm, SEG), lambda i: (i, 0))
    cktblk = pl.BlockSpec((None, None, SEG, tm), lambda i: (layer, i // nq, 0, i % nq))
    cv4blk = pl.BlockSpec((None, tm * HEADS, HEAD_W), lambda i: (layer, i, 0))
    sds = lambda dt: jax.ShapeDtypeStruct((m, SEG), dt)
    in_specs = _proj_in_specs(layer, tm)
    n_in = len(in_specs)
    return pl.pallas_call(
        functools.partial(_proj_kernel, prompt=True, tm=tm, off=0),
        grid=(m // tm,),
        in_specs=in_specs + [pl.BlockSpec(memory_space=pl.ANY)] * 2,
        out_specs=[blk, blk, blk, blk, blk, cktblk, blk, cv4blk, blk, blk],
        out_shape=[sds(BF16), sds(F32), sds(F32), sds(BF16), sds(BF16),
                   jax.ShapeDtypeStruct(ckt_all.shape, F32), sds(BF16),
                   jax.ShapeDtypeStruct(cv4_all.shape, F32), sds(BF16), sds(BF16)],
        input_output_aliases={n_in: 5, n_in + 1: 7},
        compiler_params=_cparams("parallel"),
        name="proj_in_prompt",
    )(x, g, w9, cn, ws, bst, qn, kn, mqn, ckt_all, cv4_all)


RET_BATCH = 8
RET_CHUNK = 256


def _ret_log_decay(h):
    return math.log1p(-(2.0 ** (-5 - h)))


def _ret_finish(o, gate, gn):
    return gate * _sigmoid(gate) * _layer_norm(o, gn)


def _retention_kernel(rqk_ref, rv_ref, rg_ref, gn_ref, o_ref, st_o, st_ref, dec_ref, qd_ref, kd_ref):
    c = pl.program_id(1)

    @pl.when(c == 0)
    def _():
        st_ref[...] = jnp.zeros_like(st_ref)
        i = lax.broadcasted_iota(jnp.int32, (RET_CHUNK, RET_CHUNK), 0).astype(F32)
        j = lax.broadcasted_iota(jnp.int32, (RET_CHUNK, RET_CHUNK), 1).astype(F32)
        t = lax.broadcasted_iota(jnp.int32, (RET_CHUNK, HEAD_W), 0).astype(F32)
        for h in range(HEADS):
            lg = _ret_log_decay(h)
            dec_ref[h] = jnp.where(i >= j, jnp.exp(lg * jnp.maximum(i - j, 0.0)), 0.0)
            qd_ref[h] = jnp.exp(lg * (t + 1.0))
            kd_ref[h] = jnp.exp(lg * (RET_CHUNK - 1.0 - t))

    gn = gn_ref[...]
    lane = lax.broadcasted_iota(jnp.int32, (1, HEADS * R_DK), 1)
    for b in range(RET_BATCH):
        qk = rqk_ref[b]
        q_all = qk[:, :HEADS * R_DK]
        k_all = qk[:, HEADS * R_DK:]
        v_all = rv_ref[b]
        rg = rg_ref[b]
        st = st_ref[b]
        stb = st.astype(BF16)
        zero = jnp.zeros_like(q_all)
        vk_parts = []
        for h in range(HEADS):
            hs = slice(h * HEAD_W, (h + 1) * HEAD_W)
            v = v_all[:, hs]
            qm = jnp.where((lane >= h * R_DK) & (lane < (h + 1) * R_DK), q_all, zero)
            s = _dot_nt(qm, k_all) * dec_ref[h]
            intra = _dot(s.astype(BF16), v.astype(BF16))
            cross = _dot(qm, stb) * qd_ref[h]
            o_ref[b, :, hs] = _ret_finish(intra + cross, rg[:, hs], gn[:, hs]).astype(BF16)
            vk_parts.append((v * kd_ref[h]).astype(BF16))
        kv = _dot_tn(k_all, jnp.concatenate(vk_parts, axis=-1))
        for h in range(HEADS):
            rs = slice(h * R_DK, (h + 1) * R_DK)
            st_ref[b, rs, :] = (st[rs, :] * math.exp(_ret_log_decay(h) * RET_CHUNK)
                                + kv[rs, h * HEAD_W:(h + 1) * HEAD_W])

    @pl.when(c == pl.num_programs(1) - 1)
    def _():
        st_o[...] = st_ref[...]


def _retention_prompt(layer, rqk, rv, rg, gn, batch, seq):
    nc = seq // RET_CHUNK
    blk = pl.BlockSpec((RET_BATCH, RET_CHUNK, SEG), lambda b, c: (b, c, 0))
    stblk = pl.BlockSpec((RET_BATCH, HEADS * R_DK, HEAD_W), lambda b, c: (b, 0, 0))
    tbl = pltpu.VMEM((HEADS, RET_CHUNK, HEAD_W), F32)
    r3 = lambda a: a.reshape(batch, seq, SEG)
    o_ret, st = pl.pallas_call(
        _retention_kernel,
        grid=(batch // RET_BATCH, nc),
        in_specs=[blk, blk, blk, _layer_spec(layer, (1, SEG))],
        out_specs=[blk, stblk],
        out_shape=[jax.ShapeDtypeStruct((batch, seq, SEG), BF16),
                   jax.ShapeDtypeStruct((batch, HEADS * R_DK, HEAD_W), F32)],
        scratch_shapes=[pltpu.VMEM((RET_BATCH, HEADS * R_DK, HEAD_W), F32),
                        pltpu.VMEM((HEADS, RET_CHUNK, RET_CHUNK), F32), tbl, tbl],
        compiler_params=_cparams("parallel", "arbitrary"),
        name="retention_prompt",
    )(r3(rqk), r3(rv), r3(rg), gn)
    return o_ret.reshape(batch * seq, SEG), st


def _diff_finish(o0, o1, lam, hn, lam_init):
    o = o0 - lam * o1
    return _rms(o, hn) * (1.0 - lam_init)


def _alibi_slope_log2(h, shape):
    return jnp.exp2(jnp.full(shape, -8.0 / HEADS, F32) * (h + 1).astype(F32)) * LOG2E


ONES_ROWS = 16


def _diff_attn_kernel(lw_ref, q_ref, k_ref, v_ref, hn_ref, o_ref, vt_ref, acc_ref, s_ref, *, tq, nblk, lam_init):
    h = pl.program_id(1)
    ones = jnp.ones((ONES_ROWS, tq), BF16)
    for t in range(nblk):
        vt_ref[t, :HEAD_W, :] = v_ref[t * tq:(t + 1) * tq, :].astype(F32).T.astype(BF16)
        vt_ref[t, HEAD_W:, :] = ones

    lane = lax.broadcasted_iota(jnp.int32, (1, HEAD_W), 1)
    krow = lax.broadcasted_iota(jnp.int32, (tq, HEAD_W), 0).astype(F32)
    brep = _alibi_slope_log2(h, (tq, HEAD_W)) * krow
    bias = jnp.concatenate([brep] * (tq // HEAD_W), axis=1)
    slope_row = _alibi_slope_log2(h, (1, tq))
    rowi = lax.broadcasted_iota(jnp.int32, (tq, tq), 0)
    coli = lax.broadcasted_iota(jnp.int32, (tq, tq), 1)
    lam = _diff_lambda(lw_ref[...], lam_init)
    hn = hn_ref[...]

    q_maps = {}

    def maps_of(i):
        if i not in q_maps:
            q = q_ref[i * tq:(i + 1) * tq, :]
            zero = jnp.zeros_like(q)
            q_maps[i] = (jnp.where(lane < C_DQK, q, zero), jnp.where(lane < C_DQK, zero, q))
        return q_maps[i]

    def scores(i, j, c, slot):
        s_ref[slot, c] = bias + _dot_nt(k_ref[j * tq:(j + 1) * tq, :], maps_of(i)[c])

    schedule = [(i, j) for i in range(nblk) for j in range(i + 1)]
    scores(0, 0, 0, 0)
    scores(0, 0, 1, 0)
    m = [None, None]
    for n, (i, j) in enumerate(schedule):
        slot = n % 2
        nxt = schedule[n + 1] if n + 1 < len(schedule) else None
        off = slope_row * float((j - i) * tq)
        for c in range(2):
            if nxt is not None:
                scores(nxt[0], nxt[1], c, 1 - slot)
            s = s_ref[slot, c]
            if j == i:
                s = jnp.where(coli >= rowi, s, NEG_INF)
            blk_max = jnp.max(s, axis=0, keepdims=True) + off
            if j == 0:
                m_new = blk_max
                acc_ref[c] = _dot(vt_ref[j], jnp.exp2(s - (m_new - off)).astype(BF16))
            else:
                m_new = jnp.maximum(m[c], blk_max)
                p = jnp.exp2(s - (m_new - off)).astype(BF16)
                acc_ref[c] = jnp.exp2(m[c] - m_new) * acc_ref[c] + _dot(vt_ref[j], p)
            m[c] = m_new
        if j == i:
            o0 = acc_ref[0, :HEAD_W, :] / acc_ref[0, HEAD_W:HEAD_W + 1, :]
            o1 = acc_ref[1, :HEAD_W, :] / acc_ref[1, HEAD_W:HEAD_W + 1, :]
            o = o0 - lam * o1
            y = o * lax.rsqrt(jnp.mean(o * o, axis=0, keepdims=True) + EPS)
            o_ref[i * tq:(i + 1) * tq, :] = (y.T * hn * (1.0 - lam_init)).astype(BF16)


def _diff_attn_prompt(layer, lw, cq, ckb, cvb, hn, batch, seq, lam_init):
    tq = 512
    nq = seq // tq
    blk = pl.BlockSpec((seq, HEAD_W), lambda b, h: (b, h))
    return pl.pallas_call(
        functools.partial(_diff_attn_kernel, tq=tq, nblk=nq, lam_init=lam_init),
        grid=(batch, HEADS),
        in_specs=[_layer_spec(layer, (4, C_DQK)), blk, blk, blk,
                  pl.BlockSpec((None, 1, HEAD_W), lambda b, h: (layer, 0, h))],
        out_specs=blk,
        out_shape=jax.ShapeDtypeStruct((batch * seq, SEG), BF16),
        scratch_shapes=[pltpu.VMEM((nq, HEAD_W + ONES_ROWS, tq), BF16),
                        pltpu.VMEM((2, HEAD_W + ONES_ROWS, tq), F32),
                        pltpu.VMEM((2, 2, tq, tq), F32)],
        compiler_params=_cparams("parallel", "parallel"),
        name="diff_attn_prompt",
    )(lw, cq, ckb, cvb, hn)


def _mem_kv_kernel(x_ref, g_ref, w_ref, kn_ref, k_all, v_all, k_o, v_o, *, tm):
    del k_all, v_all
    hb = _rms(x_ref[...], g_ref[...]).astype(BF16)
    k = _head_rms(_dot(hb, w_ref[:, :SEG]), kn_ref[...], M_DH)
    v = _dot(hb, w_ref[:, SEG:])
    for h in range(HEADS):
        hs = slice(h * HEAD_W, (h + 1) * HEAD_W)
        k_o[pl.ds(h, tm, stride=HEADS), :] = k[:, hs]
        v_o[pl.ds(h, tm, stride=HEADS), :] = v[:, hs]


def _mem_kv(mem, g, w2, kn, mk_all, mv_all, *, layer):
    m = mem.shape[0]
    tm = 512
    row = lambda i: (i, 0)
    blk = pl.BlockSpec((None, tm * HEADS, M_DH), lambda i: (layer, i, 0))
    return pl.pallas_call(
        functools.partial(_mem_kv_kernel, tm=tm),
        grid=(m // tm,),
        in_specs=[pl.BlockSpec((tm, D_MODEL), row), _layer_spec(layer, (1, D_MODEL)),
                  _layer_spec(layer, (D_MODEL, 2 * SEG)), _layer_spec(layer, (1, SEG)),
                  pl.BlockSpec(memory_space=pl.ANY), pl.BlockSpec(memory_space=pl.ANY)],
        out_specs=[blk, blk],
        out_shape=[jax.ShapeDtypeStruct(mk_all.shape, F32)] * 2,
        input_output_aliases={4: 0, 5: 1},
        compiler_params=_cparams("parallel"),
        name="mem_kv",
    )(mem, g, w2, kn, mk_all, mv_all)


def _mem_attend_head(qh, kh, vh):
    s = _dot_nt(qh, kh.astype(BF16))
    p = jnp.exp(s - jnp.max(s, axis=-1, keepdims=True))
    o = _dot(p.astype(BF16), vh.astype(BF16))
    return o / jnp.sum(p, axis=-1, keepdims=True)


def _mem_attn_kernel(q_ref, k_ref, v_ref, o_ref, *, n_mem):
    heads = range(HEADS)
    hsl = [slice(h * HEAD_W, (h + 1) * HEAD_W) for h in heads]
    s = [_dot_nt(q_ref[:, hsl[h]], k_ref[pl.ds(h, n_mem, stride=HEADS), :].astype(BF16)) for h in heads]
    p = [jnp.exp(x - jnp.max(x, axis=-1, keepdims=True)) for x in s]
    o = [_dot(p[h].astype(BF16), v_ref[pl.ds(h, n_mem, stride=HEADS), :].astype(BF16)) for h in heads]
    for h in heads:
        o_ref[:, hsl[h]] = (o[h] / jnp.sum(p[h], axis=-1, keepdims=True)).astype(BF16)


def _mem_attn_prompt(layer, mq, mk_all, mv_all, batch, seq, n_mem):
    tq = 512
    nq = seq // tq
    qblk = pl.BlockSpec((tq, SEG), lambda i: (i, 0))
    kvblk = pl.BlockSpec((None, n_mem * HEADS, M_DH), lambda i: (layer, i // nq, 0))
    return pl.pallas_call(
        functools.partial(_mem_attn_kernel, n_mem=n_mem),
        grid=(batch * nq,),
        in_specs=[qblk, kvblk, kvblk],
        out_specs=qblk,
        out_shape=jax.ShapeDtypeStruct((batch * seq, SEG), BF16),
        compiler_params=_cparams("parallel"),
        name="mem_attn_prompt",
    )(mq, mk_all, mv_all)


GATE_SEGS = N_BRANCH * D_MODEL // SEG


def _merge_kernel(x_ref, g_ref, b0_ref, b1_ref, b2_ref, b3_ref, *rest):
    wg_refs = rest[:GATE_SEGS]
    wb_ref, wo_ref, o_ref = rest[GATE_SEGS:]
    per = GATE_SEGS // N_BRANCH
    x = x_ref[...]
    hb = _rms(x, g_ref[...]).astype(BF16)
    merged = None
    for n, b_ref in enumerate((b0_ref, b1_ref, b2_ref, b3_ref)):
        logits = jnp.concatenate([_dot(hb, wg_refs[n * per + k][...]) for k in range(per)], axis=1)
        gate = _sigmoid(logits)
        term = gate * _dot(b_ref[...], wb_ref[n])
        merged = term if merged is None else merged + term
    o_ref[...] = x + _dot(merged.astype(BF16), wo_ref[...])


def _merge(layer, x, g, branches, w_in, wb, wo):
    m = x.shape[0]
    tm = min(512, m)
    row = lambda i: (i, 0)
    xblk = pl.BlockSpec((tm, D_MODEL), row)
    bblk = pl.BlockSpec((tm, SEG), row)

    def gate_spec(k):
        return pl.BlockSpec((None, D_MODEL, SEG), lambda i: (layer, 0, N_SEG + k), pipeline_mode=pl.Buffered(1))

    return pl.pallas_call(
        _merge_kernel,
        grid=(m // tm,),
        in_specs=[xblk, _layer_spec(layer, (1, D_MODEL)), bblk, bblk, bblk, bblk]
                 + [gate_spec(k) for k in range(GATE_SEGS)]
                 + [_layer_spec(layer, (N_BRANCH, SEG, D_MODEL)), _layer_spec(layer, (D_MODEL, D_MODEL))],
        out_specs=xblk,
        out_shape=jax.ShapeDtypeStruct((m, D_MODEL), F32),
        compiler_params=_cparams("parallel"),
        name="merge",
    )(x, g, *branches, *([w_in] * GATE_SEGS), wb, wo)


def _ffn_kernel(x_ref, g_ref, wi_ref, wo_ref, o_ref):
    x = x_ref[...]
    hb = _rms(x, g_ref[...]).astype(BF16)
    acc = x
    for a, b in FFN_CHUNKS:
        gate = _dot(hb, wi_ref[:, a:b])
        up = _dot(hb, wi_ref[:, D_FF + a:D_FF + b])
        act = (gate * _sigmoid(gate) * up).astype(BF16)
        acc = acc + _dot(act, wo_ref[a:b, :])
    o_ref[...] = acc


def _ffn(layer, x, g, wi, wo):
    m = x.shape[0]
    tm = min(512, m)
    row = lambda i: (i, 0)
    xblk = pl.BlockSpec((tm, D_MODEL), row)
    return pl.pallas_call(
        _ffn_kernel,
        grid=(m // tm,),
        in_specs=[xblk, _layer_spec(layer, (1, D_MODEL)), _layer_spec(layer, (D_MODEL, 2 * D_FF)),
                  _layer_spec(layer, (D_FF, D_MODEL))],
        out_specs=xblk,
        out_shape=jax.ShapeDtypeStruct((m, D_MODEL), F32),
        compiler_params=_cparams("parallel"),
        name="ffn",
    )(x, g, wi, wo)


PAD_ROWS = 16
SAMPLE_BATCH = 4


def _sample_mix_kernel(rqk_ref, rv_ref, rg_ref, mq_ref, st_ref, mk_ref, mv_ref, gn_ref,
                       oret_o, omm_o, st_o, *, n_mem):
    gn = gn_ref[...]
    lane = lax.broadcasted_iota(jnp.int32, (1, HEADS * R_DK), 1)

    def first_row(a):
        r = lax.broadcasted_iota(jnp.int32, (PAD_ROWS, a.shape[1]), 0)
        return jnp.where(r == 0, jnp.broadcast_to(a, (PAD_ROWS, a.shape[1])), 0.0).astype(BF16)

    r16 = lax.broadcasted_iota(jnp.int32, (PAD_ROWS, HEADS * R_DK), 0)
    l16 = lax.broadcasted_iota(jnp.int32, (PAD_ROWS, HEADS * R_DK), 1)
    head_rows = (l16 >= r16 * R_DK) & (l16 < (r16 + 1) * R_DK)
    seqs = range(SAMPLE_BATCH)
    heads = range(HEADS)
    hsl = [slice(h * HEAD_W, (h + 1) * HEAD_W) for h in heads]

    q = [rqk_ref[b][:, :HEADS * R_DK] for b in seqs]
    k = [rqk_ref[b][:, HEADS * R_DK:] for b in seqs]
    v = [rv_ref[b] for b in seqs]
    st = [st_ref[b] for b in seqs]
    kv = [_dot_tn(first_row(k[b]), first_row(v[b])) for b in seqs]
    q4 = [jnp.where(head_rows, jnp.broadcast_to(q[b], (PAD_ROWS, HEADS * R_DK)), 0.0).astype(BF16) for b in seqs]
    cross = [_dot(q4[b], st[b].astype(BF16)) for b in seqs]
    mem_s = [[_dot_nt(jnp.broadcast_to(mq_ref[b][:, hsl[h]], (PAD_ROWS, HEAD_W)),
                      mk_ref[b, pl.ds(h, n_mem, stride=HEADS), :].astype(BF16)) for h in heads] for b in seqs]
    mem_p = [[jnp.exp(s - jnp.max(s, axis=-1, keepdims=True)) for s in row] for row in mem_s]
    mem_o = [[_dot(mem_p[b][h].astype(BF16), mv_ref[b, pl.ds(h, n_mem, stride=HEADS), :].astype(BF16))
              for h in heads] for b in seqs]

    for b in seqs:
        rg = rg_ref[b]
        for h in heads:
            rs = slice(h * R_DK, (h + 1) * R_DK)
            gamma = math.exp(_ret_log_decay(h))
            qm = jnp.where((lane >= h * R_DK) & (lane < (h + 1) * R_DK), q[b], 0.0)
            score = jnp.sum(qm * k[b], axis=-1, keepdims=True)
            o = score * v[b][:, hsl[h]] + cross[b][h:h + 1] * gamma
            oret_o[b, :, hsl[h]] = _ret_finish(o, rg[:, hsl[h]], gn[:, hsl[h]]).astype(BF16)
            st_o[b, rs, :] = st[b][rs, :] * gamma + kv[b][rs, hsl[h]]
            o_mem = mem_o[b][h] / jnp.sum(mem_p[b][h], axis=-1, keepdims=True)
            omm_o[b, :, hsl[h]] = o_mem[0:1].astype(BF16)


def _sample_mix(layer, rqk, rv, rg, mq, state, mem_k, mem_v, gn, n_mem):
    db = rqk.shape[0]
    sb = SAMPLE_BATCH
    row = pl.BlockSpec((sb, 1, SEG), lambda b: (b, 0, 0))
    stblk = pl.BlockSpec((None, sb, HEADS * R_DK, HEAD_W), lambda b: (layer, b, 0, 0))
    memblk = pl.BlockSpec((None, sb, n_mem * HEADS, M_DH), lambda b: (layer, b, 0, 0))
    r3 = lambda a: a.reshape(db, 1, SEG)
    return pl.pallas_call(
        functools.partial(_sample_mix_kernel, n_mem=n_mem),
        grid=(db // sb,),
        in_specs=[row, row, row, row, stblk, memblk, memblk, _layer_spec(layer, (1, SEG))],
        out_specs=[row, row, pl.BlockSpec((sb, HEADS * R_DK, HEAD_W), lambda b: (b, 0, 0))],
        out_shape=[jax.ShapeDtypeStruct((db, 1, SEG), BF16), jax.ShapeDtypeStruct((db, 1, SEG), BF16),
                   jax.ShapeDtypeStruct((db, HEADS * R_DK, HEAD_W), F32)],
        compiler_params=_cparams("parallel"),
        name="sample_mix",
    )(r3(rqk), r3(rv), r3(rg), r3(mq), state, mem_k, mem_v, gn)


N_MAPS = 2 * HEADS
PAGES_PER_STEP = 32


def _paged_attn_kernel(pt_ref, lw_ref, q_ref, kn_ref, vn_ref, hn_ref, *rest, page, past_len, lam_init):
    del pt_ref
    npg = PAGES_PER_STEP
    k_refs = rest[:npg]
    v_refs = rest[npg:2 * npg]
    o_ref, m_ref, l_ref, acc_ref = rest[2 * npg:]
    s_idx = pl.program_id(1)

    @pl.when(s_idx == 0)
    def _():
        m_ref[...] = jnp.full_like(m_ref, NEG_INF)
        l_ref[...] = jnp.zeros_like(l_ref)
        acc_ref[...] = jnp.zeros_like(acc_ref)

    r8 = lax.broadcasted_iota(jnp.int32, (N_MAPS, SEG), 0)
    l8 = lax.broadcasted_iota(jnp.int32, (N_MAPS, SEG), 1)
    sel = (l8 >= r8 * C_DQK) & (l8 < (r8 + 1) * C_DQK)
    q8f = jnp.where(sel, jnp.broadcast_to(q_ref[...].astype(F32), (N_MAPS, SEG)), 0.0)
    q8 = q8f.astype(BF16)
    hrow = jnp.right_shift(lax.broadcasted_iota(jnp.int32, (N_MAPS, 1), 0), 1)
    hrow_w = jnp.right_shift(lax.broadcasted_iota(jnp.int32, (N_MAPS, HEAD_W), 0), 1)
    slope = jnp.exp2((-8.0 / HEADS) * (hrow + 1).astype(F32)) * LOG2E
    kpos = s_idx * (npg * page) + lax.broadcasted_iota(jnp.int32, (1, npg * page), 1)
    bias = slope * (kpos - past_len).astype(F32)
    kt = jnp.concatenate([k_refs[p][...].astype(BF16) for p in range(npg)], axis=1)
    s = _dot(q8, kt) + bias
    m_prev = m_ref[...]
    m_new = jnp.maximum(m_prev, jnp.max(s, axis=-1, keepdims=True))
    alpha = jnp.exp2(m_prev - m_new)
    pr = jnp.exp2(s - m_new)
    l_ref[...] = alpha * l_ref[...] + jnp.sum(pr, axis=-1, keepdims=True)
    acc = alpha * acc_ref[...]
    prb = pr.astype(BF16)
    for h in range(HEADS):
        vh = jnp.concatenate([v_refs[p][pl.ds(h, page, stride=HEADS), :].astype(BF16) for p in range(npg)], axis=0)
        acc = acc + jnp.where(hrow_w == h, _dot(prb, vh), 0.0)
    acc_ref[...] = acc
    m_ref[...] = m_new

    @pl.when(s_idx == pl.num_programs(1) - 1)
    def _():
        s_new = jnp.sum(q8f * kn_ref[...], axis=-1, keepdims=True)
        m_fin = jnp.maximum(m_ref[...], s_new)
        a = jnp.exp2(m_ref[...] - m_fin)
        p_new = jnp.exp2(s_new - m_fin)
        l_fin = a * l_ref[...] + p_new
        vn = vn_ref[...]
        vn8 = jnp.zeros((N_MAPS, HEAD_W), F32)
        for h in range(HEADS):
            vn8 = jnp.where(hrow_w == h, jnp.broadcast_to(vn[:, h * HEAD_W:(h + 1) * HEAD_W], (N_MAPS, HEAD_W)), vn8)
        o_all = (a * acc_ref[...] + p_new * vn8) / l_fin
        lam = _diff_lambda(lw_ref[...], lam_init)
        hn = hn_ref[...]
        for h in range(HEADS):
            hs = slice(h * HEAD_W, (h + 1) * HEAD_W)
            o = _diff_finish(o_all[2 * h:2 * h + 1], o_all[2 * h + 1:2 * h + 2], lam, hn[:, hs], lam_init)
            o_ref[:, hs] = o.astype(BF16)


def _paged_attn(layer, page_table, lw, cq, ck, cv, hn, cache_kt, cache_v, lam_init):
    db, n_pages = page_table.shape
    page = cache_kt.shape[3]
    npg = PAGES_PER_STEP
    row = pl.BlockSpec((None, 1, SEG), lambda b, s, pt: (b, 0, 0))

    def page_spec(p):
        return pl.BlockSpec((None, None, SEG, page),
                            lambda b, s, pt: (layer, pt[b * n_pages + s * npg + p], 0, 0))

    r3 = lambda a: a.reshape(db, 1, SEG)
    grid_spec = pltpu.PrefetchScalarGridSpec(
        num_scalar_prefetch=1,
        grid=(db, n_pages // npg),
        in_specs=[pl.BlockSpec((None, 4, C_DQK), lambda b, s, pt: (layer, 0, 0)), row, row, row,
                  pl.BlockSpec((None, 1, SEG), lambda b, s, pt: (layer, 0, 0))]
                 + [page_spec(p) for p in range(npg)] * 2,
        out_specs=row,
        scratch_shapes=[pltpu.VMEM((N_MAPS, 1), F32), pltpu.VMEM((N_MAPS, 1), F32),
                        pltpu.VMEM((N_MAPS, HEAD_W), F32)],
    )
    return pl.pallas_call(
        functools.partial(_paged_attn_kernel, page=page, past_len=n_pages * page, lam_init=lam_init),
        grid_spec=grid_spec,
        out_shape=jax.ShapeDtypeStruct((db, 1, SEG), BF16),
        compiler_params=_cparams("parallel", "arbitrary"),
        name="paged_diff_attn",
    )(page_table.reshape(-1), lw, r3(cq), r3(ck), r3(cv), hn,
      *([cache_kt] * npg), *([cache_v] * npg))


def kernel(x_prompt, x_sample, mem_prompt, cache_diff_k, cache_diff_v, page_table, cache_mem_k, cache_mem_v, state_ret, norm_mix, w_in, ret_norm, cmlp_norm, cmlp_ws, cmlp_bs, diff_qn, diff_kn, diff_lambda_w, diff_hn, mem_norm, w_mem_kv, mem_qn, mem_kn, w_branch, w_out, norm_ffn, w_ffn_in, w_ffn_out):
    batch, seq, _ = x_prompt.shape
    db, dec_seq, _ = x_sample.shape
    assert dec_seq == 1, "the sample group decodes one token per sequence"
    depth = w_in.shape[0]
    n_mem = mem_prompt.shape[1]
    n_phys, page = cache_diff_k.shape[1:3]
    past_len = page_table.shape[1] * page
    n_main = N_SEG * SEG

    w9 = wgl = w_in.astype(BF16)
    wmem = w_mem_kv.astype(BF16)
    wb = w_branch.astype(BF16)
    wo = w_out.astype(BF16)
    wfi = w_ffn_in.astype(BF16)
    wfo = w_ffn_out.astype(BF16)
    bst = cmlp_bs.transpose(0, 2, 1)
    tile = lambda a, n: jnp.tile(a, (1, n)).reshape(depth, 1, SEG)
    qn, kn = tile(diff_qn, SEG // C_DQK), tile(diff_kn, SEG // C_DQK)
    mqn, mkn = tile(mem_qn, HEADS), tile(mem_kn, HEADS)
    row = lambda a: a.reshape(depth, 1, -1)
    g_mix, g_ffn, g_mem = row(norm_mix), row(norm_ffn), row(mem_norm)
    g_ret, g_cm, g_hn = row(ret_norm), row(cmlp_norm), row(diff_hn)

    cache_kt = cache_diff_k.transpose(0, 1, 3, 4, 5, 2).reshape(depth, n_phys, SEG, page)
    cache_v = cache_diff_v.reshape(depth, n_phys, page * HEADS, HEAD_W)
    mem_k_s = cache_mem_k.reshape(depth, db, n_mem * HEADS, M_DH)
    mem_v_s = cache_mem_v.reshape(depth, db, n_mem * HEADS, M_DH)
    state_s = state_ret.reshape(depth, db, HEADS * R_DK, HEAD_W)

    xp = x_prompt.reshape(batch * seq, D_MODEL)
    xs = x_sample.reshape(db, D_MODEL)
    mem = mem_prompt.reshape(batch * n_mem, D_MODEL)

    ckt_all = jnp.zeros((depth, batch, SEG, seq), F32)
    cv4_all = jnp.zeros((depth, batch * seq * HEADS, HEAD_W), F32)
    mk_all = jnp.zeros((depth, batch * n_mem * HEADS, M_DH), F32)
    mv_all = jnp.zeros((depth, batch * n_mem * HEADS, M_DH), F32)

    pst = []
    sk, sv, sst, scv = [], [], [], []
    for l in range(depth):
        lam_init = 0.8 - 0.6 * math.exp(-0.3 * l)
        lw = diff_lambda_w

        rqk, rv, rg, o_cm, cq, ckt_all, ckb, cv4_all, cvb, mq = _proj_in_prompt(
            xp, g_mix, w9, g_cm, cmlp_ws, bst, qn, kn, mqn, ckt_all, cv4_all, layer=l, seq=seq)
        o_ret, st_p = _retention_prompt(l, rqk, rv, rg, g_ret, batch, seq)
        o_df = _diff_attn_prompt(l, lw, cq, ckb, cvb, g_hn, batch, seq, lam_init)
        mk_all, mv_all = _mem_kv(mem, g_mem, wmem, mkn, mk_all, mv_all, layer=l)
        o_mm = _mem_attn_prompt(l, mq, mk_all, mv_all, batch, seq, n_mem)
        xp = _merge(l, xp, g_mix, (o_ret, o_cm, o_df, o_mm), wgl, wb, wo)
        xp = _ffn(l, xp, g_ffn, wfi, wfo)
        pst.append(st_p.reshape(batch, HEADS, R_DK, HEAD_W))

        rqk, rv, rg, o_cm, vn_s, cq, ck, cv, mq = _proj_in_sample(
            xs, g_mix, w9, g_cm, cmlp_ws, bst, qn, kn, mqn, layer=l, off=past_len % CHUNK)
        o_ret, o_mm, st_s = _sample_mix(l, rqk, rv, rg, mq, state_s, mem_k_s, mem_v_s, g_ret, n_mem)
        o_df = _paged_attn(l, page_table, lw, cq, ck, cv, g_hn, cache_kt, cache_v, lam_init)
        xs = _merge(l, xs, g_mix, (o_ret.reshape(db, SEG), o_cm, o_df.reshape(db, SEG), o_mm.reshape(db, SEG)),
                    wgl, wb, wo)
        xs = _ffn(l, xs, g_ffn, wfi, wfo)
        sk.append(ck.reshape(db, 1, HEADS, 2, C_DQK))
        sv.append(cv.reshape(db, 1, HEADS, HEAD_W))
        sst.append(st_s.reshape(db, HEADS, R_DK, HEAD_W))
        scv.append(vn_s.reshape(db, 1, SEG))

    new_k = ckt_all.reshape(depth, batch, HEADS, 2, C_DQK, seq).transpose(0, 1, 5, 2, 3, 4)
    new_v = cv4_all.reshape(depth, batch, seq, HEADS, HEAD_W)
    new_mk = mk_all.reshape(depth, batch, n_mem, HEADS, M_DH)
    new_mv = mv_all.reshape(depth, batch, n_mem, HEADS, M_DH)
    return (xp.reshape(batch, seq, D_MODEL), xs.reshape(db, 1, D_MODEL),
            new_k, new_v, new_mk, new_mv, jnp.stack(pst),
            jnp.stack(sk), jnp.stack(sv), jnp.stack(sst), jnp.stack(scv))
```

```python
import functools
import math

import jax
import jax.numpy as jnp
from jax import lax
from jax.experimental import pallas as pl
from jax.experimental.pallas import tpu as pltpu

F32 = jnp.float32
BF16 = jnp.bfloat16

EPS = 1e-6
NEG_INF = -1e30
LOG2E = math.log2(math.e)

D_MODEL = 1024
SEG = 512
N_SEG = 9
HEADS = 4
R_DK = 64
HEAD_W = 128
CHUNK = 128
C_DQK = 64
M_DH = 128
N_BRANCH = 4
D_FF = 2816
FFN_CHUNKS = ((0, 1024), (1024, 2048), (2048, 2816))

VMEM_LIMIT_BYTES = 56 * 1024 * 1024


def _cparams(*sem):
    return pltpu.CompilerParams(dimension_semantics=sem, vmem_limit_bytes=VMEM_LIMIT_BYTES)


def _dot(a, b):
    return jnp.dot(a, b, preferred_element_type=F32)


def _dot_nt(a, b):
    return lax.dot_general(a, b, (((1,), (1,)), ((), ())), preferred_element_type=F32)


def _dot_tn(a, b):
    return lax.dot_general(a, b, (((0,), (0,)), ((), ())), preferred_element_type=F32)


def _sigmoid(x):
    return 1.0 / (1.0 + jnp.exp(-x))


def _rms(x, g):
    return x * lax.rsqrt(jnp.mean(x * x, axis=-1, keepdims=True) + EPS) * g


def _layer_norm(x, g):
    xc = x - jnp.mean(x, axis=-1, keepdims=True)
    return xc * lax.rsqrt(jnp.mean(xc * xc, axis=-1, keepdims=True) + EPS) * g


def _head_rms(y, g, group):
    lane = lax.broadcasted_iota(jnp.int32, (1, HEAD_W), 1)
    outs = []
    for hb in range(SEG // HEAD_W):
        blk = y[:, hb * HEAD_W:(hb + 1) * HEAD_W]
        sq = blk * blk
        if group == HEAD_W:
            ms = jnp.mean(sq, axis=-1, keepdims=True)
        else:
            lo = jnp.sum(jnp.where(lane < group, sq, 0.0), axis=-1, keepdims=True)
            hi = jnp.sum(jnp.where(lane < group, 0.0, sq), axis=-1, keepdims=True)
            ms = jnp.where(lane < group, lo, hi) * (1.0 / group)
        outs.append(blk * lax.rsqrt(ms + EPS))
    return jnp.concatenate(outs, axis=-1) * g


def _diff_lambda(wl, lam_init):
    a = jnp.sum(wl[0:1] * wl[1:2], axis=-1, keepdims=True)
    b = jnp.sum(wl[2:3] * wl[3:4], axis=-1, keepdims=True)
    return jnp.exp(a) - jnp.exp(b) + lam_init


def _layer_spec(layer, shape):
    nd = len(shape)
    return pl.BlockSpec((None,) + tuple(shape), lambda *_: (layer,) + (0,) * nd,
                        pipeline_mode=pl.Buffered(1))


def _proj_kernel(x_ref, g_ref, w_ref, cn_ref, ws_ref, bst_ref, qn_ref, kn_ref, mqn_ref, *rest,
                 prompt, tm, off, first=False):
    if prompt:
        rqk_o, rv_o, rg_o, ocm_o, cq_o, ckt_o, ckb_o, cv4_o, cvb_o, mq_o = rest if first else rest[2:]
    else:
        rqk_o, rv_o, rg_o, ocm_o, vn_o, cq_o, ck_o, cv_o, mq_o = rest
    hb = _rms(x_ref[...], g_ref[...]).astype(BF16)

    def seg(s):
        return _dot(hb, w_ref[:, s * SEG:(s + 1) * SEG])

    lane = lax.broadcasted_iota(jnp.int32, (1, SEG), 1)
    y = seg(0) * jnp.where(lane < HEADS * R_DK, 1.0, R_DK ** -0.5)
    rqk_o[...] = y.astype(rqk_o.dtype)
    rv_o[...] = seg(1)
    rg_o[...] = seg(2)

    gu = seg(3)
    vn = _layer_norm(seg(4), cn_ref[...])
    if prompt:
        vnb = vn.astype(BF16)
        row = lax.broadcasted_iota(jnp.int32, (CHUNK, CHUNK), 0)
        col = lax.broadcasted_iota(jnp.int32, (CHUNK, CHUNK), 1)
        for g in range(HEADS):
            gs = slice(g * HEAD_W, (g + 1) * HEAD_W)
            wt = jnp.where(row >= col, ws_ref[g], 0.0).astype(BF16)
            bcol = bst_ref[:, g:g + 1]
            for c in range(tm // CHUNK):
                cs = slice(c * CHUNK, (c + 1) * CHUNK)
                mix = _dot(wt, vnb[cs, gs]) + bcol
                ocm_o[cs, gs] = (gu[cs, gs] * mix).astype(BF16)
    else:
        vn_o[...] = vn
        for g in range(HEADS):
            gs = slice(g * HEAD_W, (g + 1) * HEAD_W)
            w00 = ws_ref[g][off:off + 1, off:off + 1]
            b0 = bst_ref[off:off + 1, g:g + 1]
            ocm_o[:, gs] = (gu[:, gs] * (w00 * vn[:, gs] + b0)).astype(BF16)

    cq_o[...] = (_head_rms(seg(5), qn_ref[...], C_DQK) * (C_DQK ** -0.5 * LOG2E)).astype(BF16)
    ck = _head_rms(seg(6), kn_ref[...], C_DQK)
    cv = seg(7)
    if prompt:
        if first:
            ckt_l, cv4_l = ckt_o.at[0], cv4_o.at[0]
            ckt_o[1:] = jnp.zeros((ckt_o.shape[0] - 1,) + ckt_o.shape[1:], F32)
            cv4_o[1:] = jnp.zeros((cv4_o.shape[0] - 1,) + cv4_o.shape[1:], F32)
        else:
            ckt_l, cv4_l = ckt_o, cv4_o
        ckt_l[...] = ck.T
        for h in range(HEADS):
            cv4_l[pl.ds(h, tm, stride=HEADS), :] = cv[:, h * HEAD_W:(h + 1) * HEAD_W]
        ckb_o[...] = ck.astype(BF16)
        cvb_o[...] = cv.astype(BF16)
    else:
        ck_o[...] = ck
        cv_o[...] = cv
    mq_o[...] = (_head_rms(seg(8), mqn_ref[...], M_DH) * (M_DH ** -0.5)).astype(BF16)


def _proj_in_specs(layer, tm):
    row = lambda i: (i, 0)
    return [pl.BlockSpec((tm, D_MODEL), row), _layer_spec(layer, (1, D_MODEL)),
            _layer_spec(layer, (D_MODEL, N_SEG * SEG)), _layer_spec(layer, (1, SEG)),
            _layer_spec(layer, (HEADS, CHUNK, CHUNK)), _layer_spec(layer, (CHUNK, HEADS)),
            _layer_spec(layer, (1, SEG)), _layer_spec(layer, (1, SEG)), _layer_spec(layer, (1, SEG))]


def _proj_in_sample(x, g, w9, cn, ws, bst, qn, kn, mqn, *, layer, off):
    m = x.shape[0]
    blk = pl.BlockSpec((m, SEG), lambda i: (i, 0))
    dts = (F32, F32, F32, BF16, F32, BF16, F32, F32, BF16)
    return pl.pallas_call(
        functools.partial(_proj_kernel, prompt=False, tm=m, off=off),
        grid=(1,),
        in_specs=_proj_in_specs(layer, m),
        out_specs=[blk] * len(dts),
        out_shape=[jax.ShapeDtypeStruct((m, SEG), dt) for dt in dts],
        compiler_params=_cparams("parallel"),
        name="proj_in_sample",
    )(x, g, w9, cn, ws, bst, qn, kn, mqn)


def _proj_in_prompt(x, g, w9, cn, ws, bst, qn, kn, mqn, ckt_all, cv4_all, *, layer, depth, batch, seq):
    m = x.shape[0]
    tm = 512
    nq = seq // tm
    first = ckt_all is None
    blk = pl.BlockSpec((tm, SEG), lambda i: (i, 0))
    lead, at = ((depth,), 0) if first else ((None,), layer)
    cktblk = pl.BlockSpec(lead + (None, SEG, tm), lambda i: (at, i // nq, 0, i % nq))
    cv4blk = pl.BlockSpec(lead + (tm * HEADS, HEAD_W), lambda i: (at, i, 0))
    sds = lambda dt: jax.ShapeDtypeStruct((m, SEG), dt)
    in_specs = _proj_in_specs(layer, tm)
    n_in = len(in_specs)
    buffers = () if first else (ckt_all, cv4_all)
    return pl.pallas_call(
        functools.partial(_proj_kernel, prompt=True, tm=tm, off=0, first=first),
        grid=(m // tm,),
        in_specs=in_specs + [pl.BlockSpec(memory_space=pl.ANY)] * len(buffers),
        out_specs=[blk, blk, blk, blk, blk, cktblk, blk, cv4blk, blk, blk],
        out_shape=[sds(BF16), sds(F32), sds(F32), sds(BF16), sds(BF16),
                   jax.ShapeDtypeStruct((depth, batch, SEG, seq), F32), sds(BF16),
                   jax.ShapeDtypeStruct((depth, m * HEADS, HEAD_W), F32), sds(BF16), sds(BF16)],
        input_output_aliases={} if first else {n_in: 5, n_in + 1: 7},
        compiler_params=_cparams("parallel"),
        name="proj_in_prompt",
    )(x, g, w9, cn, ws, bst, qn, kn, mqn, *buffers)


RET_BATCH = 8
RET_CHUNK = 256


def _ret_log_decay(h):
    return math.log1p(-(2.0 ** (-5 - h)))


def _ret_finish(o, gate, gn):
    return gate * _sigmoid(gate) * _layer_norm(o, gn)


def _retention_kernel(rqk_ref, rv_ref, rg_ref, gn_ref, o_ref, st_o, st_ref, dec_ref, qd_ref, kd_ref):
    c = pl.program_id(1)

    @pl.when(c == 0)
    def _():
        st_ref[...] = jnp.zeros_like(st_ref)
        i = lax.broadcasted_iota(jnp.int32, (RET_CHUNK, RET_CHUNK), 0).astype(F32)
        j = lax.broadcasted_iota(jnp.int32, (RET_CHUNK, RET_CHUNK), 1).astype(F32)
        t = lax.broadcasted_iota(jnp.int32, (RET_CHUNK, HEAD_W), 0).astype(F32)
        for h in range(HEADS):
            lg = _ret_log_decay(h)
            dec_ref[h] = jnp.where(i >= j, jnp.exp(lg * jnp.maximum(i - j, 0.0)), 0.0)
            qd_ref[h] = jnp.exp(lg * (t + 1.0))
            kd_ref[h] = jnp.exp(lg * (RET_CHUNK - 1.0 - t))

    gn = gn_ref[...]
    lane = lax.broadcasted_iota(jnp.int32, (1, HEADS * R_DK), 1)
    for b in range(RET_BATCH):
        qk = rqk_ref[b]
        q_all = qk[:, :HEADS * R_DK]
        k_all = qk[:, HEADS * R_DK:]
        v_all = rv_ref[b]
        rg = rg_ref[b]
        st = st_ref[b]
        stb = st.astype(BF16)
        zero = jnp.zeros_like(q_all)
        vk_parts = []
        for h in range(HEADS):
            hs = slice(h * HEAD_W, (h + 1) * HEAD_W)
            v = v_all[:, hs]
            qm = jnp.where((lane >= h * R_DK) & (lane < (h + 1) * R_DK), q_all, zero)
            s = _dot_nt(qm, k_all) * dec_ref[h]
            intra = _dot(s.astype(BF16), v.astype(BF16))
            cross = _dot(qm, stb) * qd_ref[h]
            o_ref[b, :, hs] = _ret_finish(intra + cross, rg[:, hs], gn[:, hs]).astype(BF16)
            vk_parts.append((v * kd_ref[h]).astype(BF16))
        kv = _dot_tn(k_all, jnp.concatenate(vk_parts, axis=-1))
        for h in range(HEADS):
            rs = slice(h * R_DK, (h + 1) * R_DK)
            st_ref[b, rs, :] = (st[rs, :] * math.exp(_ret_log_decay(h) * RET_CHUNK)
                                + kv[rs, h * HEAD_W:(h + 1) * HEAD_W])

    @pl.when(c == pl.num_programs(1) - 1)
    def _():
        st_o[...] = st_ref[...]


def _retention_prompt(layer, rqk, rv, rg, gn, batch, seq):
    nc = seq // RET_CHUNK
    blk = pl.BlockSpec((RET_BATCH, RET_CHUNK, SEG), lambda b, c: (b, c, 0))
    stblk = pl.BlockSpec((RET_BATCH, HEADS * R_DK, HEAD_W), lambda b, c: (b, 0, 0))
    tbl = pltpu.VMEM((HEADS, RET_CHUNK, HEAD_W), F32)
    r3 = lambda a: a.reshape(batch, seq, SEG)
    o_ret, st = pl.pallas_call(
        _retention_kernel,
        grid=(batch // RET_BATCH, nc),
        in_specs=[blk, blk, blk, _layer_spec(layer, (1, SEG))],
        out_specs=[blk, stblk],
        out_shape=[jax.ShapeDtypeStruct((batch, seq, SEG), BF16),
                   jax.ShapeDtypeStruct((batch, HEADS * R_DK, HEAD_W), F32)],
        scratch_shapes=[pltpu.VMEM((RET_BATCH, HEADS * R_DK, HEAD_W), F32),
                        pltpu.VMEM((HEADS, RET_CHUNK, RET_CHUNK), F32), tbl, tbl],
        compiler_params=_cparams("parallel", "arbitrary"),
        name="retention_prompt",
    )(r3(rqk), r3(rv), r3(rg), gn)
    return o_ret.reshape(batch * seq, SEG), st


def _diff_finish(o0, o1, lam, hn, lam_init):
    o = o0 - lam * o1
    return _rms(o, hn) * (1.0 - lam_init)


def _alibi_slope_log2(h, shape):
    return jnp.exp2(jnp.full(shape, -8.0 / HEADS, F32) * (h + 1).astype(F32)) * LOG2E


ONES_ROWS = 16


def _diff_attn_kernel(lw_ref, q_ref, k_ref, v_ref, hn_ref, o_ref, vt_ref, acc_ref, s_ref, *, tq, nblk, lam_init):
    h = pl.program_id(1)
    ones = jnp.ones((ONES_ROWS, tq), BF16)
    for t in range(nblk):
        vt_ref[t, :HEAD_W, :] = v_ref[t * tq:(t + 1) * tq, :].astype(F32).T.astype(BF16)
        vt_ref[t, HEAD_W:, :] = ones

    lane = lax.broadcasted_iota(jnp.int32, (1, HEAD_W), 1)
    krow = lax.broadcasted_iota(jnp.int32, (tq, HEAD_W), 0).astype(F32)
    brep = _alibi_slope_log2(h, (tq, HEAD_W)) * krow
    bias = jnp.concatenate([brep] * (tq // HEAD_W), axis=1)
    slope_row = _alibi_slope_log2(h, (1, tq))
    rowi = lax.broadcasted_iota(jnp.int32, (tq, tq), 0)
    coli = lax.broadcasted_iota(jnp.int32, (tq, tq), 1)
    lam = _diff_lambda(lw_ref[...], lam_init)
    hn = hn_ref[...]

    q_maps = {}

    def maps_of(i):
        if i not in q_maps:
            q = q_ref[i * tq:(i + 1) * tq, :]
            zero = jnp.zeros_like(q)
            q_maps[i] = (jnp.where(lane < C_DQK, q, zero), jnp.where(lane < C_DQK, zero, q))
        return q_maps[i]

    def scores(i, j, c, slot):
        s_ref[slot, c] = bias + _dot_nt(k_ref[j * tq:(j + 1) * tq, :], maps_of(i)[c])

    schedule = [(i, j) for i in range(nblk) for j in range(i + 1)]
    scores(0, 0, 0, 0)
    scores(0, 0, 1, 0)
    m = [None, None]
    for n, (i, j) in enumerate(schedule):
        slot = n % 2
        nxt = schedule[n + 1] if n + 1 < len(schedule) else None
        off = slope_row * float((j - i) * tq)
        for c in range(2):
            if nxt is not None:
                scores(nxt[0], nxt[1], c, 1 - slot)
            s = s_ref[slot, c]
            if j == i:
                s = jnp.where(coli >= rowi, s, NEG_INF)
            blk_max = jnp.max(s, axis=0, keepdims=True) + off
            if j == 0:
                m_new = blk_max
                acc_ref[c] = _dot(vt_ref[j], jnp.exp2(s - (m_new - off)).astype(BF16))
            else:
                m_new = jnp.maximum(m[c], blk_max)
                p = jnp.exp2(s - (m_new - off)).astype(BF16)
                acc_ref[c] = jnp.exp2(m[c] - m_new) * acc_ref[c] + _dot(vt_ref[j], p)
            m[c] = m_new
        if j == i:
            o0 = acc_ref[0, :HEAD_W, :] / acc_ref[0, HEAD_W:HEAD_W + 1, :]
            o1 = acc_ref[1, :HEAD_W, :] / acc_ref[1, HEAD_W:HEAD_W + 1, :]
            o = o0 - lam * o1
            y = o * lax.rsqrt(jnp.mean(o * o, axis=0, keepdims=True) + EPS)
            o_ref[i * tq:(i + 1) * tq, :] = (y.T * hn * (1.0 - lam_init)).astype(BF16)


def _diff_attn_prompt(layer, lw, cq, ckb, cvb, hn, batch, seq, lam_init):
    tq = 512
    nq = seq // tq
    blk = pl.BlockSpec((seq, HEAD_W), lambda b, h: (b, h))
    return pl.pallas_call(
        functools.partial(_diff_attn_kernel, tq=tq, nblk=nq, lam_init=lam_init),
        grid=(batch, HEADS),
        in_specs=[_layer_spec(layer, (4, C_DQK)), blk, blk, blk,
                  pl.BlockSpec((None, 1, HEAD_W), lambda b, h: (layer, 0, h))],
        out_specs=blk,
        out_shape=jax.ShapeDtypeStruct((batch * seq, SEG), BF16),
        scratch_shapes=[pltpu.VMEM((nq, HEAD_W + ONES_ROWS, tq), BF16),
                        pltpu.VMEM((2, HEAD_W + ONES_ROWS, tq), F32),
                        pltpu.VMEM((2, 2, tq, tq), F32)],
        compiler_params=_cparams("parallel", "parallel"),
        name="diff_attn_prompt",
    )(lw, cq, ckb, cvb, hn)


def _mem_kv_kernel(x_ref, g_ref, w_ref, kn_ref, *rest, tm, first):
    k_o, v_o = rest[-2:]
    hb = _rms(x_ref[...], g_ref[...]).astype(BF16)
    k = _head_rms(_dot(hb, w_ref[:, :SEG]), kn_ref[...], M_DH)
    v = _dot(hb, w_ref[:, SEG:])
    if first:
        k_l, v_l = k_o.at[0], v_o.at[0]
        k_o[1:] = jnp.zeros((k_o.shape[0] - 1,) + k_o.shape[1:], F32)
        v_o[1:] = jnp.zeros((v_o.shape[0] - 1,) + v_o.shape[1:], F32)
    else:
        k_l, v_l = k_o, v_o
    for h in range(HEADS):
        hs = slice(h * HEAD_W, (h + 1) * HEAD_W)
        k_l[pl.ds(h, tm, stride=HEADS), :] = k[:, hs]
        v_l[pl.ds(h, tm, stride=HEADS), :] = v[:, hs]


def _mem_kv(mem, g, w2, kn, mk_all, mv_all, *, layer, depth):
    m = mem.shape[0]
    tm = 512
    row = lambda i: (i, 0)
    first = mk_all is None
    lead, at = ((depth,), 0) if first else ((None,), layer)
    blk = pl.BlockSpec(lead + (tm * HEADS, M_DH), lambda i: (at, i, 0))
    buffers = () if first else (mk_all, mv_all)
    return pl.pallas_call(
        functools.partial(_mem_kv_kernel, tm=tm, first=first),
        grid=(m // tm,),
        in_specs=[pl.BlockSpec((tm, D_MODEL), row), _layer_spec(layer, (1, D_MODEL)),
                  _layer_spec(layer, (D_MODEL, 2 * SEG)), _layer_spec(layer, (1, SEG))]
                 + [pl.BlockSpec(memory_space=pl.ANY)] * len(buffers),
        out_specs=[blk, blk],
        out_shape=[jax.ShapeDtypeStruct((depth, m * HEADS, M_DH), F32)] * 2,
        input_output_aliases={} if first else {4: 0, 5: 1},
        compiler_params=_cparams("parallel"),
        name="mem_kv",
    )(mem, g, w2, kn, *buffers)


def _mem_attend_head(qh, kh, vh):
    s = _dot_nt(qh, kh.astype(BF16))
    p = jnp.exp(s - jnp.max(s, axis=-1, keepdims=True))
    o = _dot(p.astype(BF16), vh.astype(BF16))
    return o / jnp.sum(p, axis=-1, keepdims=True)


def _mem_attn_kernel(q_ref, k_ref, v_ref, o_ref, *, n_mem):
    heads = range(HEADS)
    hsl = [slice(h * HEAD_W, (h + 1) * HEAD_W) for h in heads]
    s = [_dot_nt(q_ref[:, hsl[h]], k_ref[pl.ds(h, n_mem, stride=HEADS), :].astype(BF16)) for h in heads]
    p = [jnp.exp(x - jnp.max(x, axis=-1, keepdims=True)) for x in s]
    o = [_dot(p[h].astype(BF16), v_ref[pl.ds(h, n_mem, stride=HEADS), :].astype(BF16)) for h in heads]
    for h in heads:
        o_ref[:, hsl[h]] = (o[h] / jnp.sum(p[h], axis=-1, keepdims=True)).astype(BF16)


def _mem_attn_prompt(layer, mq, mk_all, mv_all, batch, seq, n_mem):
    tq = 512
    nq = seq // tq
    qblk = pl.BlockSpec((tq, SEG), lambda i: (i, 0))
    kvblk = pl.BlockSpec((None, n_mem * HEADS, M_DH), lambda i: (layer, i // nq, 0))
    return pl.pallas_call(
        functools.partial(_mem_attn_kernel, n_mem=n_mem),
        grid=(batch * nq,),
        in_specs=[qblk, kvblk, kvblk],
        out_specs=qblk,
        out_shape=jax.ShapeDtypeStruct((batch * seq, SEG), BF16),
        compiler_params=_cparams("parallel"),
        name="mem_attn_prompt",
    )(mq, mk_all, mv_all)


GATE_SEGS = N_BRANCH * D_MODEL // SEG


def _merge_kernel(x_ref, g_ref, b0_ref, b1_ref, b2_ref, b3_ref, *rest):
    wg_refs = rest[:GATE_SEGS]
    wb_ref, wo_ref, o_ref = rest[GATE_SEGS:]
    per = GATE_SEGS // N_BRANCH
    x = x_ref[...]
    hb = _rms(x, g_ref[...]).astype(BF16)
    merged = None
    for n, b_ref in enumerate((b0_ref, b1_ref, b2_ref, b3_ref)):
        logits = jnp.concatenate([_dot(hb, wg_refs[n * per + k][...]) for k in range(per)], axis=1)
        gate = _sigmoid(logits)
        term = gate * _dot(b_ref[...], wb_ref[n])
        merged = term if merged is None else merged + term
    o_ref[...] = x + _dot(merged.astype(BF16), wo_ref[...])


def _merge(layer, x, g, branches, w_in, wb, wo):
    m = x.shape[0]
    tm = min(512, m)
    row = lambda i: (i, 0)
    xblk = pl.BlockSpec((tm, D_MODEL), row)
    bblk = pl.BlockSpec((tm, SEG), row)

    def gate_spec(k):
        return pl.BlockSpec((None, D_MODEL, SEG), lambda i: (layer, 0, N_SEG + k), pipeline_mode=pl.Buffered(1))

    return pl.pallas_call(
        _merge_kernel,
        grid=(m // tm,),
        in_specs=[xblk, _layer_spec(layer, (1, D_MODEL)), bblk, bblk, bblk, bblk]
                 + [gate_spec(k) for k in range(GATE_SEGS)]
                 + [_layer_spec(layer, (N_BRANCH, SEG, D_MODEL)), _layer_spec(layer, (D_MODEL, D_MODEL))],
        out_specs=xblk,
        out_shape=jax.ShapeDtypeStruct((m, D_MODEL), F32),
        compiler_params=_cparams("parallel"),
        name="merge",
    )(x, g, *branches, *([w_in] * GATE_SEGS), wb, wo)


def _ffn_kernel(x_ref, g_ref, wi_ref, wo_ref, o_ref):
    x = x_ref[...]
    hb = _rms(x, g_ref[...]).astype(BF16)
    acc = x
    for a, b in FFN_CHUNKS:
        gate = _dot(hb, wi_ref[:, a:b])
        up = _dot(hb, wi_ref[:, D_FF + a:D_FF + b])
        act = (gate * _sigmoid(gate) * up).astype(BF16)
        acc = acc + _dot(act, wo_ref[a:b, :])
    o_ref[...] = acc


def _ffn(layer, x, g, wi, wo):
    m = x.shape[0]
    tm = min(512, m)
    row = lambda i: (i, 0)
    xblk = pl.BlockSpec((tm, D_MODEL), row)
    return pl.pallas_call(
        _ffn_kernel,
        grid=(m // tm,),
        in_specs=[xblk, _layer_spec(layer, (1, D_MODEL)), _layer_spec(layer, (D_MODEL, 2 * D_FF)),
                  _layer_spec(layer, (D_FF, D_MODEL))],
        out_specs=xblk,
        out_shape=jax.ShapeDtypeStruct((m, D_MODEL), F32),
        compiler_params=_cparams("parallel"),
        name="ffn",
    )(x, g, wi, wo)


PAD_ROWS = 16
SAMPLE_BATCH = 4


def _sample_mix_kernel(rqk_ref, rv_ref, rg_ref, mq_ref, st_ref, mk_ref, mv_ref, gn_ref,
                       oret_o, omm_o, st_o, *, n_mem):
    gn = gn_ref[...]
    lane = lax.broadcasted_iota(jnp.int32, (1, HEADS * R_DK), 1)

    def first_row(a):
        r = lax.broadcasted_iota(jnp.int32, (PAD_ROWS, a.shape[1]), 0)
        return jnp.where(r == 0, jnp.broadcast_to(a, (PAD_ROWS, a.shape[1])), 0.0).astype(BF16)

    r16 = lax.broadcasted_iota(jnp.int32, (PAD_ROWS, HEADS * R_DK), 0)
    l16 = lax.broadcasted_iota(jnp.int32, (PAD_ROWS, HEADS * R_DK), 1)
    head_rows = (l16 >= r16 * R_DK) & (l16 < (r16 + 1) * R_DK)
    seqs = range(SAMPLE_BATCH)
    heads = range(HEADS)
    hsl = [slice(h * HEAD_W, (h + 1) * HEAD_W) for h in heads]

    q = [rqk_ref[b][:, :HEADS * R_DK] for b in seqs]
    k = [rqk_ref[b][:, HEADS * R_DK:] for b in seqs]
    v = [rv_ref[b] for b in seqs]
    st = [st_ref[b] for b in seqs]
    kv = [_dot_tn(first_row(k[b]), first_row(v[b])) for b in seqs]
    q4 = [jnp.where(head_rows, jnp.broadcast_to(q[b], (PAD_ROWS, HEADS * R_DK)), 0.0).astype(BF16) for b in seqs]
    cross = [_dot(q4[b], st[b].astype(BF16)) for b in seqs]
    mem_s = [[_dot_nt(jnp.broadcast_to(mq_ref[b][:, hsl[h]], (PAD_ROWS, HEAD_W)),
                      mk_ref[b, pl.ds(h, n_mem, stride=HEADS), :].astype(BF16)) for h in heads] for b in seqs]
    mem_p = [[jnp.exp(s - jnp.max(s, axis=-1, keepdims=True)) for s in row] for row in mem_s]
    mem_o = [[_dot(mem_p[b][h].astype(BF16), mv_ref[b, pl.ds(h, n_mem, stride=HEADS), :].astype(BF16))
              for h in heads] for b in seqs]

    for b in seqs:
        rg = rg_ref[b]
        for h in heads:
            rs = slice(h * R_DK, (h + 1) * R_DK)
            gamma = math.exp(_ret_log_decay(h))
            qm = jnp.where((lane >= h * R_DK) & (lane < (h + 1) * R_DK), q[b], 0.0)
            score = jnp.sum(qm * k[b], axis=-1, keepdims=True)
            o = score * v[b][:, hsl[h]] + cross[b][h:h + 1] * gamma
            oret_o[b, :, hsl[h]] = _ret_finish(o, rg[:, hsl[h]], gn[:, hsl[h]]).astype(BF16)
            st_o[b, rs, :] = st[b][rs, :] * gamma + kv[b][rs, hsl[h]]
            o_mem = mem_o[b][h] / jnp.sum(mem_p[b][h], axis=-1, keepdims=True)
            omm_o[b, :, hsl[h]] = o_mem[0:1].astype(BF16)


def _sample_mix(layer, rqk, rv, rg, mq, state, mem_k, mem_v, gn, n_mem):
    db = rqk.shape[0]
    sb = SAMPLE_BATCH
    row = pl.BlockSpec((sb, 1, SEG), lambda b: (b, 0, 0))
    stblk = pl.BlockSpec((None, sb, HEADS * R_DK, HEAD_W), lambda b: (layer, b, 0, 0))
    memblk = pl.BlockSpec((None, sb, n_mem * HEADS, M_DH), lambda b: (layer, b, 0, 0))
    r3 = lambda a: a.reshape(db, 1, SEG)
    return pl.pallas_call(
        functools.partial(_sample_mix_kernel, n_mem=n_mem),
        grid=(db // sb,),
        in_specs=[row, row, row, row, stblk, memblk, memblk, _layer_spec(layer, (1, SEG))],
        out_specs=[row, row, pl.BlockSpec((sb, HEADS * R_DK, HEAD_W), lambda b: (b, 0, 0))],
        out_shape=[jax.ShapeDtypeStruct((db, 1, SEG), BF16), jax.ShapeDtypeStruct((db, 1, SEG), BF16),
                   jax.ShapeDtypeStruct((db, HEADS * R_DK, HEAD_W), F32)],
        compiler_params=_cparams("parallel"),
        name="sample_mix",
    )(r3(rqk), r3(rv), r3(rg), r3(mq), state, mem_k, mem_v, gn)


N_MAPS = 2 * HEADS
PAGES_PER_STEP = 32


def _paged_attn_kernel(pt_ref, lw_ref, q_ref, kn_ref, vn_ref, hn_ref, *rest, page, past_len, lam_init):
    del pt_ref
    npg = PAGES_PER_STEP
    k_refs = rest[:npg]
    v_refs = rest[npg:2 * npg]
    o_ref, m_ref, l_ref, acc_ref = rest[2 * npg:]
    s_idx = pl.program_id(1)

    @pl.when(s_idx == 0)
    def _():
        m_ref[...] = jnp.full_like(m_ref, NEG_INF)
        l_ref[...] = jnp.zeros_like(l_ref)
        acc_ref[...] = jnp.zeros_like(acc_ref)

    r8 = lax.broadcasted_iota(jnp.int32, (N_MAPS, SEG), 0)
    l8 = lax.broadcasted_iota(jnp.int32, (N_MAPS, SEG), 1)
    sel = (l8 >= r8 * C_DQK) & (l8 < (r8 + 1) * C_DQK)
    q8f = jnp.where(sel, jnp.broadcast_to(q_ref[...].astype(F32), (N_MAPS, SEG)), 0.0)
    q8 = q8f.astype(BF16)
    hrow = jnp.right_shift(lax.broadcasted_iota(jnp.int32, (N_MAPS, 1), 0), 1)
    hrow_w = jnp.right_shift(lax.broadcasted_iota(jnp.int32, (N_MAPS, HEAD_W), 0), 1)
    slope = jnp.exp2((-8.0 / HEADS) * (hrow + 1).astype(F32)) * LOG2E
    kpos = s_idx * (npg * page) + lax.broadcasted_iota(jnp.int32, (1, npg * page), 1)
    bias = slope * (kpos - past_len).astype(F32)
    kt = jnp.concatenate([k_refs[p][...].astype(BF16) for p in range(npg)], axis=1)
    s = _dot(q8, kt) + bias
    m_prev = m_ref[...]
    m_new = jnp.maximum(m_prev, jnp.max(s, axis=-1, keepdims=True))
    alpha = jnp.exp2(m_prev - m_new)
    pr = jnp.exp2(s - m_new)
    l_ref[...] = alpha * l_ref[...] + jnp.sum(pr, axis=-1, keepdims=True)
    acc = alpha * acc_ref[...]
    prb = pr.astype(BF16)
    for h in range(HEADS):
        vh = jnp.concatenate([v_refs[p][pl.ds(h, page, stride=HEADS), :].astype(BF16) for p in range(npg)], axis=0)
        acc = acc + jnp.where(hrow_w == h, _dot(prb, vh), 0.0)
    acc_ref[...] = acc
    m_ref[...] = m_new

    @pl.when(s_idx == pl.num_programs(1) - 1)
    def _():
        s_new = jnp.sum(q8f * kn_ref[...], axis=-1, keepdims=True)
        m_fin = jnp.maximum(m_ref[...], s_new)
        a = jnp.exp2(m_ref[...] - m_fin)
        p_new = jnp.exp2(s_new - m_fin)
        l_fin = a * l_ref[...] + p_new
        vn = vn_ref[...]
        vn8 = jnp.zeros((N_MAPS, HEAD_W), F32)
        for h in range(HEADS):
            vn8 = jnp.where(hrow_w == h, jnp.broadcast_to(vn[:, h * HEAD_W:(h + 1) * HEAD_W], (N_MAPS, HEAD_W)), vn8)
        o_all = (a * acc_ref[...] + p_new * vn8) / l_fin
        lam = _diff_lambda(lw_ref[...], lam_init)
        hn = hn_ref[...]
        for h in range(HEADS):
            hs = slice(h * HEAD_W, (h + 1) * HEAD_W)
            o = _diff_finish(o_all[2 * h:2 * h + 1], o_all[2 * h + 1:2 * h + 2], lam, hn[:, hs], lam_init)
            o_ref[:, hs] = o.astype(BF16)


def _paged_attn(layer, page_table, lw, cq, ck, cv, hn, cache_kt, cache_v, lam_init):
    db, n_pages = page_table.shape
    page = cache_kt.shape[3]
    npg = PAGES_PER_STEP
    row = pl.BlockSpec((None, 1, SEG), lambda b, s, pt: (b, 0, 0))

    def page_spec(p):
        return pl.BlockSpec((None, None, SEG, page),
                            lambda b, s, pt: (layer, pt[b * n_pages + s * npg + p], 0, 0))

    r3 = lambda a: a.reshape(db, 1, SEG)
    grid_spec = pltpu.PrefetchScalarGridSpec(
        num_scalar_prefetch=1,
        grid=(db, n_pages // npg),
        in_specs=[pl.BlockSpec((None, 4, C_DQK), lambda b, s, pt: (layer, 0, 0)), row, row, row,
                  pl.BlockSpec((None, 1, SEG), lambda b, s, pt: (layer, 0, 0))]
                 + [page_spec(p) for p in range(npg)] * 2,
        out_specs=row,
        scratch_shapes=[pltpu.VMEM((N_MAPS, 1), F32), pltpu.VMEM((N_MAPS, 1), F32),
                        pltpu.VMEM((N_MAPS, HEAD_W), F32)],
    )
    return pl.pallas_call(
        functools.partial(_paged_attn_kernel, page=page, past_len=n_pages * page, lam_init=lam_init),
        grid_spec=grid_spec,
        out_shape=jax.ShapeDtypeStruct((db, 1, SEG), BF16),
        compiler_params=_cparams("parallel", "arbitrary"),
        name="paged_diff_attn",
    )(page_table.reshape(-1), lw, r3(cq), r3(ck), r3(cv), hn,
      *([cache_kt] * npg), *([cache_v] * npg))


def kernel(x_prompt, x_sample, mem_prompt, cache_diff_k, cache_diff_v, page_table, cache_mem_k, cache_mem_v, state_ret, norm_mix, w_in, ret_norm, cmlp_norm, cmlp_ws, cmlp_bs, diff_qn, diff_kn, diff_lambda_w, diff_hn, mem_norm, w_mem_kv, mem_qn, mem_kn, w_branch, w_out, norm_ffn, w_ffn_in, w_ffn_out):
    batch, seq, _ = x_prompt.shape
    db, dec_seq, _ = x_sample.shape
    assert dec_seq == 1, "the sample group decodes one token per sequence"
    depth = w_in.shape[0]
    n_mem = mem_prompt.shape[1]
    n_phys, page = cache_diff_k.shape[1:3]
    past_len = page_table.shape[1] * page
    n_main = N_SEG * SEG

    w9 = wgl = w_in.astype(BF16)
    wmem = w_mem_kv.astype(BF16)
    wb = w_branch.astype(BF16)
    wo = w_out.astype(BF16)
    wfi = w_ffn_in.astype(BF16)
    wfo = w_ffn_out.astype(BF16)
    bst = cmlp_bs.transpose(0, 2, 1)
    tile = lambda a, n: jnp.tile(a, (1, n)).reshape(depth, 1, SEG)
    qn, kn = tile(diff_qn, SEG // C_DQK), tile(diff_kn, SEG // C_DQK)
    mqn, mkn = tile(mem_qn, HEADS), tile(mem_kn, HEADS)
    row = lambda a: a.reshape(depth, 1, -1)
    g_mix, g_ffn, g_mem = row(norm_mix), row(norm_ffn), row(mem_norm)
    g_ret, g_cm, g_hn = row(ret_norm), row(cmlp_norm), row(diff_hn)

    cache_kt = cache_diff_k.transpose(0, 1, 3, 4, 5, 2).reshape(depth, n_phys, SEG, page)
    cache_v = cache_diff_v.reshape(depth, n_phys, page * HEADS, HEAD_W)
    mem_k_s = cache_mem_k.reshape(depth, db, n_mem * HEADS, M_DH)
    mem_v_s = cache_mem_v.reshape(depth, db, n_mem * HEADS, M_DH)
    state_s = state_ret.reshape(depth, db, HEADS * R_DK, HEAD_W)

    xp = x_prompt.reshape(batch * seq, D_MODEL)
    xs = x_sample.reshape(db, D_MODEL)
    mem = mem_prompt.reshape(batch * n_mem, D_MODEL)

    ckt_all = cv4_all = mk_all = mv_all = None

    pst = []
    sk, sv, sst, scv = [], [], [], []
    for l in range(depth):
        lam_init = 0.8 - 0.6 * math.exp(-0.3 * l)
        lw = diff_lambda_w

        rqk, rv, rg, o_cm, cq, ckt_all, ckb, cv4_all, cvb, mq = _proj_in_prompt(
            xp, g_mix, w9, g_cm, cmlp_ws, bst, qn, kn, mqn, ckt_all, cv4_all,
            layer=l, depth=depth, batch=batch, seq=seq)
        o_ret, st_p = _retention_prompt(l, rqk, rv, rg, g_ret, batch, seq)
        o_df = _diff_attn_prompt(l, lw, cq, ckb, cvb, g_hn, batch, seq, lam_init)
        mk_all, mv_all = _mem_kv(mem, g_mem, wmem, mkn, mk_all, mv_all, layer=l, depth=depth)
        o_mm = _mem_attn_prompt(l, mq, mk_all, mv_all, batch, seq, n_mem)
        xp = _merge(l, xp, g_mix, (o_ret, o_cm, o_df, o_mm), wgl, wb, wo)
        xp = _ffn(l, xp, g_ffn, wfi, wfo)
        pst.append(st_p.reshape(batch, HEADS, R_DK, HEAD_W))

        rqk, rv, rg, o_cm, vn_s, cq, ck, cv, mq = _proj_in_sample(
            xs, g_mix, w9, g_cm, cmlp_ws, bst, qn, kn, mqn, layer=l, off=past_len % CHUNK)
        o_ret, o_mm, st_s = _sample_mix(l, rqk, rv, rg, mq, state_s, mem_k_s, mem_v_s, g_ret, n_mem)
        o_df = _paged_attn(l, page_table, lw, cq, ck, cv, g_hn, cache_kt, cache_v, lam_init)
        xs = _merge(l, xs, g_mix, (o_ret.reshape(db, SEG), o_cm, o_df.reshape(db, SEG), o_mm.reshape(db, SEG)),
                    wgl, wb, wo)
        xs = _ffn(l, xs, g_ffn, wfi, wfo)
        sk.append(ck.reshape(db, 1, HEADS, 2, C_DQK))
        sv.append(cv.reshape(db, 1, HEADS, HEAD_W))
        sst.append(st_s.reshape(db, HEADS, R_DK, HEAD_W))
        scv.append(vn_s.reshape(db, 1, SEG))

    new_k = ckt_all.reshape(depth, batch, HEADS, 2, C_DQK, seq).transpose(0, 1, 5, 2, 3, 4)
    new_v = cv4_all.reshape(depth, batch, seq, HEADS, HEAD_W)
    new_mk = mk_all.reshape(depth, batch, n_mem, HEADS, M_DH)
    new_mv = mv_all.reshape(depth, batch, n_mem, HEADS, M_DH)
    return (xp.reshape(batch, seq, D_MODEL), xs.reshape(db, 1, D_MODEL),
            new_k, new_v, new_mk, new_mv, jnp.stack(pst),
            jnp.stack(sk), jnp.stack(sv), jnp.stack(sst), jnp.stack(scv))
```

```python
import functools
import math

import jax
import jax.numpy as jnp
from jax import lax
from jax.experimental import pallas as pl
from jax.experimental.pallas import tpu as pltpu

F32 = jnp.float32
BF16 = jnp.bfloat16

EPS = 1e-6
NEG_INF = -1e30
LOG2E = math.log2(math.e)

D_MODEL = 1024
SEG = 512
N_SEG = 9
HEADS = 4
R_DK = 64
HEAD_W = 128
CHUNK = 128
C_DQK = 64
M_DH = 128
N_BRANCH = 4
D_FF = 2816
FFN_CHUNKS = ((0, 1024), (1024, 2048), (2048, 2816))

VMEM_LIMIT_BYTES = 56 * 1024 * 1024


def _cparams(*sem):
    return pltpu.CompilerParams(dimension_semantics=sem, vmem_limit_bytes=VMEM_LIMIT_BYTES)


def _dot(a, b):
    return jnp.dot(a, b, preferred_element_type=F32)


def _dot_nt(a, b):
    return lax.dot_general(a, b, (((1,), (1,)), ((), ())), preferred_element_type=F32)


def _dot_tn(a, b):
    return lax.dot_general(a, b, (((0,), (0,)), ((), ())), preferred_element_type=F32)


def _sigmoid(x):
    return 1.0 / (1.0 + jnp.exp(-x))


def _rms(x, g):
    return x * lax.rsqrt(jnp.mean(x * x, axis=-1, keepdims=True) + EPS) * g


def _layer_norm(x, g):
    xc = x - jnp.mean(x, axis=-1, keepdims=True)
    return xc * lax.rsqrt(jnp.mean(xc * xc, axis=-1, keepdims=True) + EPS) * g


def _head_rms(y, g, group):
    lane = lax.broadcasted_iota(jnp.int32, (1, HEAD_W), 1)
    outs = []
    for hb in range(SEG // HEAD_W):
        blk = y[:, hb * HEAD_W:(hb + 1) * HEAD_W]
        sq = blk * blk
        if group == HEAD_W:
            ms = jnp.mean(sq, axis=-1, keepdims=True)
        else:
            lo = jnp.sum(jnp.where(lane < group, sq, 0.0), axis=-1, keepdims=True)
            hi = jnp.sum(jnp.where(lane < group, 0.0, sq), axis=-1, keepdims=True)
            ms = jnp.where(lane < group, lo, hi) * (1.0 / group)
        outs.append(blk * lax.rsqrt(ms + EPS))
    return jnp.concatenate(outs, axis=-1) * g


def _diff_lambda(wl, lam_init):
    a = jnp.sum(wl[0:1] * wl[1:2], axis=-1, keepdims=True)
    b = jnp.sum(wl[2:3] * wl[3:4], axis=-1, keepdims=True)
    return jnp.exp(a) - jnp.exp(b) + lam_init


def _layer_spec(layer, shape):
    nd = len(shape)
    return pl.BlockSpec((None,) + tuple(shape), lambda *_: (layer,) + (0,) * nd,
                        pipeline_mode=pl.Buffered(1))


def _proj_kernel(x_ref, g_ref, w_ref, cn_ref, ws_ref, bst_ref, qn_ref, kn_ref, mqn_ref, *rest,
                 prompt, tm, off, first=False):
    if prompt:
        rqk_o, rv_o, rg_o, ocm_o, cq_o, ckt_o, ckb_o, cv4_o, cvb_o, mq_o = rest if first else rest[2:]
    else:
        rqk_o, rv_o, rg_o, ocm_o, vn_o, cq_o, ck_o, cv_o, mq_o = rest
    hb = _rms(x_ref[...], g_ref[...]).astype(BF16)

    def seg(s):
        return _dot(hb, w_ref[:, s * SEG:(s + 1) * SEG])

    lane = lax.broadcasted_iota(jnp.int32, (1, SEG), 1)
    y = seg(0) * jnp.where(lane < HEADS * R_DK, 1.0, R_DK ** -0.5)
    rqk_o[...] = y.astype(rqk_o.dtype)
    rv_o[...] = seg(1)
    rg_o[...] = seg(2)

    gu = seg(3)
    vn = _layer_norm(seg(4), cn_ref[...])
    if prompt:
        vnb = vn.astype(BF16)
        row = lax.broadcasted_iota(jnp.int32, (CHUNK, CHUNK), 0)
        col = lax.broadcasted_iota(jnp.int32, (CHUNK, CHUNK), 1)
        for g in range(HEADS):
            gs = slice(g * HEAD_W, (g + 1) * HEAD_W)
            wt = jnp.where(row >= col, ws_ref[g], 0.0).astype(BF16)
            bcol = bst_ref[:, g:g + 1]
            for c in range(tm // CHUNK):
                cs = slice(c * CHUNK, (c + 1) * CHUNK)
                mix = _dot(wt, vnb[cs, gs]) + bcol
                ocm_o[cs, gs] = (gu[cs, gs] * mix).astype(BF16)
    else:
        vn_o[...] = vn
        for g in range(HEADS):
            gs = slice(g * HEAD_W, (g + 1) * HEAD_W)
            w00 = ws_ref[g][off:off + 1, off:off + 1]
            b0 = bst_ref[off:off + 1, g:g + 1]
            ocm_o[:, gs] = (gu[:, gs] * (w00 * vn[:, gs] + b0)).astype(BF16)

    cq_o[...] = (_head_rms(seg(5), qn_ref[...], C_DQK) * (C_DQK ** -0.5 * LOG2E)).astype(BF16)
    ck = _head_rms(seg(6), kn_ref[...], C_DQK)
    cv = seg(7)
    if prompt:
        if first:
            ckt_l, cv4_l = ckt_o.at[0], cv4_o.at[0]
            ckt_o[1:] = jnp.zeros((ckt_o.shape[0] - 1,) + ckt_o.shape[1:], F32)
            cv4_o[1:] = jnp.zeros((cv4_o.shape[0] - 1,) + cv4_o.shape[1:], F32)
        else:
            ckt_l, cv4_l = ckt_o, cv4_o
        ckt_l[...] = ck.T
        for h in range(HEADS):
            cv4_l[pl.ds(h, tm, stride=HEADS), :] = cv[:, h * HEAD_W:(h + 1) * HEAD_W]
        ckb_o[...] = ck.astype(BF16)
        cvb_o[...] = cv.astype(BF16)
    else:
        ck_o[...] = ck
        cv_o[...] = cv
    mq_o[...] = (_head_rms(seg(8), mqn_ref[...], M_DH) * (M_DH ** -0.5)).astype(BF16)


def _proj_in_specs(layer, tm):
    row = lambda i: (i, 0)
    return [pl.BlockSpec((tm, D_MODEL), row), _layer_spec(layer, (1, D_MODEL)),
            _layer_spec(layer, (D_MODEL, N_SEG * SEG)), _layer_spec(layer, (1, SEG)),
            _layer_spec(layer, (HEADS, CHUNK, CHUNK)), _layer_spec(layer, (CHUNK, HEADS)),
            _layer_spec(layer, (1, SEG)), _layer_spec(layer, (1, SEG)), _layer_spec(layer, (1, SEG))]


def _proj_in_sample(x, g, w9, cn, ws, bst, qn, kn, mqn, *, layer, off):
    m = x.shape[0]
    blk = pl.BlockSpec((m, SEG), lambda i: (i, 0))
    dts = (F32, F32, F32, BF16, F32, BF16, F32, F32, BF16)
    return pl.pallas_call(
        functools.partial(_proj_kernel, prompt=False, tm=m, off=off),
        grid=(1,),
        in_specs=_proj_in_specs(layer, m),
        out_specs=[blk] * len(dts),
        out_shape=[jax.ShapeDtypeStruct((m, SEG), dt) for dt in dts],
        compiler_params=_cparams("parallel"),
        name="proj_in_sample",
    )(x, g, w9, cn, ws, bst, qn, kn, mqn)


def _proj_in_prompt(x, g, w9, cn, ws, bst, qn, kn, mqn, ckt_all, cv4_all, *, layer, depth, batch, seq):
    m = x.shape[0]
    tm = 512
    nq = seq // tm
    first = ckt_all is None
    blk = pl.BlockSpec((tm, SEG), lambda i: (i, 0))
    lead, at = ((depth,), 0) if first else ((None,), layer)
    cktblk = pl.BlockSpec(lead + (None, SEG, tm), lambda i: (at, i // nq, 0, i % nq))
    cv4blk = pl.BlockSpec(lead + (tm * HEADS, HEAD_W), lambda i: (at, i, 0))
    sds = lambda dt: jax.ShapeDtypeStruct((m, SEG), dt)
    in_specs = _proj_in_specs(layer, tm)
    n_in = len(in_specs)
    buffers = () if first else (ckt_all, cv4_all)
    return pl.pallas_call(
        functools.partial(_proj_kernel, prompt=True, tm=tm, off=0, first=first),
        grid=(m // tm,),
        in_specs=in_specs + [pl.BlockSpec(memory_space=pl.ANY)] * len(buffers),
        out_specs=[blk, blk, blk, blk, blk, cktblk, blk, cv4blk, blk, blk],
        out_shape=[sds(BF16), sds(F32), sds(F32), sds(BF16), sds(BF16),
                   jax.ShapeDtypeStruct((depth, batch, SEG, seq), F32), sds(BF16),
                   jax.ShapeDtypeStruct((depth, m * HEADS, HEAD_W), F32), sds(BF16), sds(BF16)],
        input_output_aliases={} if first else {n_in: 5, n_in + 1: 7},
        compiler_params=_cparams("parallel"),
        name="proj_in_prompt",
    )(x, g, w9, cn, ws, bst, qn, kn, mqn, *buffers)


RET_BATCH = 8
RET_CHUNK = 256


def _ret_log_decay(h):
    return math.log1p(-(2.0 ** (-5 - h)))


def _ret_finish(o, gate, gn):
    return gate * _sigmoid(gate) * _layer_norm(o, gn)


def _retention_kernel(rqk_ref, rv_ref, rg_ref, gn_ref, o_ref, st_o, st_ref, dec_ref, qd_ref, kd_ref):
    c = pl.program_id(1)

    @pl.when(c == 0)
    def _():
        st_ref[...] = jnp.zeros_like(st_ref)
        i = lax.broadcasted_iota(jnp.int32, (RET_CHUNK, RET_CHUNK), 0).astype(F32)
        j = lax.broadcasted_iota(jnp.int32, (RET_CHUNK, RET_CHUNK), 1).astype(F32)
        t = lax.broadcasted_iota(jnp.int32, (RET_CHUNK, HEAD_W), 0).astype(F32)
        for h in range(HEADS):
            lg = _ret_log_decay(h)
            dec_ref[h] = jnp.where(i >= j, jnp.exp(lg * jnp.maximum(i - j, 0.0)), 0.0)
            qd_ref[h] = jnp.exp(lg * (t + 1.0))
            kd_ref[h] = jnp.exp(lg * (RET_CHUNK - 1.0 - t))

    gn = gn_ref[...]
    lane = lax.broadcasted_iota(jnp.int32, (1, HEADS * R_DK), 1)
    for b in range(RET_BATCH):
        qk = rqk_ref[b]
        q_all = qk[:, :HEADS * R_DK]
        k_all = qk[:, HEADS * R_DK:]
        v_all = rv_ref[b]
        rg = rg_ref[b]
        st = st_ref[b]
        stb = st.astype(BF16)
        zero = jnp.zeros_like(q_all)
        vk_parts = []
        for h in range(HEADS):
            hs = slice(h * HEAD_W, (h + 1) * HEAD_W)
            v = v_all[:, hs]
            qm = jnp.where((lane >= h * R_DK) & (lane < (h + 1) * R_DK), q_all, zero)
            s = _dot_nt(qm, k_all) * dec_ref[h]
            intra = _dot(s.astype(BF16), v.astype(BF16))
            cross = _dot(qm, stb) * qd_ref[h]
            o_ref[b, :, hs] = _ret_finish(intra + cross, rg[:, hs], gn[:, hs]).astype(BF16)
            vk_parts.append((v * kd_ref[h]).astype(BF16))
        kv = _dot_tn(k_all, jnp.concatenate(vk_parts, axis=-1))
        for h in range(HEADS):
            rs = slice(h * R_DK, (h + 1) * R_DK)
            st_ref[b, rs, :] = (st[rs, :] * math.exp(_ret_log_decay(h) * RET_CHUNK)
                                + kv[rs, h * HEAD_W:(h + 1) * HEAD_W])

    @pl.when(c == pl.num_programs(1) - 1)
    def _():
        st_o[...] = st_ref[...]


def _retention_prompt(layer, rqk, rv, rg, gn, batch, seq):
    nc = seq // RET_CHUNK
    blk = pl.BlockSpec((RET_BATCH, RET_CHUNK, SEG), lambda b, c: (b, c, 0))
    stblk = pl.BlockSpec((RET_BATCH, HEADS * R_DK, HEAD_W), lambda b, c: (b, 0, 0))
    tbl = pltpu.VMEM((HEADS, RET_CHUNK, HEAD_W), F32)
    r3 = lambda a: a.reshape(batch, seq, SEG)
    o_ret, st = pl.pallas_call(
        _retention_kernel,
        grid=(batch // RET_BATCH, nc),
        in_specs=[blk, blk, blk, _layer_spec(layer, (1, SEG))],
        out_specs=[blk, stblk],
        out_shape=[jax.ShapeDtypeStruct((batch, seq, SEG), BF16),
                   jax.ShapeDtypeStruct((batch, HEADS * R_DK, HEAD_W), F32)],
        scratch_shapes=[pltpu.VMEM((RET_BATCH, HEADS * R_DK, HEAD_W), F32),
                        pltpu.VMEM((HEADS, RET_CHUNK, RET_CHUNK), F32), tbl, tbl],
        compiler_params=_cparams("parallel", "arbitrary"),
        name="retention_prompt",
    )(r3(rqk), r3(rv), r3(rg), gn)
    return o_ret.reshape(batch * seq, SEG), st


def _diff_finish(o0, o1, lam, hn, lam_init):
    o = o0 - lam * o1
    return _rms(o, hn) * (1.0 - lam_init)


def _alibi_slope_log2(h, shape):
    return jnp.exp2(jnp.full(shape, -8.0 / HEADS, F32) * (h + 1).astype(F32)) * LOG2E


ONES_ROWS = 16


def _diff_attn_kernel(lw_ref, q_ref, k_ref, v_ref, hn_ref, o_ref, vt_ref, acc_ref, s_ref, *, tq, nblk, lam_init):
    h = pl.program_id(1)
    ones = jnp.ones((ONES_ROWS, tq), BF16)
    for t in range(nblk):
        vt_ref[t, :HEAD_W, :] = v_ref[t * tq:(t + 1) * tq, :].astype(F32).T.astype(BF16)
        vt_ref[t, HEAD_W:, :] = ones

    lane = lax.broadcasted_iota(jnp.int32, (1, HEAD_W), 1)
    krow = lax.broadcasted_iota(jnp.int32, (tq, HEAD_W), 0).astype(F32)
    brep = _alibi_slope_log2(h, (tq, HEAD_W)) * krow
    bias = jnp.concatenate([brep] * (tq // HEAD_W), axis=1)
    slope_row = _alibi_slope_log2(h, (1, tq))
    rowi = lax.broadcasted_iota(jnp.int32, (tq, tq), 0)
    coli = lax.broadcasted_iota(jnp.int32, (tq, tq), 1)
    lam = _diff_lambda(lw_ref[...], lam_init)
    hn = hn_ref[...]

    q_maps = {}

    def maps_of(i):
        if i not in q_maps:
            q = q_ref[i * tq:(i + 1) * tq, :]
            zero = jnp.zeros_like(q)
            q_maps[i] = (jnp.where(lane < C_DQK, q, zero), jnp.where(lane < C_DQK, zero, q))
        return q_maps[i]

    def scores(i, j, c, slot):
        s_ref[slot, c] = bias + _dot_nt(k_ref[j * tq:(j + 1) * tq, :], maps_of(i)[c])

    schedule = [(i, j) for i in range(nblk) for j in range(i + 1)]
    scores(0, 0, 0, 0)
    scores(0, 0, 1, 0)
    m = [None, None]
    for n, (i, j) in enumerate(schedule):
        slot = n % 2
        nxt = schedule[n + 1] if n + 1 < len(schedule) else None
        off = slope_row * float((j - i) * tq)
        for c in range(2):
            if nxt is not None:
                scores(nxt[0], nxt[1], c, 1 - slot)
            s = s_ref[slot, c]
            if j == i:
                s = jnp.where(coli >= rowi, s, NEG_INF)
            blk_max = jnp.max(s, axis=0, keepdims=True) + off
            if j == 0:
                m_new = blk_max
                acc_ref[c] = _dot(vt_ref[j], jnp.exp2(s - (m_new - off)).astype(BF16))
            else:
                m_new = jnp.maximum(m[c], blk_max)
                p = jnp.exp2(s - (m_new - off)).astype(BF16)
                acc_ref[c] = jnp.exp2(m[c] - m_new) * acc_ref[c] + _dot(vt_ref[j], p)
            m[c] = m_new
        if j == i:
            o0 = acc_ref[0, :HEAD_W, :] / acc_ref[0, HEAD_W:HEAD_W + 1, :]
            o1 = acc_ref[1, :HEAD_W, :] / acc_ref[1, HEAD_W:HEAD_W + 1, :]
            o = o0 - lam * o1
            y = o * lax.rsqrt(jnp.mean(o * o, axis=0, keepdims=True) + EPS)
            o_ref[i * tq:(i + 1) * tq, :] = (y.T * hn * (1.0 - lam_init)).astype(BF16)


def _diff_attn_prompt(layer, lw, cq, ckb, cvb, hn, batch, seq, lam_init):
    tq = 512
    nq = seq // tq
    blk = pl.BlockSpec((seq, HEAD_W), lambda b, h: (b, h))
    return pl.pallas_call(
        functools.partial(_diff_attn_kernel, tq=tq, nblk=nq, lam_init=lam_init),
        grid=(batch, HEADS),
        in_specs=[_layer_spec(layer, (4, C_DQK)), blk, blk, blk,
                  pl.BlockSpec((None, 1, HEAD_W), lambda b, h: (layer, 0, h))],
        out_specs=blk,
        out_shape=jax.ShapeDtypeStruct((batch * seq, SEG), BF16),
        scratch_shapes=[pltpu.VMEM((nq, HEAD_W + ONES_ROWS, tq), BF16),
                        pltpu.VMEM((2, HEAD_W + ONES_ROWS, tq), F32),
                        pltpu.VMEM((2, 2, tq, tq), F32)],
        compiler_params=_cparams("parallel", "parallel"),
        name="diff_attn_prompt",
    )(lw, cq, ckb, cvb, hn)


def _mem_kv_kernel(x_ref, g_ref, w_ref, kn_ref, *rest, tm, first):
    k_o, v_o = rest[-2:]
    hb = _rms(x_ref[...], g_ref[...]).astype(BF16)
    k = _head_rms(_dot(hb, w_ref[:, :SEG]), kn_ref[...], M_DH)
    v = _dot(hb, w_ref[:, SEG:])
    if first:
        k_l, v_l = k_o.at[0], v_o.at[0]
        k_o[1:] = jnp.zeros((k_o.shape[0] - 1,) + k_o.shape[1:], F32)
        v_o[1:] = jnp.zeros((v_o.shape[0] - 1,) + v_o.shape[1:], F32)
    else:
        k_l, v_l = k_o, v_o
    for h in range(HEADS):
        hs = slice(h * HEAD_W, (h + 1) * HEAD_W)
        k_l[pl.ds(h, tm, stride=HEADS), :] = k[:, hs]
        v_l[pl.ds(h, tm, stride=HEADS), :] = v[:, hs]


def _mem_kv(mem, g, w2, kn, mk_all, mv_all, *, layer, depth):
    m = mem.shape[0]
    tm = 512
    row = lambda i: (i, 0)
    first = mk_all is None
    lead, at = ((depth,), 0) if first else ((None,), layer)
    blk = pl.BlockSpec(lead + (tm * HEADS, M_DH), lambda i: (at, i, 0))
    buffers = () if first else (mk_all, mv_all)
    return pl.pallas_call(
        functools.partial(_mem_kv_kernel, tm=tm, first=first),
        grid=(m // tm,),
        in_specs=[pl.BlockSpec((tm, D_MODEL), row), _layer_spec(layer, (1, D_MODEL)),
                  _layer_spec(layer, (D_MODEL, 2 * SEG)), _layer_spec(layer, (1, SEG))]
                 + [pl.BlockSpec(memory_space=pl.ANY)] * len(buffers),
        out_specs=[blk, blk],
        out_shape=[jax.ShapeDtypeStruct((depth, m * HEADS, M_DH), F32)] * 2,
        input_output_aliases={} if first else {4: 0, 5: 1},
        compiler_params=_cparams("parallel"),
        name="mem_kv",
    )(mem, g, w2, kn, *buffers)


def _mem_attn_kernel(q_ref, k_ref, v_ref, o_ref, *, n_mem):
    heads = range(HEADS)
    hsl = [slice(h * HEAD_W, (h + 1) * HEAD_W) for h in heads]
    s = [_dot_nt(q_ref[:, hsl[h]], k_ref[pl.ds(h, n_mem, stride=HEADS), :].astype(BF16)) for h in heads]
    p = [jnp.exp(x - jnp.max(x, axis=-1, keepdims=True)) for x in s]
    o = [_dot(p[h].astype(BF16), v_ref[pl.ds(h, n_mem, stride=HEADS), :].astype(BF16)) for h in heads]
    for h in heads:
        o_ref[:, hsl[h]] = (o[h] / jnp.sum(p[h], axis=-1, keepdims=True)).astype(BF16)


def _mem_attn_prompt(layer, mq, mk_all, mv_all, batch, seq, n_mem):
    tq = 512
    nq = seq // tq
    qblk = pl.BlockSpec((tq, SEG), lambda i: (i, 0))
    kvblk = pl.BlockSpec((None, n_mem * HEADS, M_DH), lambda i: (layer, i // nq, 0))
    return pl.pallas_call(
        functools.partial(_mem_attn_kernel, n_mem=n_mem),
        grid=(batch * nq,),
        in_specs=[qblk, kvblk, kvblk],
        out_specs=qblk,
        out_shape=jax.ShapeDtypeStruct((batch * seq, SEG), BF16),
        compiler_params=_cparams("parallel"),
        name="mem_attn_prompt",
    )(mq, mk_all, mv_all)


GATE_SEGS = N_BRANCH * D_MODEL // SEG


def _merge_kernel(x_ref, g_ref, b0_ref, b1_ref, b2_ref, b3_ref, *rest):
    wg_refs = rest[:GATE_SEGS]
    wb_ref, wo_ref, o_ref = rest[GATE_SEGS:]
    per = GATE_SEGS // N_BRANCH
    x = x_ref[...]
    hb = _rms(x, g_ref[...]).astype(BF16)
    merged = None
    for n, b_ref in enumerate((b0_ref, b1_ref, b2_ref, b3_ref)):
        logits = jnp.concatenate([_dot(hb, wg_refs[n * per + k][...]) for k in range(per)], axis=1)
        gate = _sigmoid(logits)
        term = gate * _dot(b_ref[...], wb_ref[n])
        merged = term if merged is None else merged + term
    o_ref[...] = x + _dot(merged.astype(BF16), wo_ref[...])


def _merge(layer, x, g, branches, w_in, wb, wo):
    m = x.shape[0]
    tm = min(512, m)
    row = lambda i: (i, 0)
    xblk = pl.BlockSpec((tm, D_MODEL), row)
    bblk = pl.BlockSpec((tm, SEG), row)

    def gate_spec(k):
        return pl.BlockSpec((None, D_MODEL, SEG), lambda i: (layer, 0, N_SEG + k), pipeline_mode=pl.Buffered(1))

    return pl.pallas_call(
        _merge_kernel,
        grid=(m // tm,),
        in_specs=[xblk, _layer_spec(layer, (1, D_MODEL)), bblk, bblk, bblk, bblk]
                 + [gate_spec(k) for k in range(GATE_SEGS)]
                 + [_layer_spec(layer, (N_BRANCH, SEG, D_MODEL)), _layer_spec(layer, (D_MODEL, D_MODEL))],
        out_specs=xblk,
        out_shape=jax.ShapeDtypeStruct((m, D_MODEL), F32),
        compiler_params=_cparams("parallel"),
        name="merge",
    )(x, g, *branches, *([w_in] * GATE_SEGS), wb, wo)


def _ffn_kernel(x_ref, g_ref, wi_ref, wo_ref, o_ref):
    x = x_ref[...]
    hb = _rms(x, g_ref[...]).astype(BF16)
    acc = x
    for a, b in FFN_CHUNKS:
        gate = _dot(hb, wi_ref[:, a:b])
        up = _dot(hb, wi_ref[:, D_FF + a:D_FF + b])
        act = (gate * _sigmoid(gate) * up).astype(BF16)
        acc = acc + _dot(act, wo_ref[a:b, :])
    o_ref[...] = acc


def _ffn(layer, x, g, wi, wo):
    m = x.shape[0]
    tm = min(512, m)
    row = lambda i: (i, 0)
    xblk = pl.BlockSpec((tm, D_MODEL), row)
    return pl.pallas_call(
        _ffn_kernel,
        grid=(m // tm,),
        in_specs=[xblk, _layer_spec(layer, (1, D_MODEL)), _layer_spec(layer, (D_MODEL, 2 * D_FF)),
                  _layer_spec(layer, (D_FF, D_MODEL))],
        out_specs=xblk,
        out_shape=jax.ShapeDtypeStruct((m, D_MODEL), F32),
        compiler_params=_cparams("parallel"),
        name="ffn",
    )(x, g, wi, wo)


PAD_ROWS = 16
SAMPLE_BATCH = 8


def _sample_mix_kernel(rqk_ref, rv_ref, rg_ref, mq_ref, st_ref, mk_ref, mv_ref, gn_ref,
                       oret_o, omm_o, st_o, *, n_mem):
    gn = gn_ref[...]
    lane = lax.broadcasted_iota(jnp.int32, (1, HEADS * R_DK), 1)

    def first_row(a):
        r = lax.broadcasted_iota(jnp.int32, (PAD_ROWS, a.shape[1]), 0)
        return jnp.where(r == 0, jnp.broadcast_to(a, (PAD_ROWS, a.shape[1])), 0.0).astype(BF16)

    r16 = lax.broadcasted_iota(jnp.int32, (PAD_ROWS, HEADS * R_DK), 0)
    l16 = lax.broadcasted_iota(jnp.int32, (PAD_ROWS, HEADS * R_DK), 1)
    head_rows = (l16 >= r16 * R_DK) & (l16 < (r16 + 1) * R_DK)
    seqs = range(SAMPLE_BATCH)
    heads = range(HEADS)
    hsl = [slice(h * HEAD_W, (h + 1) * HEAD_W) for h in heads]

    q = [rqk_ref[b][:, :HEADS * R_DK] for b in seqs]
    k = [rqk_ref[b][:, HEADS * R_DK:] for b in seqs]
    v = [rv_ref[b] for b in seqs]
    st = [st_ref[b] for b in seqs]
    kv = [_dot_tn(first_row(k[b]), first_row(v[b])) for b in seqs]
    q4 = [jnp.where(head_rows, jnp.broadcast_to(q[b], (PAD_ROWS, HEADS * R_DK)), 0.0).astype(BF16) for b in seqs]
    cross = [_dot(q4[b], st[b].astype(BF16)) for b in seqs]
    mem_s = [[_dot_nt(jnp.broadcast_to(mq_ref[b][:, hsl[h]], (PAD_ROWS, HEAD_W)),
                      mk_ref[b, pl.ds(h, n_mem, stride=HEADS), :].astype(BF16)) for h in heads] for b in seqs]
    mem_p = [[jnp.exp(s - jnp.max(s, axis=-1, keepdims=True)) for s in row] for row in mem_s]
    mem_o = [[_dot(mem_p[b][h].astype(BF16), mv_ref[b, pl.ds(h, n_mem, stride=HEADS), :].astype(BF16))
              for h in heads] for b in seqs]

    for b in seqs:
        rg = rg_ref[b]
        for h in heads:
            rs = slice(h * R_DK, (h + 1) * R_DK)
            gamma = math.exp(_ret_log_decay(h))
            qm = jnp.where((lane >= h * R_DK) & (lane < (h + 1) * R_DK), q[b], 0.0)
            score = jnp.sum(qm * k[b], axis=-1, keepdims=True)
            o = score * v[b][:, hsl[h]] + cross[b][h:h + 1] * gamma
            oret_o[b, :, hsl[h]] = _ret_finish(o, rg[:, hsl[h]], gn[:, hsl[h]]).astype(BF16)
            st_o[b, rs, :] = st[b][rs, :] * gamma + kv[b][rs, hsl[h]]
            o_mem = mem_o[b][h] / jnp.sum(mem_p[b][h], axis=-1, keepdims=True)
            omm_o[b, :, hsl[h]] = o_mem[0:1].astype(BF16)


def _sample_mix(layer, rqk, rv, rg, mq, state, mem_k, mem_v, gn, n_mem):
    db = rqk.shape[0]
    sb = SAMPLE_BATCH
    row = pl.BlockSpec((sb, 1, SEG), lambda b: (b, 0, 0))
    stblk = pl.BlockSpec((None, sb, HEADS * R_DK, HEAD_W), lambda b: (layer, b, 0, 0))
    memblk = pl.BlockSpec((None, sb, n_mem * HEADS, M_DH), lambda b: (layer, b, 0, 0))
    r3 = lambda a: a.reshape(db, 1, SEG)
    return pl.pallas_call(
        functools.partial(_sample_mix_kernel, n_mem=n_mem),
        grid=(db // sb,),
        in_specs=[row, row, row, row, stblk, memblk, memblk, _layer_spec(layer, (1, SEG))],
        out_specs=[row, row, pl.BlockSpec((sb, HEADS * R_DK, HEAD_W), lambda b: (b, 0, 0))],
        out_shape=[jax.ShapeDtypeStruct((db, 1, SEG), BF16), jax.ShapeDtypeStruct((db, 1, SEG), BF16),
                   jax.ShapeDtypeStruct((db, HEADS * R_DK, HEAD_W), F32)],
        compiler_params=_cparams("parallel"),
        name="sample_mix",
    )(r3(rqk), r3(rv), r3(rg), r3(mq), state, mem_k, mem_v, gn)


N_MAPS = 2 * HEADS
PAGES_PER_STEP = 32


def _paged_attn_kernel(pt_ref, lw_ref, q_ref, kn_ref, vn_ref, hn_ref, *rest, page, past_len, lam_init):
    del pt_ref
    npg = PAGES_PER_STEP
    k_refs = rest[:npg]
    v_refs = rest[npg:2 * npg]
    o_ref, m_ref, l_ref, acc_ref = rest[2 * npg:]
    s_idx = pl.program_id(1)

    @pl.when(s_idx == 0)
    def _():
        m_ref[...] = jnp.full_like(m_ref, NEG_INF)
        l_ref[...] = jnp.zeros_like(l_ref)
        acc_ref[...] = jnp.zeros_like(acc_ref)

    r8 = lax.broadcasted_iota(jnp.int32, (N_MAPS, SEG), 0)
    l8 = lax.broadcasted_iota(jnp.int32, (N_MAPS, SEG), 1)
    sel = (l8 >= r8 * C_DQK) & (l8 < (r8 + 1) * C_DQK)
    q8f = jnp.where(sel, jnp.broadcast_to(q_ref[...].astype(F32), (N_MAPS, SEG)), 0.0)
    q8 = q8f.astype(BF16)
    hrow = jnp.right_shift(lax.broadcasted_iota(jnp.int32, (N_MAPS, 1), 0), 1)
    hrow_w = jnp.right_shift(lax.broadcasted_iota(jnp.int32, (N_MAPS, HEAD_W), 0), 1)
    slope = jnp.exp2((-8.0 / HEADS) * (hrow + 1).astype(F32)) * LOG2E
    kpos = s_idx * (npg * page) + lax.broadcasted_iota(jnp.int32, (1, npg * page), 1)
    bias = slope * (kpos - past_len).astype(F32)
    kt = jnp.concatenate([k_refs[p][...].astype(BF16) for p in range(npg)], axis=1)
    s = _dot(q8, kt) + bias
    m_prev = m_ref[...]
    m_new = jnp.maximum(m_prev, jnp.max(s, axis=-1, keepdims=True))
    alpha = jnp.exp2(m_prev - m_new)
    pr = jnp.exp2(s - m_new)
    l_ref[...] = alpha * l_ref[...] + jnp.sum(pr, axis=-1, keepdims=True)
    acc = alpha * acc_ref[...]
    prb = pr.astype(BF16)
    for h in range(HEADS):
        vh = jnp.concatenate([v_refs[p][pl.ds(h, page, stride=HEADS), :].astype(BF16) for p in range(npg)], axis=0)
        acc = acc + jnp.where(hrow_w == h, _dot(prb, vh), 0.0)
    acc_ref[...] = acc
    m_ref[...] = m_new

    @pl.when(s_idx == pl.num_programs(1) - 1)
    def _():
        s_new = jnp.sum(q8f * kn_ref[...], axis=-1, keepdims=True)
        m_fin = jnp.maximum(m_ref[...], s_new)
        a = jnp.exp2(m_ref[...] - m_fin)
        p_new = jnp.exp2(s_new - m_fin)
        l_fin = a * l_ref[...] + p_new
        vn = vn_ref[...]
        vn8 = jnp.zeros((N_MAPS, HEAD_W), F32)
        for h in range(HEADS):
            vn8 = jnp.where(hrow_w == h, jnp.broadcast_to(vn[:, h * HEAD_W:(h + 1) * HEAD_W], (N_MAPS, HEAD_W)), vn8)
        o_all = (a * acc_ref[...] + p_new * vn8) / l_fin
        lam = _diff_lambda(lw_ref[...], lam_init)
        hn = hn_ref[...]
        for h in range(HEADS):
            hs = slice(h * HEAD_W, (h + 1) * HEAD_W)
            o = _diff_finish(o_all[2 * h:2 * h + 1], o_all[2 * h + 1:2 * h + 2], lam, hn[:, hs], lam_init)
            o_ref[:, hs] = o.astype(BF16)


def _paged_attn(layer, page_table, lw, cq, ck, cv, hn, cache_kt, cache_v, lam_init):
    db, n_pages = page_table.shape
    page = cache_kt.shape[3]
    npg = PAGES_PER_STEP
    row = pl.BlockSpec((None, 1, SEG), lambda b, s, pt: (b, 0, 0))

    def page_spec(p):
        return pl.BlockSpec((None, None, SEG, page),
                            lambda b, s, pt: (layer, pt[b * n_pages + s * npg + p], 0, 0))

    r3 = lambda a: a.reshape(db, 1, SEG)
    grid_spec = pltpu.PrefetchScalarGridSpec(
        num_scalar_prefetch=1,
        grid=(db, n_pages // npg),
        in_specs=[pl.BlockSpec((None, 4, C_DQK), lambda b, s, pt: (layer, 0, 0)), row, row, row,
                  pl.BlockSpec((None, 1, SEG), lambda b, s, pt: (layer, 0, 0))]
                 + [page_spec(p) for p in range(npg)] * 2,
        out_specs=row,
        scratch_shapes=[pltpu.VMEM((N_MAPS, 1), F32), pltpu.VMEM((N_MAPS, 1), F32),
                        pltpu.VMEM((N_MAPS, HEAD_W), F32)],
    )
    return pl.pallas_call(
        functools.partial(_paged_attn_kernel, page=page, past_len=n_pages * page, lam_init=lam_init),
        grid_spec=grid_spec,
        out_shape=jax.ShapeDtypeStruct((db, 1, SEG), BF16),
        compiler_params=_cparams("parallel", "arbitrary"),
        name="paged_diff_attn",
    )(page_table.reshape(-1), lw, r3(cq), r3(ck), r3(cv), hn,
      *([cache_kt] * npg), *([cache_v] * npg))


def kernel(x_prompt, x_sample, mem_prompt, cache_diff_k, cache_diff_v, page_table, cache_mem_k, cache_mem_v, state_ret, norm_mix, w_in, ret_norm, cmlp_norm, cmlp_ws, cmlp_bs, diff_qn, diff_kn, diff_lambda_w, diff_hn, mem_norm, w_mem_kv, mem_qn, mem_kn, w_branch, w_out, norm_ffn, w_ffn_in, w_ffn_out):
    batch, seq, _ = x_prompt.shape
    db, dec_seq, _ = x_sample.shape
    assert dec_seq == 1, "the sample group decodes one token per sequence"
    depth = w_in.shape[0]
    n_mem = mem_prompt.shape[1]
    n_phys, page = cache_diff_k.shape[1:3]
    past_len = page_table.shape[1] * page

    w9 = wgl = w_in.astype(BF16)
    wmem = w_mem_kv.astype(BF16)
    wb = w_branch.astype(BF16)
    wo = w_out.astype(BF16)
    wfi = w_ffn_in.astype(BF16)
    wfo = w_ffn_out.astype(BF16)
    bst = cmlp_bs.transpose(0, 2, 1)
    tile = lambda a, n: jnp.tile(a, (1, n)).reshape(depth, 1, SEG)
    qn, kn = tile(diff_qn, SEG // C_DQK), tile(diff_kn, SEG // C_DQK)
    mqn, mkn = tile(mem_qn, HEADS), tile(mem_kn, HEADS)
    row = lambda a: a.reshape(depth, 1, -1)
    g_mix, g_ffn, g_mem = row(norm_mix), row(norm_ffn), row(mem_norm)
    g_ret, g_cm, g_hn = row(ret_norm), row(cmlp_norm), row(diff_hn)

    cache_kt = cache_diff_k.transpose(0, 1, 3, 4, 5, 2).reshape(depth, n_phys, SEG, page)
    cache_v = cache_diff_v.reshape(depth, n_phys, page * HEADS, HEAD_W)
    mem_k_s = cache_mem_k.reshape(depth, db, n_mem * HEADS, M_DH)
    mem_v_s = cache_mem_v.reshape(depth, db, n_mem * HEADS, M_DH)
    state_s = state_ret.reshape(depth, db, HEADS * R_DK, HEAD_W)

    xp = x_prompt.reshape(batch * seq, D_MODEL)
    xs = x_sample.reshape(db, D_MODEL)
    mem = mem_prompt.reshape(batch * n_mem, D_MODEL)

    ckt_all = cv4_all = mk_all = mv_all = None

    pst = []
    sk, sv, sst, scv = [], [], [], []
    for l in range(depth):
        lam_init = 0.8 - 0.6 * math.exp(-0.3 * l)
        lw = diff_lambda_w

        rqk, rv, rg, o_cm, cq, ckt_all, ckb, cv4_all, cvb, mq = _proj_in_prompt(
            xp, g_mix, w9, g_cm, cmlp_ws, bst, qn, kn, mqn, ckt_all, cv4_all,
            layer=l, depth=depth, batch=batch, seq=seq)
        o_ret, st_p = _retention_prompt(l, rqk, rv, rg, g_ret, batch, seq)
        o_df = _diff_attn_prompt(l, lw, cq, ckb, cvb, g_hn, batch, seq, lam_init)
        mk_all, mv_all = _mem_kv(mem, g_mem, wmem, mkn, mk_all, mv_all, layer=l, depth=depth)
        o_mm = _mem_attn_prompt(l, mq, mk_all, mv_all, batch, seq, n_mem)
        xp = _merge(l, xp, g_mix, (o_ret, o_cm, o_df, o_mm), wgl, wb, wo)
        xp = _ffn(l, xp, g_ffn, wfi, wfo)
        pst.append(st_p.reshape(batch, HEADS, R_DK, HEAD_W))

        rqk, rv, rg, o_cm, vn_s, cq, ck, cv, mq = _proj_in_sample(
            xs, g_mix, w9, g_cm, cmlp_ws, bst, qn, kn, mqn, layer=l, off=past_len % CHUNK)
        o_ret, o_mm, st_s = _sample_mix(l, rqk, rv, rg, mq, state_s, mem_k_s, mem_v_s, g_ret, n_mem)
        o_df = _paged_attn(l, page_table, lw, cq, ck, cv, g_hn, cache_kt, cache_v, lam_init)
        xs = _merge(l, xs, g_mix, (o_ret.reshape(db, SEG), o_cm, o_df.reshape(db, SEG), o_mm.reshape(db, SEG)),
                    wgl, wb, wo)
        xs = _ffn(l, xs, g_ffn, wfi, wfo)
        sk.append(ck.reshape(db, 1, HEADS, 2, C_DQK))
        sv.append(cv.reshape(db, 1, HEADS, HEAD_W))
        sst.append(st_s.reshape(db, HEADS, R_DK, HEAD_W))
        scv.append(vn_s.reshape(db, 1, SEG))

    new_k = ckt_all.reshape(depth, batch, HEADS, 2, C_DQK, seq).transpose(0, 1, 5, 2, 3, 4)
    new_v = cv4_all.reshape(depth, batch, seq, HEADS, HEAD_W)
    new_mk = mk_all.reshape(depth, batch, n_mem, HEADS, M_DH)
    new_mv = mv_all.reshape(depth, batch, n_mem, HEADS, M_DH)
    return (xp.reshape(batch, seq, D_MODEL), xs.reshape(db, 1, D_MODEL),
            new_k, new_v, new_mk, new_mv, jnp.stack(pst),
            jnp.stack(sk), jnp.stack(sv), jnp.stack(sst), jnp.stack(scv))
```

```python
import functools
import math

import jax
import jax.numpy as jnp
from jax import lax
from jax.experimental import pallas as pl
from jax.experimental.pallas import tpu as pltpu

F32 = jnp.float32
BF16 = jnp.bfloat16

EPS = 1e-6
NEG_INF = -1e30
LOG2E = math.log2(math.e)

D_MODEL = 1024
SEG = 512
N_SEG = 9
HEADS = 4
R_DK = 64
HEAD_W = 128
CHUNK = 128
C_DQK = 64
M_DH = 128
N_BRANCH = 4
D_FF = 2816
FFN_CHUNKS = ((0, 1024), (1024, 2048), (2048, 2816))

VMEM_LIMIT_BYTES = 56 * 1024 * 1024


def _cparams(*sem):
    return pltpu.CompilerParams(dimension_semantics=sem, vmem_limit_bytes=VMEM_LIMIT_BYTES)


def _dot(a, b):
    return jnp.dot(a, b, preferred_element_type=F32)


def _dot_nt(a, b):
    return lax.dot_general(a, b, (((1,), (1,)), ((), ())), preferred_element_type=F32)


def _dot_tn(a, b):
    return lax.dot_general(a, b, (((0,), (0,)), ((), ())), preferred_element_type=F32)


def _sigmoid(x):
    return 1.0 / (1.0 + jnp.exp(-x))


def _rms(x, g):
    return x * lax.rsqrt(jnp.mean(x * x, axis=-1, keepdims=True) + EPS) * g


def _layer_norm(x, g):
    xc = x - jnp.mean(x, axis=-1, keepdims=True)
    return xc * lax.rsqrt(jnp.mean(xc * xc, axis=-1, keepdims=True) + EPS) * g


def _head_rms(y, g, group):
    lane = lax.broadcasted_iota(jnp.int32, (1, HEAD_W), 1)
    outs = []
    for hb in range(SEG // HEAD_W):
        blk = y[:, hb * HEAD_W:(hb + 1) * HEAD_W]
        sq = blk * blk
        if group == HEAD_W:
            ms = jnp.mean(sq, axis=-1, keepdims=True)
        else:
            lo = jnp.sum(jnp.where(lane < group, sq, 0.0), axis=-1, keepdims=True)
            hi = jnp.sum(jnp.where(lane < group, 0.0, sq), axis=-1, keepdims=True)
            ms = jnp.where(lane < group, lo, hi) * (1.0 / group)
        outs.append(blk * lax.rsqrt(ms + EPS))
    return jnp.concatenate(outs, axis=-1) * g


def _diff_lambda(wl, lam_init):
    a = jnp.sum(wl[0:1] * wl[1:2], axis=-1, keepdims=True)
    b = jnp.sum(wl[2:3] * wl[3:4], axis=-1, keepdims=True)
    return jnp.exp(a) - jnp.exp(b) + lam_init


def _layer_spec(layer, shape):
    nd = len(shape)
    return pl.BlockSpec((None,) + tuple(shape), lambda *_: (layer,) + (0,) * nd,
                        pipeline_mode=pl.Buffered(1))


def _proj_kernel(x_ref, g_ref, w_ref, cn_ref, ws_ref, bst_ref, qn_ref, kn_ref, mqn_ref, *rest,
                 prompt, tm, off, first=False):
    if prompt:
        rqk_o, rv_o, rg_o, ocm_o, cq_o, ckt_o, ckb_o, cv4_o, cvb_o, mq_o = rest if first else rest[2:]
    else:
        rqk_o, rv_o, rg_o, ocm_o, vn_o, cq_o, ck_o, cv_o, mq_o = rest
    hb = _rms(x_ref[...], g_ref[...]).astype(BF16)

    def seg(s):
        return _dot(hb, w_ref[:, s * SEG:(s + 1) * SEG])

    lane = lax.broadcasted_iota(jnp.int32, (1, SEG), 1)
    y = seg(0) * jnp.where(lane < HEADS * R_DK, 1.0, R_DK ** -0.5)
    rqk_o[...] = y.astype(rqk_o.dtype)
    rv_o[...] = seg(1)
    rg_o[...] = seg(2)

    gu = seg(3)
    vn = _layer_norm(seg(4), cn_ref[...])
    if prompt:
        vnb = vn.astype(BF16)
        row = lax.broadcasted_iota(jnp.int32, (CHUNK, CHUNK), 0)
        col = lax.broadcasted_iota(jnp.int32, (CHUNK, CHUNK), 1)
        for g in range(HEADS):
            gs = slice(g * HEAD_W, (g + 1) * HEAD_W)
            wt = jnp.where(row >= col, ws_ref[g], 0.0).astype(BF16)
            bcol = bst_ref[:, g:g + 1]
            for c in range(tm // CHUNK):
                cs = slice(c * CHUNK, (c + 1) * CHUNK)
                mix = _dot(wt, vnb[cs, gs]) + bcol
                ocm_o[cs, gs] = (gu[cs, gs] * mix).astype(BF16)
    else:
        vn_o[...] = vn
        for g in range(HEADS):
            gs = slice(g * HEAD_W, (g + 1) * HEAD_W)
            w00 = ws_ref[g][off:off + 1, off:off + 1]
            b0 = bst_ref[off:off + 1, g:g + 1]
            ocm_o[:, gs] = (gu[:, gs] * (w00 * vn[:, gs] + b0)).astype(BF16)

    cq_o[...] = (_head_rms(seg(5), qn_ref[...], C_DQK) * (C_DQK ** -0.5 * LOG2E)).astype(BF16)
    ck = _head_rms(seg(6), kn_ref[...], C_DQK)
    cv = seg(7)
    if prompt:
        if first:
            ckt_l, cv4_l = ckt_o.at[0], cv4_o.at[0]
            ckt_o[1:] = jnp.zeros((ckt_o.shape[0] - 1,) + ckt_o.shape[1:], F32)
            cv4_o[1:] = jnp.zeros((cv4_o.shape[0] - 1,) + cv4_o.shape[1:], F32)
        else:
            ckt_l, cv4_l = ckt_o, cv4_o
        ckt_l[...] = ck.T
        for h in range(HEADS):
            cv4_l[pl.ds(h, tm, stride=HEADS), :] = cv[:, h * HEAD_W:(h + 1) * HEAD_W]
        ckb_o[...] = ck.astype(BF16)
        cvb_o[...] = cv.astype(BF16)
    else:
        ck_o[...] = ck
        cv_o[...] = cv
    mq_o[...] = (_head_rms(seg(8), mqn_ref[...], M_DH) * (M_DH ** -0.5)).astype(BF16)


def _proj_in_specs(layer, tm):
    row = lambda i: (i, 0)
    return [pl.BlockSpec((tm, D_MODEL), row), _layer_spec(layer, (1, D_MODEL)),
            _layer_spec(layer, (D_MODEL, N_SEG * SEG)), _layer_spec(layer, (1, SEG)),
            _layer_spec(layer, (HEADS, CHUNK, CHUNK)), _layer_spec(layer, (CHUNK, HEADS)),
            _layer_spec(layer, (1, SEG)), _layer_spec(layer, (1, SEG)), _layer_spec(layer, (1, SEG))]


def _proj_in_sample(x, g, w9, cn, ws, bst, qn, kn, mqn, *, layer, off):
    m = x.shape[0]
    blk = pl.BlockSpec((m, SEG), lambda i: (i, 0))
    dts = (F32, F32, F32, BF16, F32, BF16, F32, F32, BF16)
    return pl.pallas_call(
        functools.partial(_proj_kernel, prompt=False, tm=m, off=off),
        grid=(1,),
        in_specs=_proj_in_specs(layer, m),
        out_specs=[blk] * len(dts),
        out_shape=[jax.ShapeDtypeStruct((m, SEG), dt) for dt in dts],
        compiler_params=_cparams("parallel"),
        name="proj_in_sample",
    )(x, g, w9, cn, ws, bst, qn, kn, mqn)


def _proj_in_prompt(x, g, w9, cn, ws, bst, qn, kn, mqn, ckt_all, cv4_all, *, layer, depth, batch, seq):
    m = x.shape[0]
    tm = 512
    nq = seq // tm
    first = ckt_all is None
    blk = pl.BlockSpec((tm, SEG), lambda i: (i, 0))
    lead, at = ((depth,), 0) if first else ((None,), layer)
    cktblk = pl.BlockSpec(lead + (None, SEG, tm), lambda i: (at, i // nq, 0, i % nq))
    cv4blk = pl.BlockSpec(lead + (tm * HEADS, HEAD_W), lambda i: (at, i, 0))
    sds = lambda dt: jax.ShapeDtypeStruct((m, SEG), dt)
    in_specs = _proj_in_specs(layer, tm)
    n_in = len(in_specs)
    buffers = () if first else (ckt_all, cv4_all)
    return pl.pallas_call(
        functools.partial(_proj_kernel, prompt=True, tm=tm, off=0, first=first),
        grid=(m // tm,),
        in_specs=in_specs + [pl.BlockSpec(memory_space=pl.ANY)] * len(buffers),
        out_specs=[blk, blk, blk, blk, blk, cktblk, blk, cv4blk, blk, blk],
        out_shape=[sds(BF16), sds(F32), sds(F32), sds(BF16), sds(BF16),
                   jax.ShapeDtypeStruct((depth, batch, SEG, seq), F32), sds(BF16),
                   jax.ShapeDtypeStruct((depth, m * HEADS, HEAD_W), F32), sds(BF16), sds(BF16)],
        input_output_aliases={} if first else {n_in: 5, n_in + 1: 7},
        compiler_params=_cparams("parallel"),
        name="proj_in_prompt",
    )(x, g, w9, cn, ws, bst, qn, kn, mqn, *buffers)


RET_BATCH = 8
RET_CHUNK = 256


def _ret_log_decay(h):
    return math.log1p(-(2.0 ** (-5 - h)))


def _ret_finish(o, gate, gn):
    return gate * _sigmoid(gate) * _layer_norm(o, gn)


def _retention_kernel(rqk_ref, rv_ref, rg_ref, gn_ref, o_ref, st_o, st_ref, dec_ref, qd_ref, kd_ref):
    c = pl.program_id(1)

    @pl.when(c == 0)
    def _():
        st_ref[...] = jnp.zeros_like(st_ref)
        i = lax.broadcasted_iota(jnp.int32, (RET_CHUNK, RET_CHUNK), 0).astype(F32)
        j = lax.broadcasted_iota(jnp.int32, (RET_CHUNK, RET_CHUNK), 1).astype(F32)
        t = lax.broadcasted_iota(jnp.int32, (RET_CHUNK, HEAD_W), 0).astype(F32)
        for h in range(HEADS):
            lg = _ret_log_decay(h)
            dec_ref[h] = jnp.where(i >= j, jnp.exp(lg * jnp.maximum(i - j, 0.0)), 0.0)
            qd_ref[h] = jnp.exp(lg * (t + 1.0))
            kd_ref[h] = jnp.exp(lg * (RET_CHUNK - 1.0 - t))

    gn = gn_ref[...]
    lane = lax.broadcasted_iota(jnp.int32, (1, HEADS * R_DK), 1)
    for b in range(RET_BATCH):
        qk = rqk_ref[b]
        q_all = qk[:, :HEADS * R_DK]
        k_all = qk[:, HEADS * R_DK:]
        v_all = rv_ref[b]
        rg = rg_ref[b]
        st = st_ref[b]
        stb = st.astype(BF16)
        zero = jnp.zeros_like(q_all)
        vk_parts = []
        for h in range(HEADS):
            hs = slice(h * HEAD_W, (h + 1) * HEAD_W)
            v = v_all[:, hs]
            qm = jnp.where((lane >= h * R_DK) & (lane < (h + 1) * R_DK), q_all, zero)
            s = _dot_nt(qm, k_all) * dec_ref[h]
            intra = _dot(s.astype(BF16), v.astype(BF16))
            cross = _dot(qm, stb) * qd_ref[h]
            o_ref[b, :, hs] = _ret_finish(intra + cross, rg[:, hs], gn[:, hs]).astype(BF16)
            vk_parts.append((v * kd_ref[h]).astype(BF16))
        kv = _dot_tn(k_all, jnp.concatenate(vk_parts, axis=-1))
        for h in range(HEADS):
            rs = slice(h * R_DK, (h + 1) * R_DK)
            st_ref[b, rs, :] = (st[rs, :] * math.exp(_ret_log_decay(h) * RET_CHUNK)
                                + kv[rs, h * HEAD_W:(h + 1) * HEAD_W])

    @pl.when(c == pl.num_programs(1) - 1)
    def _():
        st_o[...] = st_ref[...]


def _retention_prompt(layer, rqk, rv, rg, gn, batch, seq):
    nc = seq // RET_CHUNK
    blk = pl.BlockSpec((RET_BATCH, RET_CHUNK, SEG), lambda b, c: (b, c, 0))
    stblk = pl.BlockSpec((RET_BATCH, HEADS * R_DK, HEAD_W), lambda b, c: (b, 0, 0))
    tbl = pltpu.VMEM((HEADS, RET_CHUNK, HEAD_W), F32)
    r3 = lambda a: a.reshape(batch, seq, SEG)
    o_ret, st = pl.pallas_call(
        _retention_kernel,
        grid=(batch // RET_BATCH, nc),
        in_specs=[blk, blk, blk, _layer_spec(layer, (1, SEG))],
        out_specs=[blk, stblk],
        out_shape=[jax.ShapeDtypeStruct((batch, seq, SEG), BF16),
                   jax.ShapeDtypeStruct((batch, HEADS * R_DK, HEAD_W), F32)],
        scratch_shapes=[pltpu.VMEM((RET_BATCH, HEADS * R_DK, HEAD_W), F32),
                        pltpu.VMEM((HEADS, RET_CHUNK, RET_CHUNK), F32), tbl, tbl],
        compiler_params=_cparams("parallel", "arbitrary"),
        name="retention_prompt",
    )(r3(rqk), r3(rv), r3(rg), gn)
    return o_ret.reshape(batch * seq, SEG), st


def _diff_finish(o0, o1, lam, hn, lam_init):
    o = o0 - lam * o1
    return _rms(o, hn) * (1.0 - lam_init)


def _alibi_slope_log2(h, shape):
    return jnp.exp2(jnp.full(shape, -8.0 / HEADS, F32) * (h + 1).astype(F32)) * LOG2E


ONES_ROWS = 16


def _diff_attn_kernel(lw_ref, q_ref, k_ref, v_ref, hn_ref, o_ref, vt_ref, acc_ref, s_ref, *, tq, nblk, lam_init):
    h = pl.program_id(1)
    ones = jnp.ones((ONES_ROWS, tq), BF16)
    for t in range(nblk):
        vt_ref[t, :HEAD_W, :] = v_ref[t * tq:(t + 1) * tq, :].astype(F32).T.astype(BF16)
        vt_ref[t, HEAD_W:, :] = ones

    lane = lax.broadcasted_iota(jnp.int32, (1, HEAD_W), 1)
    krow = lax.broadcasted_iota(jnp.int32, (tq, HEAD_W), 0).astype(F32)
    brep = _alibi_slope_log2(h, (tq, HEAD_W)) * krow
    bias = jnp.concatenate([brep] * (tq // HEAD_W), axis=1)
    slope_row = _alibi_slope_log2(h, (1, tq))
    rowi = lax.broadcasted_iota(jnp.int32, (tq, tq), 0)
    coli = lax.broadcasted_iota(jnp.int32, (tq, tq), 1)
    lam = _diff_lambda(lw_ref[...], lam_init)
    hn = hn_ref[...]

    q_maps = {}

    def maps_of(i):
        if i not in q_maps:
            q = q_ref[i * tq:(i + 1) * tq, :]
            zero = jnp.zeros_like(q)
            q_maps[i] = (jnp.where(lane < C_DQK, q, zero), jnp.where(lane < C_DQK, zero, q))
        return q_maps[i]

    def scores(i, j, c, slot):
        s_ref[slot, c] = bias + _dot_nt(k_ref[j * tq:(j + 1) * tq, :], maps_of(i)[c])

    schedule = [(i, j) for i in range(nblk) for j in range(i + 1)]
    scores(0, 0, 0, 0)
    scores(0, 0, 1, 0)
    m = [None, None]
    for n, (i, j) in enumerate(schedule):
        slot = n % 2
        nxt = schedule[n + 1] if n + 1 < len(schedule) else None
        off = slope_row * float((j - i) * tq)
        for c in range(2):
            if nxt is not None:
                scores(nxt[0], nxt[1], c, 1 - slot)
            s = s_ref[slot, c]
            if j == i:
                s = jnp.where(coli >= rowi, s, NEG_INF)
            blk_max = jnp.max(s, axis=0, keepdims=True) + off
            if j == 0:
                m_new = blk_max
                acc_ref[c] = _dot(vt_ref[j], jnp.exp2(s - (m_new - off)).astype(BF16))
            else:
                m_new = jnp.maximum(m[c], blk_max)
                p = jnp.exp2(s - (m_new - off)).astype(BF16)
                acc_ref[c] = jnp.exp2(m[c] - m_new) * acc_ref[c] + _dot(vt_ref[j], p)
            m[c] = m_new
        if j == i:
            o0 = acc_ref[0, :HEAD_W, :] / acc_ref[0, HEAD_W:HEAD_W + 1, :]
            o1 = acc_ref[1, :HEAD_W, :] / acc_ref[1, HEAD_W:HEAD_W + 1, :]
            o = o0 - lam * o1
            y = o * lax.rsqrt(jnp.mean(o * o, axis=0, keepdims=True) + EPS)
            o_ref[i * tq:(i + 1) * tq, :] = (y.T * hn * (1.0 - lam_init)).astype(BF16)


def _diff_attn_prompt(layer, lw, cq, ckb, cvb, hn, batch, seq, lam_init):
    tq = 512
    nq = seq // tq
    blk = pl.BlockSpec((seq, HEAD_W), lambda b, h: (b, h))
    return pl.pallas_call(
        functools.partial(_diff_attn_kernel, tq=tq, nblk=nq, lam_init=lam_init),
        grid=(batch, HEADS),
        in_specs=[_layer_spec(layer, (4, C_DQK)), blk, blk, blk,
                  pl.BlockSpec((None, 1, HEAD_W), lambda b, h: (layer, 0, h))],
        out_specs=blk,
        out_shape=jax.ShapeDtypeStruct((batch * seq, SEG), BF16),
        scratch_shapes=[pltpu.VMEM((nq, HEAD_W + ONES_ROWS, tq), BF16),
                        pltpu.VMEM((2, HEAD_W + ONES_ROWS, tq), F32),
                        pltpu.VMEM((2, 2, tq, tq), F32)],
        compiler_params=_cparams("parallel", "parallel"),
        name="diff_attn_prompt",
    )(lw, cq, ckb, cvb, hn)


def _mem_kv_kernel(x_ref, g_ref, w_ref, kn_ref, *rest, tm, first):
    k_o, v_o = rest[-2:]
    hb = _rms(x_ref[...], g_ref[...]).astype(BF16)
    k = _head_rms(_dot(hb, w_ref[:, :SEG]), kn_ref[...], M_DH)
    v = _dot(hb, w_ref[:, SEG:])
    if first:
        k_l, v_l = k_o.at[0], v_o.at[0]
        k_o[1:] = jnp.zeros((k_o.shape[0] - 1,) + k_o.shape[1:], F32)
        v_o[1:] = jnp.zeros((v_o.shape[0] - 1,) + v_o.shape[1:], F32)
    else:
        k_l, v_l = k_o, v_o
    for h in range(HEADS):
        hs = slice(h * HEAD_W, (h + 1) * HEAD_W)
        k_l[pl.ds(h, tm, stride=HEADS), :] = k[:, hs]
        v_l[pl.ds(h, tm, stride=HEADS), :] = v[:, hs]


def _mem_kv(mem, g, w2, kn, mk_all, mv_all, *, layer, depth):
    m = mem.shape[0]
    tm = 512
    row = lambda i: (i, 0)
    first = mk_all is None
    lead, at = ((depth,), 0) if first else ((None,), layer)
    blk = pl.BlockSpec(lead + (tm * HEADS, M_DH), lambda i: (at, i, 0))
    buffers = () if first else (mk_all, mv_all)
    return pl.pallas_call(
        functools.partial(_mem_kv_kernel, tm=tm, first=first),
        grid=(m // tm,),
        in_specs=[pl.BlockSpec((tm, D_MODEL), row), _layer_spec(layer, (1, D_MODEL)),
                  _layer_spec(layer, (D_MODEL, 2 * SEG)), _layer_spec(layer, (1, SEG))]
                 + [pl.BlockSpec(memory_space=pl.ANY)] * len(buffers),
        out_specs=[blk, blk],
        out_shape=[jax.ShapeDtypeStruct((depth, m * HEADS, M_DH), F32)] * 2,
        input_output_aliases={} if first else {4: 0, 5: 1},
        compiler_params=_cparams("parallel"),
        name="mem_kv",
    )(mem, g, w2, kn, *buffers)


def _mem_attend_head(qh, kh, vh):
    s = _dot_nt(qh, kh.astype(BF16))
    p = jnp.exp(s - jnp.max(s, axis=-1, keepdims=True))
    o = _dot(p.astype(BF16), vh.astype(BF16))
    return o / jnp.sum(p, axis=-1, keepdims=True)


def _mem_attn_kernel(q_ref, k_ref, v_ref, o_ref, *, n_mem):
    heads = range(HEADS)
    hsl = [slice(h * HEAD_W, (h + 1) * HEAD_W) for h in heads]
    s = [_dot_nt(q_ref[:, hsl[h]], k_ref[pl.ds(h, n_mem, stride=HEADS), :].astype(BF16)) for h in heads]
    p = [jnp.exp(x - jnp.max(x, axis=-1, keepdims=True)) for x in s]
    o = [_dot(p[h].astype(BF16), v_ref[pl.ds(h, n_mem, stride=HEADS), :].astype(BF16)) for h in heads]
    for h in heads:
        o_ref[:, hsl[h]] = (o[h] / jnp.sum(p[h], axis=-1, keepdims=True)).astype(BF16)


def _mem_attn_prompt(layer, mq, mk_all, mv_all, batch, seq, n_mem):
    tq = 512
    nq = seq // tq
    qblk = pl.BlockSpec((tq, SEG), lambda i: (i, 0))
    kvblk = pl.BlockSpec((None, n_mem * HEADS, M_DH), lambda i: (layer, i // nq, 0))
    return pl.pallas_call(
        functools.partial(_mem_attn_kernel, n_mem=n_mem),
        grid=(batch * nq,),
        in_specs=[qblk, kvblk, kvblk],
        out_specs=qblk,
        out_shape=jax.ShapeDtypeStruct((batch * seq, SEG), BF16),
        compiler_params=_cparams("parallel"),
        name="mem_attn_prompt",
    )(mq, mk_all, mv_all)


GATE_SEGS = N_BRANCH * D_MODEL // SEG


def _merge_kernel(x_ref, g_ref, b0_ref, b1_ref, b2_ref, b3_ref, *rest):
    wg_refs = rest[:GATE_SEGS]
    wb_ref, wo_ref, o_ref = rest[GATE_SEGS:]
    per = GATE_SEGS // N_BRANCH
    x = x_ref[...]
    hb = _rms(x, g_ref[...]).astype(BF16)
    merged = None
    for n, b_ref in enumerate((b0_ref, b1_ref, b2_ref, b3_ref)):
        logits = jnp.concatenate([_dot(hb, wg_refs[n * per + k][...]) for k in range(per)], axis=1)
        gate = _sigmoid(logits)
        term = gate * _dot(b_ref[...], wb_ref[n])
        merged = term if merged is None else merged + term
    o_ref[...] = x + _dot(merged.astype(BF16), wo_ref[...])


def _merge(layer, x, g, branches, w_in, wb, wo):
    m = x.shape[0]
    tm = min(512, m)
    row = lambda i: (i, 0)
    xblk = pl.BlockSpec((tm, D_MODEL), row)
    bblk = pl.BlockSpec((tm, SEG), row)

    def gate_spec(k):
        return pl.BlockSpec((None, D_MODEL, SEG), lambda i: (layer, 0, N_SEG + k), pipeline_mode=pl.Buffered(1))

    return pl.pallas_call(
        _merge_kernel,
        grid=(m // tm,),
        in_specs=[xblk, _layer_spec(layer, (1, D_MODEL)), bblk, bblk, bblk, bblk]
                 + [gate_spec(k) for k in range(GATE_SEGS)]
                 + [_layer_spec(layer, (N_BRANCH, SEG, D_MODEL)), _layer_spec(layer, (D_MODEL, D_MODEL))],
        out_specs=xblk,
        out_shape=jax.ShapeDtypeStruct((m, D_MODEL), F32),
        compiler_params=_cparams("parallel"),
        name="merge",
    )(x, g, *branches, *([w_in] * GATE_SEGS), wb, wo)


def _ffn_kernel(x_ref, g_ref, wi_ref, wo_ref, o_ref):
    x = x_ref[...]
    hb = _rms(x, g_ref[...]).astype(BF16)
    acc = x
    for a, b in FFN_CHUNKS:
        gate = _dot(hb, wi_ref[:, a:b])
        up = _dot(hb, wi_ref[:, D_FF + a:D_FF + b])
        act = (gate * _sigmoid(gate) * up).astype(BF16)
        acc = acc + _dot(act, wo_ref[a:b, :])
    o_ref[...] = acc


def _ffn(layer, x, g, wi, wo):
    m = x.shape[0]
    tm = min(512, m)
    row = lambda i: (i, 0)
    xblk = pl.BlockSpec((tm, D_MODEL), row)
    return pl.pallas_call(
        _ffn_kernel,
        grid=(m // tm,),
        in_specs=[xblk, _layer_spec(layer, (1, D_MODEL)), _layer_spec(layer, (D_MODEL, 2 * D_FF)),
                  _layer_spec(layer, (D_FF, D_MODEL))],
        out_specs=xblk,
        out_shape=jax.ShapeDtypeStruct((m, D_MODEL), F32),
        compiler_params=_cparams("parallel"),
        name="ffn",
    )(x, g, wi, wo)


FUSED_ROWS = 256


def _merge_ffn_kernel(x_ref, g_ref, g2_ref, b0_ref, b1_ref, b2_ref, b3_ref, *rest):
    wg_refs = rest[:GATE_SEGS]
    wb_ref, wo_ref, wi_ref, wfo_ref, o_ref = rest[GATE_SEGS:]
    per = GATE_SEGS // N_BRANCH
    x = x_ref[...]
    hb = _rms(x, g_ref[...]).astype(BF16)
    merged = None
    for n, b_ref in enumerate((b0_ref, b1_ref, b2_ref, b3_ref)):
        logits = jnp.concatenate([_dot(hb, wg_refs[n * per + k][...]) for k in range(per)], axis=1)
        term = _sigmoid(logits) * _dot(b_ref[...], wb_ref[n])
        merged = term if merged is None else merged + term
    x1 = x + _dot(merged.astype(BF16), wo_ref[...])
    hb2 = _rms(x1, g2_ref[...]).astype(BF16)
    acc = x1
    for a, b in FFN_CHUNKS:
        gate = _dot(hb2, wi_ref[:, a:b])
        up = _dot(hb2, wi_ref[:, D_FF + a:D_FF + b])
        act = (gate * _sigmoid(gate) * up).astype(BF16)
        acc = acc + _dot(act, wfo_ref[a:b, :])
    o_ref[...] = acc


def _merge_ffn(layer, x, g, g2, branches, w_in, wb, wo, wi, wfo):
    m = x.shape[0]
    tm = FUSED_ROWS
    row = lambda i: (i, 0)
    xblk = pl.BlockSpec((tm, D_MODEL), row)
    bblk = pl.BlockSpec((tm, SEG), row)

    def gate_spec(k):
        return pl.BlockSpec((None, D_MODEL, SEG), lambda i: (layer, 0, N_SEG + k), pipeline_mode=pl.Buffered(1))

    return pl.pallas_call(
        _merge_ffn_kernel,
        grid=(m // tm,),
        in_specs=[xblk, _layer_spec(layer, (1, D_MODEL)), _layer_spec(layer, (1, D_MODEL)), bblk, bblk, bblk, bblk]
                 + [gate_spec(k) for k in range(GATE_SEGS)]
                 + [_layer_spec(layer, (N_BRANCH, SEG, D_MODEL)), _layer_spec(layer, (D_MODEL, D_MODEL)),
                    _layer_spec(layer, (D_MODEL, 2 * D_FF)), _layer_spec(layer, (D_FF, D_MODEL))],
        out_specs=xblk,
        out_shape=jax.ShapeDtypeStruct((m, D_MODEL), F32),
        compiler_params=_cparams("parallel"),
        name="merge_ffn",
    )(x, g, g2, *branches, *([w_in] * GATE_SEGS), wb, wo, wi, wfo)


PAD_ROWS = 16
SAMPLE_BATCH = 4


def _sample_mix_kernel(rqk_ref, rv_ref, rg_ref, mq_ref, st_ref, mk_ref, mv_ref, gn_ref,
                       oret_o, omm_o, st_o, *, n_mem):
    gn = gn_ref[...]
    lane = lax.broadcasted_iota(jnp.int32, (1, HEADS * R_DK), 1)

    def first_row(a):
        r = lax.broadcasted_iota(jnp.int32, (PAD_ROWS, a.shape[1]), 0)
        return jnp.where(r == 0, jnp.broadcast_to(a, (PAD_ROWS, a.shape[1])), 0.0).astype(BF16)

    r16 = lax.broadcasted_iota(jnp.int32, (PAD_ROWS, HEADS * R_DK), 0)
    l16 = lax.broadcasted_iota(jnp.int32, (PAD_ROWS, HEADS * R_DK), 1)
    head_rows = (l16 >= r16 * R_DK) & (l16 < (r16 + 1) * R_DK)
    seqs = range(SAMPLE_BATCH)
    heads = range(HEADS)
    hsl = [slice(h * HEAD_W, (h + 1) * HEAD_W) for h in heads]

    q = [rqk_ref[b][:, :HEADS * R_DK] for b in seqs]
    k = [rqk_ref[b][:, HEADS * R_DK:] for b in seqs]
    v = [rv_ref[b] for b in seqs]
    st = [st_ref[b] for b in seqs]
    kv = [_dot_tn(first_row(k[b]), first_row(v[b])) for b in seqs]
    q4 = [jnp.where(head_rows, jnp.broadcast_to(q[b], (PAD_ROWS, HEADS * R_DK)), 0.0).astype(BF16) for b in seqs]
    cross = [_dot(q4[b], st[b].astype(BF16)) for b in seqs]
    mem_s = [[_dot_nt(jnp.broadcast_to(mq_ref[b][:, hsl[h]], (PAD_ROWS, HEAD_W)),
                      mk_ref[b, pl.ds(h, n_mem, stride=HEADS), :].astype(BF16)) for h in heads] for b in seqs]
    mem_p = [[jnp.exp(s - jnp.max(s, axis=-1, keepdims=True)) for s in row] for row in mem_s]
    mem_o = [[_dot(mem_p[b][h].astype(BF16), mv_ref[b, pl.ds(h, n_mem, stride=HEADS), :].astype(BF16))
              for h in heads] for b in seqs]

    for b in seqs:
        rg = rg_ref[b]
        for h in heads:
            rs = slice(h * R_DK, (h + 1) * R_DK)
            gamma = math.exp(_ret_log_decay(h))
            qm = jnp.where((lane >= h * R_DK) & (lane < (h + 1) * R_DK), q[b], 0.0)
            score = jnp.sum(qm * k[b], axis=-1, keepdims=True)
            o = score * v[b][:, hsl[h]] + cross[b][h:h + 1] * gamma
            oret_o[b, :, hsl[h]] = _ret_finish(o, rg[:, hsl[h]], gn[:, hsl[h]]).astype(BF16)
            st_o[b, rs, :] = st[b][rs, :] * gamma + kv[b][rs, hsl[h]]
            o_mem = mem_o[b][h] / jnp.sum(mem_p[b][h], axis=-1, keepdims=True)
            omm_o[b, :, hsl[h]] = o_mem[0:1].astype(BF16)


def _sample_mix(layer, rqk, rv, rg, mq, state, mem_k, mem_v, gn, n_mem):
    db = rqk.shape[0]
    sb = SAMPLE_BATCH
    row = pl.BlockSpec((sb, 1, SEG), lambda b: (b, 0, 0))
    stblk = pl.BlockSpec((None, sb, HEADS * R_DK, HEAD_W), lambda b: (layer, b, 0, 0))
    memblk = pl.BlockSpec((None, sb, n_mem * HEADS, M_DH), lambda b: (layer, b, 0, 0))
    r3 = lambda a: a.reshape(db, 1, SEG)
    return pl.pallas_call(
        functools.partial(_sample_mix_kernel, n_mem=n_mem),
        grid=(db // sb,),
        in_specs=[row, row, row, row, stblk, memblk, memblk, _layer_spec(layer, (1, SEG))],
        out_specs=[row, row, pl.BlockSpec((sb, HEADS * R_DK, HEAD_W), lambda b: (b, 0, 0))],
        out_shape=[jax.ShapeDtypeStruct((db, 1, SEG), BF16), jax.ShapeDtypeStruct((db, 1, SEG), BF16),
                   jax.ShapeDtypeStruct((db, HEADS * R_DK, HEAD_W), F32)],
        compiler_params=_cparams("parallel"),
        name="sample_mix",
    )(r3(rqk), r3(rv), r3(rg), r3(mq), state, mem_k, mem_v, gn)


N_MAPS = 2 * HEADS
PAGES_PER_STEP = 32


def _paged_attn_kernel(pt_ref, lw_ref, q_ref, kn_ref, vn_ref, hn_ref, *rest, page, past_len, lam_init):
    del pt_ref
    npg = PAGES_PER_STEP
    k_refs = rest[:npg]
    v_refs = rest[npg:2 * npg]
    o_ref, m_ref, l_ref, acc_ref = rest[2 * npg:]
    s_idx = pl.program_id(1)

    @pl.when(s_idx == 0)
    def _():
        m_ref[...] = jnp.full_like(m_ref, NEG_INF)
        l_ref[...] = jnp.zeros_like(l_ref)
        acc_ref[...] = jnp.zeros_like(acc_ref)

    r8 = lax.broadcasted_iota(jnp.int32, (N_MAPS, SEG), 0)
    l8 = lax.broadcasted_iota(jnp.int32, (N_MAPS, SEG), 1)
    sel = (l8 >= r8 * C_DQK) & (l8 < (r8 + 1) * C_DQK)
    q8f = jnp.where(sel, jnp.broadcast_to(q_ref[...].astype(F32), (N_MAPS, SEG)), 0.0)
    q8 = q8f.astype(BF16)
    hrow = jnp.right_shift(lax.broadcasted_iota(jnp.int32, (N_MAPS, 1), 0), 1)
    hrow_w = jnp.right_shift(lax.broadcasted_iota(jnp.int32, (N_MAPS, HEAD_W), 0), 1)
    slope = jnp.exp2((-8.0 / HEADS) * (hrow + 1).astype(F32)) * LOG2E
    kpos = s_idx * (npg * page) + lax.broadcasted_iota(jnp.int32, (1, npg * page), 1)
    bias = slope * (kpos - past_len).astype(F32)
    kt = jnp.concatenate([k_refs[p][...].astype(BF16) for p in range(npg)], axis=1)
    s = _dot(q8, kt) + bias
    m_prev = m_ref[...]
    m_new = jnp.maximum(m_prev, jnp.max(s, axis=-1, keepdims=True))
    alpha = jnp.exp2(m_prev - m_new)
    pr = jnp.exp2(s - m_new)
    l_ref[...] = alpha * l_ref[...] + jnp.sum(pr, axis=-1, keepdims=True)
    acc = alpha * acc_ref[...]
    prb = pr.astype(BF16)
    for h in range(HEADS):
        vh = jnp.concatenate([v_refs[p][pl.ds(h, page, stride=HEADS), :].astype(BF16) for p in range(npg)], axis=0)
        acc = acc + jnp.where(hrow_w == h, _dot(prb, vh), 0.0)
    acc_ref[...] = acc
    m_ref[...] = m_new

    @pl.when(s_idx == pl.num_programs(1) - 1)
    def _():
        s_new = jnp.sum(q8f * kn_ref[...], axis=-1, keepdims=True)
        m_fin = jnp.maximum(m_ref[...], s_new)
        a = jnp.exp2(m_ref[...] - m_fin)
        p_new = jnp.exp2(s_new - m_fin)
        l_fin = a * l_ref[...] + p_new
        vn = vn_ref[...]
        vn8 = jnp.zeros((N_MAPS, HEAD_W), F32)
        for h in range(HEADS):
            vn8 = jnp.where(hrow_w == h, jnp.broadcast_to(vn[:, h * HEAD_W:(h + 1) * HEAD_W], (N_MAPS, HEAD_W)), vn8)
        o_all = (a * acc_ref[...] + p_new * vn8) / l_fin
        lam = _diff_lambda(lw_ref[...], lam_init)
        hn = hn_ref[...]
        for h in range(HEADS):
            hs = slice(h * HEAD_W, (h + 1) * HEAD_W)
            o = _diff_finish(o_all[2 * h:2 * h + 1], o_all[2 * h + 1:2 * h + 2], lam, hn[:, hs], lam_init)
            o_ref[:, hs] = o.astype(BF16)


def _paged_attn(layer, page_table, lw, cq, ck, cv, hn, cache_kt, cache_v, lam_init):
    db, n_pages = page_table.shape
    page = cache_kt.shape[3]
    npg = PAGES_PER_STEP
    row = pl.BlockSpec((None, 1, SEG), lambda b, s, pt: (b, 0, 0))

    def page_spec(p):
        return pl.BlockSpec((None, None, SEG, page),
                            lambda b, s, pt: (layer, pt[b * n_pages + s * npg + p], 0, 0))

    r3 = lambda a: a.reshape(db, 1, SEG)
    grid_spec = pltpu.PrefetchScalarGridSpec(
        num_scalar_prefetch=1,
        grid=(db, n_pages // npg),
        in_specs=[pl.BlockSpec((None, 4, C_DQK), lambda b, s, pt: (layer, 0, 0)), row, row, row,
                  pl.BlockSpec((None, 1, SEG), lambda b, s, pt: (layer, 0, 0))]
                 + [page_spec(p) for p in range(npg)] * 2,
        out_specs=row,
        scratch_shapes=[pltpu.VMEM((N_MAPS, 1), F32), pltpu.VMEM((N_MAPS, 1), F32),
                        pltpu.VMEM((N_MAPS, HEAD_W), F32)],
    )
    return pl.pallas_call(
        functools.partial(_paged_attn_kernel, page=page, past_len=n_pages * page, lam_init=lam_init),
        grid_spec=grid_spec,
        out_shape=jax.ShapeDtypeStruct((db, 1, SEG), BF16),
        compiler_params=_cparams("parallel", "arbitrary"),
        name="paged_diff_attn",
    )(page_table.reshape(-1), lw, r3(cq), r3(ck), r3(cv), hn,
      *([cache_kt] * npg), *([cache_v] * npg))


def kernel(x_prompt, x_sample, mem_prompt, cache_diff_k, cache_diff_v, page_table, cache_mem_k, cache_mem_v, state_ret, norm_mix, w_in, ret_norm, cmlp_norm, cmlp_ws, cmlp_bs, diff_qn, diff_kn, diff_lambda_w, diff_hn, mem_norm, w_mem_kv, mem_qn, mem_kn, w_branch, w_out, norm_ffn, w_ffn_in, w_ffn_out):
    batch, seq, _ = x_prompt.shape
    db, dec_seq, _ = x_sample.shape
    assert dec_seq == 1, "the sample group decodes one token per sequence"
    depth = w_in.shape[0]
    n_mem = mem_prompt.shape[1]
    n_phys, page = cache_diff_k.shape[1:3]
    past_len = page_table.shape[1] * page
    n_main = N_SEG * SEG

    w9 = wgl = w_in.astype(BF16)
    wmem = w_mem_kv.astype(BF16)
    wb = w_branch.astype(BF16)
    wo = w_out.astype(BF16)
    wfi = w_ffn_in.astype(BF16)
    wfo = w_ffn_out.astype(BF16)
    bst = cmlp_bs.transpose(0, 2, 1)
    tile = lambda a, n: jnp.tile(a, (1, n)).reshape(depth, 1, SEG)
    qn, kn = tile(diff_qn, SEG // C_DQK), tile(diff_kn, SEG // C_DQK)
    mqn, mkn = tile(mem_qn, HEADS), tile(mem_kn, HEADS)
    row = lambda a: a.reshape(depth, 1, -1)
    g_mix, g_ffn, g_mem = row(norm_mix), row(norm_ffn), row(mem_norm)
    g_ret, g_cm, g_hn = row(ret_norm), row(cmlp_norm), row(diff_hn)

    cache_kt = cache_diff_k.transpose(0, 1, 3, 4, 5, 2).reshape(depth, n_phys, SEG, page)
    cache_v = cache_diff_v.reshape(depth, n_phys, page * HEADS, HEAD_W)
    mem_k_s = cache_mem_k.reshape(depth, db, n_mem * HEADS, M_DH)
    mem_v_s = cache_mem_v.reshape(depth, db, n_mem * HEADS, M_DH)
    state_s = state_ret.reshape(depth, db, HEADS * R_DK, HEAD_W)

    xp = x_prompt.reshape(batch * seq, D_MODEL)
    xs = x_sample.reshape(db, D_MODEL)
    mem = mem_prompt.reshape(batch * n_mem, D_MODEL)

    ckt_all = cv4_all = mk_all = mv_all = None

    pst = []
    sk, sv, sst, scv = [], [], [], []
    for l in range(depth):
        lam_init = 0.8 - 0.6 * math.exp(-0.3 * l)
        lw = diff_lambda_w

        rqk, rv, rg, o_cm, cq, ckt_all, ckb, cv4_all, cvb, mq = _proj_in_prompt(
            xp, g_mix, w9, g_cm, cmlp_ws, bst, qn, kn, mqn, ckt_all, cv4_all,
            layer=l, depth=depth, batch=batch, seq=seq)
        o_ret, st_p = _retention_prompt(l, rqk, rv, rg, g_ret, batch, seq)
        o_df = _diff_attn_prompt(l, lw, cq, ckb, cvb, g_hn, batch, seq, lam_init)
        mk_all, mv_all = _mem_kv(mem, g_mem, wmem, mkn, mk_all, mv_all, layer=l, depth=depth)
        o_mm = _mem_attn_prompt(l, mq, mk_all, mv_all, batch, seq, n_mem)
        xp = _merge_ffn(l, xp, g_mix, g_ffn, (o_ret, o_cm, o_df, o_mm), wgl, wb, wo, wfi, wfo)
        pst.append(st_p.reshape(batch, HEADS, R_DK, HEAD_W))

        rqk, rv, rg, o_cm, vn_s, cq, ck, cv, mq = _proj_in_sample(
            xs, g_mix, w9, g_cm, cmlp_ws, bst, qn, kn, mqn, layer=l, off=past_len % CHUNK)
        o_ret, o_mm, st_s = _sample_mix(l, rqk, rv, rg, mq, state_s, mem_k_s, mem_v_s, g_ret, n_mem)
        o_df = _paged_attn(l, page_table, lw, cq, ck, cv, g_hn, cache_kt, cache_v, lam_init)
        xs = _merge(l, xs, g_mix, (o_ret.reshape(db, SEG), o_cm, o_df.reshape(db, SEG), o_mm.reshape(db, SEG)),
                    wgl, wb, wo)
        xs = _ffn(l, xs, g_ffn, wfi, wfo)
        sk.append(ck.reshape(db, 1, HEADS, 2, C_DQK))
        sv.append(cv.reshape(db, 1, HEADS, HEAD_W))
        sst.append(st_s.reshape(db, HEADS, R_DK, HEAD_W))
        scv.append(vn_s.reshape(db, 1, SEG))

    new_k = ckt_all.reshape(depth, batch, HEADS, 2, C_DQK, seq).transpose(0, 1, 5, 2, 3, 4)
    new_v = cv4_all.reshape(depth, batch, seq, HEADS, HEAD_W)
    new_mk = mk_all.reshape(depth, batch, n_mem, HEADS, M_DH)
    new_mv = mv_all.reshape(depth, batch, n_mem, HEADS, M_DH)
    return (xp.reshape(batch, seq, D_MODEL), xs.reshape(db, 1, D_MODEL),
            new_k, new_v, new_mk, new_mv, jnp.stack(pst),
            jnp.stack(sk), jnp.stack(sv), jnp.stack(sst), jnp.stack(scv))
```
